```python
import math
import jax
import jax.numpy as jnp
from jax import lax
import numpy as np

D_MODEL = 4096
BATCH = 4
SEQ = 2048
DEPTH = 1
DEC_BATCH = 32
DEC_SEQ = 16
PAST_LEN = 1024

CHUNK = 64
D_SSM = D_MODEL // 2
SSM_GROUP = 16
N_SSM_GROUPS = D_SSM // SSM_GROUP
SSM_STATE = 64
FOX_HEAD_DIM = 128
D_FOX = D_MODEL // 2
N_FOX_HEADS = D_FOX // FOX_HEAD_DIM
Q_BLOCK = 128
ATTN_SCALE = FOX_HEAD_DIM ** -0.5
N_EXPERTS = 64
TOP_K = 8
EXPERT_FF = D_MODEL // 4
SHARED_FF = D_MODEL // 4
ROUTED_SCALE = 2.5
MOE_BLOCK = 128
NORM_EPS = 1e-6
DT_MIN = 0.001
DT_MAX = 0.1
FORGET_BIAS_INIT = 4.0
PROJ_SPLITS = (D_SSM, D_SSM + D_FOX, D_SSM + 2 * D_FOX, D_SSM + 3 * D_FOX,
               D_SSM + 3 * D_FOX + N_FOX_HEADS, D_SSM + 3 * D_FOX + N_FOX_HEADS + D_MODEL)
D_IN_PROJ = D_SSM + 3 * D_FOX + N_FOX_HEADS + 2 * D_MODEL

kernel_name = "hybrid_s5_fox_moe_streaming_step"


def rmsnorm(x, g):
    xf = x.astype(jnp.float32)
    r = lax.rsqrt(jnp.mean(xf * xf, axis=-1, keepdims=True) + NORM_EPS)
    return (xf * r).astype(x.dtype) * g


def swiglu(x, w_gate, w_up, w_down):
    return (jax.nn.silu(x @ w_gate) * (x @ w_up)) @ w_down


def project_in(h, w_in, b_forget):
    bsz, L, _ = h.shape
    z = jnp.einsum('bld,dn->bln', h, w_in)
    u, q, k, v, f_logit, g_a, g_b = jnp.split(z, PROJ_SPLITS, axis=-1)
    heads = (bsz, L, N_FOX_HEADS, FOX_HEAD_DIM)
    logf = jax.nn.log_sigmoid(f_logit.astype(jnp.float32) + b_forget.astype(jnp.float32))
    return u, q.reshape(heads), k.reshape(heads), v.reshape(heads), logf, g_a, g_b


def s5_params(a_re, a_im, log_dt, b_re, b_im, c_re, c_im):
    f32 = jnp.float32
    lam = lax.complex(jnp.minimum(a_re.astype(f32), -1e-4), a_im.astype(f32))
    lam_dt = lam * jnp.exp(log_dt.astype(f32))[:, None]
    lam_bar = jnp.exp(lam_dt)
    b_bar = ((lam_bar - 1.0) / lam)[:, :, None] * lax.complex(b_re.astype(f32), b_im.astype(f32))
    c = lax.complex(c_re.astype(f32), c_im.astype(f32))
    return lam_dt, lam_bar, b_bar, c


def _linear_recurrence(e1, e2):
    a1, b1 = e1
    a2, b2 = e2
    return a2 * a1, a2 * b1 + b2


def s5_mixer(u, h0, lam_dt, lam_bar, b_bar, c, d_skip):
    bsz, L, _ = u.shape
    uf = u.astype(jnp.float32).reshape(bsz, L, N_SSM_GROUPS, SSM_GROUP)
    bu = jnp.einsum('gph,blgh->blgp', b_bar, uf.astype(jnp.complex64))
    a = jnp.broadcast_to(lam_bar, bu.shape)
    _, states = lax.associative_scan(_linear_recurrence, (a, bu), axis=1)
    if h0 is not None:
        steps = jnp.arange(1, L + 1, dtype=jnp.float32)[:, None, None]
        states = states + jnp.exp(lam_dt[None] * steps)[None] * h0[:, None]
    y = jnp.einsum('ghp,blgp->blgh', c, states).real + d_skip.astype(jnp.float32).reshape(N_SSM_GROUPS, SSM_GROUP) * uf
    return y.reshape(bsz, L, D_SSM).astype(u.dtype), states[:, -1]


def fox_attend(q, c_q, q_pos, k, v, c_k):
    f32 = jnp.float32
    s = jnp.einsum('bqhd,bkhd->bhqk', q.astype(f32), k.astype(f32)) * ATTN_SCALE
    s = s + jnp.swapaxes(c_q, 1, 2)[..., None] - jnp.swapaxes(c_k, 1, 2)[:, :, None, :]
    k_pos = jnp.arange(k.shape[1], dtype=jnp.int32)
    s = jnp.where(k_pos[None, :] <= q_pos[:, None], s, -jnp.inf)
    p = jax.nn.softmax(s, axis=-1)
    return jnp.einsum('bhqk,bkhd->bqhd', p, v.astype(f32)).astype(v.dtype)


def fox_prompt(q, k, v, logf):
    bsz, L, H, Dh = q.shape
    c = jnp.cumsum(logf, axis=1)
    nb = L // Q_BLOCK
    q_blocks = jnp.moveaxis(q.reshape(bsz, nb, Q_BLOCK, H, Dh), 1, 0)
    c_blocks = jnp.moveaxis(c.reshape(bsz, nb, Q_BLOCK, H), 1, 0)
    pos_blocks = jnp.arange(L, dtype=jnp.int32).reshape(nb, Q_BLOCK)
    out = lax.map(lambda blk: fox_attend(blk[0], blk[1], blk[2], k, v, c), (q_blocks, c_blocks, pos_blocks))
    return jnp.moveaxis(out, 0, 1).reshape(bsz, L, H, Dh)


def merge_branches(y_ssm, attn, g_a, g_b, w_glu, w_fox_out, w_out):
    g = jax.nn.gelu(y_ssm)
    val, gate = jnp.split(jnp.einsum('bln,nd->bld', g, w_glu), 2, axis=-1)
    y_a = val * jax.nn.sigmoid(gate)
    y_b = jnp.einsum('bln,nd->bld', attn.reshape(attn.shape[0], attn.shape[1], D_FOX), w_fox_out)
    return jnp.einsum('bld,de->ble', jax.nn.sigmoid(g_a) * y_a + jax.nn.sigmoid(g_b) * y_b, w_out)


def routed_experts(h, idx, gates, w_gate, w_up, w_down):
    n_tok = h.shape[0]
    n_assign = n_tok * TOP_K
    n_blocks = (n_assign + N_EXPERTS * (MOE_BLOCK - 1) + MOE_BLOCK - 1) // MOE_BLOCK
    n_slots = n_blocks * MOE_BLOCK
    flat_e = idx.reshape(n_assign).astype(jnp.int32)
    order = jnp.argsort(flat_e)
    e_sorted = flat_e[order]
    counts = jnp.zeros((N_EXPERTS,), jnp.int32).at[flat_e].add(1)
    padded = (counts + MOE_BLOCK - 1) // MOE_BLOCK * MOE_BLOCK
    pad_end = jnp.cumsum(padded)
    pad_start = pad_end - padded
    start = jnp.cumsum(counts) - counts
    dest = pad_start[e_sorted] + jnp.arange(n_assign, dtype=jnp.int32) - start[e_sorted]
    slot_tok = jnp.full((n_slots,), n_tok, jnp.int32).at[dest].set((order // TOP_K).astype(jnp.int32))
    slot_gate = jnp.zeros((n_slots,), gates.dtype).at[dest].set(gates.reshape(n_assign)[order])
    block_e = jnp.minimum(jnp.searchsorted(pad_end, jnp.arange(n_blocks, dtype=jnp.int32) * MOE_BLOCK, side='right'),
                          N_EXPERTS - 1).astype(jnp.int32)
    h_pad = jnp.concatenate([h, jnp.zeros((1, h.shape[1]), h.dtype)], axis=0)

    def one_block(blk):
        tok_b, gate_b, e = blk
        return swiglu(h_pad[tok_b], w_gate[e], w_up[e], w_down[e]) * gate_b[:, None]

    out = lax.map(one_block, (slot_tok.reshape(n_blocks, MOE_BLOCK), slot_gate.reshape(n_blocks, MOE_BLOCK), block_e))
    return jax.ops.segment_sum(out.reshape(n_slots, h.shape[1]), slot_tok, num_segments=n_tok + 1)[:n_tok]


def ffn_residual(x, norm_g, w_router, router_bias, w_gate, w_up, w_down, ws_gate, ws_up, ws_down):
    bsz, L, _ = x.shape
    h = rmsnorm(x, norm_g).reshape(bsz * L, D_MODEL)
    scores = jax.nn.sigmoid(jnp.einsum('td,de->te', h.astype(jnp.float32), w_router.astype(jnp.float32)))
    _, idx = lax.top_k(scores + router_bias.astype(jnp.float32), TOP_K)
    sel = jnp.take_along_axis(scores, idx, axis=-1)
    gates = (ROUTED_SCALE * sel / jnp.sum(sel, axis=-1, keepdims=True)).astype(h.dtype)
    y = swiglu(h, ws_gate, ws_up, ws_down) + routed_experts(h, idx, gates, w_gate, w_up, w_down)
    return x + y.reshape(bsz, L, D_MODEL)


def setup_inputs(seed: int = 0) -> dict:
    key = jax.random.key(seed)
    ks = jax.random.split(key, 32)
    f32 = jnp.float32

    def nrm(k, shape, scale):
        return jax.random.normal(k, shape, f32) * scale

    G, P, HG = N_SSM_GROUPS, SSM_STATE, SSM_GROUP
    H, Dh = N_FOX_HEADS, FOX_HEAD_DIM
    n_idx = jnp.arange(P, dtype=f32)
    return {
        "x_prompt": nrm(ks[0], (BATCH, SEQ, D_MODEL), 1.0),
        "x_sample": nrm(ks[1], (DEC_BATCH, DEC_SEQ, D_MODEL), 1.0),
        "cache_k": nrm(ks[2], (DEPTH, DEC_BATCH, PAST_LEN, H, Dh), 1.0),
        "cache_v": nrm(ks[3], (DEPTH, DEC_BATCH, PAST_LEN, H, Dh), 1.0),
        "cache_logf": jax.nn.log_sigmoid(FORGET_BIAS_INIT + nrm(ks[4], (DEPTH, DEC_BATCH, PAST_LEN, H), 1.0)),
        "state_ssm_re": nrm(ks[5], (DEPTH, DEC_BATCH, G, P), 0.1),
        "state_ssm_im": nrm(ks[6], (DEPTH, DEC_BATCH, G, P), 0.1),
        "norm_mix": 1.0 + nrm(ks[7], (DEPTH, D_MODEL), 0.02),
        "w_in": nrm(ks[8], (DEPTH, D_MODEL, D_IN_PROJ), D_MODEL ** -0.5),
        "b_forget": FORGET_BIAS_INIT + nrm(ks[9], (DEPTH, H), 0.1),
        "ssm_a_re": -0.5 + nrm(ks[10], (DEPTH, G, P), 0.01),
        "ssm_a_im": math.pi * n_idx + nrm(ks[11], (DEPTH, G, P), 0.01),
        "ssm_log_dt": jax.random.uniform(ks[12], (DEPTH, G), f32, math.log(DT_MIN), math.log(DT_MAX)),
        "ssm_b_re": nrm(ks[13], (DEPTH, G, P, HG), (2.0 * HG) ** -0.5),
        "ssm_b_im": nrm(ks[14], (DEPTH, G, P, HG), (2.0 * HG) ** -0.5),
        "ssm_c_re": nrm(ks[15], (DEPTH, G, HG, P), (2.0 * P) ** -0.5),
        "ssm_c_im": nrm(ks[16], (DEPTH, G, HG, P), (2.0 * P) ** -0.5),
        "ssm_d": nrm(ks[17], (DEPTH, D_SSM), 0.5),
        "w_glu": nrm(ks[18], (DEPTH, D_SSM, 2 * D_MODEL), D_SSM ** -0.5),
        "w_fox_out": nrm(ks[19], (DEPTH, D_FOX, D_MODEL), D_FOX ** -0.5),
        "w_out": nrm(ks[20], (DEPTH, D_MODEL, D_MODEL), D_MODEL ** -0.5),
        "norm_ffn": 1.0 + nrm(ks[21], (DEPTH, D_MODEL), 0.02),
        "w_router": nrm(ks[22], (DEPTH, D_MODEL, N_EXPERTS), D_MODEL ** -0.5),
        "router_bias": nrm(ks[23], (DEPTH, N_EXPERTS), 0.01),
        "w_exp_gate": nrm(ks[24], (DEPTH, N_EXPERTS, D_MODEL, EXPERT_FF), D_MODEL ** -0.5),
        "w_exp_up": nrm(ks[25], (DEPTH, N_EXPERTS, D_MODEL, EXPERT_FF), D_MODEL ** -0.5),
        "w_exp_down": nrm(ks[26], (DEPTH, N_EXPERTS, EXPERT_FF, D_MODEL), EXPERT_FF ** -0.5),
        "w_sh_gate": nrm(ks[27], (DEPTH, D_MODEL, SHARED_FF), D_MODEL ** -0.5),
        "w_sh_up": nrm(ks[28], (DEPTH, D_MODEL, SHARED_FF), D_MODEL ** -0.5),
        "w_sh_down": nrm(ks[29], (DEPTH, SHARED_FF, D_MODEL), SHARED_FF ** -0.5),
        "norm_final": 1.0 + nrm(ks[30], (D_MODEL,), 0.02),
    }


def reference(x_prompt, x_sample, cache_k, cache_v, cache_logf, state_ssm_re, state_ssm_im,
              norm_mix, w_in, b_forget, ssm_a_re, ssm_a_im, ssm_log_dt, ssm_b_re, ssm_b_im,
              ssm_c_re, ssm_c_im, ssm_d, w_glu, w_fox_out, w_out, norm_ffn, w_router, router_bias,
              w_exp_gate, w_exp_up, w_exp_down, w_sh_gate, w_sh_up, w_sh_down, norm_final):
    xp, xs = x_prompt, x_sample
    past = cache_k.shape[2]
    kp_l, vp_l, fp_l, rp_l, ip_l = [], [], [], [], []
    ks_l, vs_l, fs_l, rs_l, is_l = [], [], [], [], []
    for l in range(DEPTH):
        lam_dt, lam_bar, b_bar, c = s5_params(ssm_a_re[l], ssm_a_im[l], ssm_log_dt[l],
                                              ssm_b_re[l], ssm_b_im[l], ssm_c_re[l], ssm_c_im[l])
        moe_w = (norm_ffn[l], w_router[l], router_bias[l], w_exp_gate[l], w_exp_up[l], w_exp_down[l],
                 w_sh_gate[l], w_sh_up[l], w_sh_down[l])

        u, q, k, v, logf, g_a, g_b = project_in(rmsnorm(xp, norm_mix[l]), w_in[l], b_forget[l])
        y_ssm, h_last = s5_mixer(u, None, lam_dt, lam_bar, b_bar, c, ssm_d[l])
        attn = fox_prompt(q, k, v, logf)
        xp = xp + merge_branches(y_ssm, attn, g_a, g_b, w_glu[l], w_fox_out[l], w_out[l])
        xp = ffn_residual(xp, *moe_w)
        kp_l.append(k)
        vp_l.append(v)
        fp_l.append(logf)
        rp_l.append(jnp.real(h_last))
        ip_l.append(jnp.imag(h_last))

        u, q, k, v, logf, g_a, g_b = project_in(rmsnorm(xs, norm_mix[l]), w_in[l], b_forget[l])
        h0 = lax.complex(state_ssm_re[l].astype(jnp.float32), state_ssm_im[l].astype(jnp.float32))
        y_ssm, h_last = s5_mixer(u, h0, lam_dt, lam_bar, b_bar, c, ssm_d[l])
        k_all = jnp.concatenate([cache_k[l], k], axis=1)
        v_all = jnp.concatenate([cache_v[l], v], axis=1)
        c_all = jnp.cumsum(jnp.concatenate([cache_logf[l].astype(jnp.float32), logf], axis=1), axis=1)
        q_pos = past + jnp.arange(xs.shape[1], dtype=jnp.int32)
        attn = fox_attend(q, c_all[:, past:], q_pos, k_all, v_all, c_all)
        xs = xs + merge_branches(y_ssm, attn, g_a, g_b, w_glu[l], w_fox_out[l], w_out[l])
        xs = ffn_residual(xs, *moe_w)
        ks_l.append(k)
        vs_l.append(v)
        fs_l.append(logf)
        rs_l.append(jnp.real(h_last))
        is_l.append(jnp.imag(h_last))

    y_prompt = rmsnorm(xp, norm_final)
    y_sample = rmsnorm(xs, norm_final)
    return (y_prompt, y_sample,
            jnp.stack(kp_l), jnp.stack(vp_l), jnp.stack(fp_l), jnp.stack(rp_l), jnp.stack(ip_l),
            jnp.stack(ks_l), jnp.stack(vs_l), jnp.stack(fs_l), jnp.stack(rs_l), jnp.stack(is_l))
```

```python
import functools
import math

import jax
import jax.numpy as jnp
from jax import lax
from jax.experimental import pallas as pl
from jax.experimental.pallas import tpu as pltpu

F32 = jnp.float32
BF16 = jnp.bfloat16
HIGHEST = lax.Precision.HIGHEST

SSM_GROUP = 16
FOX_HEAD_DIM = 128
TOP_K = 8
ROUTED_SCALE = 2.5
NORM_EPS = 1e-6
LAM_RE_MAX = -1e-4

V7X_VMEM_BYTES = 64 * 1024 * 1024
VMEM_LIMIT = V7X_VMEM_BYTES - 8 * 1024 * 1024
LANES = 128
SUBLANES = 8
SSM_BLOCK_GROUPS = 16
SSM_BLOCK_CH = SSM_BLOCK_GROUPS * SSM_GROUP
SSM_PAIRS = SSM_BLOCK_GROUPS // 2
MASK_VALUE = -1e30


def _pick(n, pref, mult=SUBLANES):
    best = None
    d = mult
    while d <= min(n, pref):
        if n % d == 0:
            best = d
        d += mult
    return best if best is not None else n


def _params(sem, vmem=VMEM_LIMIT):
    return pltpu.CompilerParams(dimension_semantics=sem, vmem_limit_bytes=vmem)


def _sigmoid(x):
    return 1.0 / (1.0 + jnp.exp(-x))


def _rms(x, g):
    r = lax.rsqrt(jnp.mean(x * x, axis=-1, keepdims=True) + NORM_EPS)
    return x * r * g


def _norm_forget_kernel(x_ref, g_ref, w_ref, b_ref, h_ref, logf_ref):
    h = _rms(x_ref[...], g_ref[...])
    h_ref[...] = h.astype(h_ref.dtype)
    z = jnp.dot(h, w_ref[...], precision=HIGHEST, preferred_element_type=F32) + b_ref[...]
    logf_ref[...] = jnp.minimum(z, 0.0) - jnp.log1p(jnp.exp(-jnp.abs(z)))


def _norm_forget(x, g, w_f, b_f):
    T, D = x.shape
    H = w_f.shape[1]
    tm = _pick(T, 256)
    return pl.pallas_call(
        _norm_forget_kernel,
        grid=(T // tm,),
        in_specs=[pl.BlockSpec((tm, D), lambda i: (i, 0)),
                  pl.BlockSpec((1, D), lambda i: (0, 0)),
                  pl.BlockSpec((D, H), lambda i: (0, 0)),
                  pl.BlockSpec((1, H), lambda i: (0, 0))],
        out_specs=[pl.BlockSpec((tm, D), lambda i: (i, 0)),
                   pl.BlockSpec((tm, H), lambda i: (i, 0))],
        out_shape=[jax.ShapeDtypeStruct((T, D), BF16), jax.ShapeDtypeStruct((T, H), F32)],
        compiler_params=_params(("arbitrary",)),
        name="norm_forget",
    )(x, g.reshape(1, D), w_f, b_f.reshape(1, H))


def _norm_router_kernel(x_ref, g_ref, w_ref, b_ref, h_ref, hf_ref, idx_ref, gate_ref):
    h = _rms(x_ref[...], g_ref[...])
    h_ref[...] = h.astype(h_ref.dtype)
    hf_ref[...] = h
    scores = _sigmoid(jnp.dot(h, w_ref[...], precision=HIGHEST, preferred_element_type=F32))
    tm, E = scores.shape
    lane = lax.broadcasted_iota(jnp.int32, (tm, E), 1).astype(F32)
    col = lax.broadcasted_iota(jnp.int32, (tm, TOP_K), 1)
    work = scores + b_ref[...]
    idx = jnp.zeros((tm, TOP_K), F32)
    sel = jnp.zeros((tm, TOP_K), F32)
    for k in range(TOP_K):
        m = jnp.max(work, axis=1, keepdims=True)
        am = jnp.min(jnp.where(work == m, lane, float(E)), axis=1, keepdims=True)
        hit = lane == am
        sk = jnp.sum(jnp.where(hit, scores, 0.0), axis=1, keepdims=True)
        idx = jnp.where(col == k, am, idx)
        sel = jnp.where(col == k, sk, sel)
        work = jnp.where(hit, -jnp.inf, work)
    idx_ref[...] = idx.astype(jnp.int32)
    gate_ref[...] = ROUTED_SCALE * sel / jnp.sum(sel, axis=1, keepdims=True)


def _norm_router(x, g, w_router, router_bias):
    T, D = x.shape
    E = w_router.shape[1]
    tm = _pick(T, 256)
    return pl.pallas_call(
        _norm_router_kernel,
        grid=(T // tm,),
        in_specs=[pl.BlockSpec((tm, D), lambda i: (i, 0)),
                  pl.BlockSpec((1, D), lambda i: (0, 0)),
                  pl.BlockSpec((D, E), lambda i: (0, 0)),
                  pl.BlockSpec((1, E), lambda i: (0, 0))],
        out_specs=[pl.BlockSpec((tm, D), lambda i: (i, 0)),
                   pl.BlockSpec((tm, D), lambda i: (i, 0)),
                   pl.BlockSpec((tm, TOP_K), lambda i: (i, 0)),
                   pl.BlockSpec((tm, TOP_K), lambda i: (i, 0))],
        out_shape=[jax.ShapeDtypeStruct((T, D), BF16), jax.ShapeDtypeStruct((T, D), F32),
                   jax.ShapeDtypeStruct((T, TOP_K), jnp.int32), jax.ShapeDtypeStruct((T, TOP_K), F32)],
        compiler_params=_params(("arbitrary",)),
        name="norm_router",
    )(x, g.reshape(1, D), w_router, router_bias.reshape(1, E))


def _mm_kernel(a_ref, w_ref, o_ref, *, sigmoid):
    acc = jnp.dot(a_ref[...], w_ref[...], preferred_element_type=F32)
    if sigmoid:
        acc = _sigmoid(acc)
    o_ref[...] = acc.astype(o_ref.dtype)


def _matmul(a, w, out_dtype, *, sigmoid=False, tm_pref=512, tn_pref=1024, name="matmul"):
    M, K = a.shape
    N = w.shape[1]
    tm = _pick(M, tm_pref, 16)
    tn = _pick(N, tn_pref, LANES)
    return pl.pallas_call(
        functools.partial(_mm_kernel, sigmoid=sigmoid),
        grid=(N // tn, M // tm),
        in_specs=[pl.BlockSpec((tm, K), lambda j, i: (i, 0)),
                  pl.BlockSpec((K, tn), lambda j, i: (0, j))],
        out_specs=pl.BlockSpec((tm, tn), lambda j, i: (i, j)),
        out_shape=jax.ShapeDtypeStruct((M, N), out_dtype),
        compiler_params=_params(("arbitrary", "arbitrary")),
        name=name,
    )(a, w)


def _mm_residual_kernel(a_ref, w_ref, r_ref, o_ref):
    o_ref[...] = r_ref[...] + jnp.dot(a_ref[...], w_ref[...], preferred_element_type=F32)


def _matmul_residual(a, w, res, *, tm_pref=512, tn_pref=1024):
    M, K = a.shape
    N = w.shape[1]
    tm = _pick(M, tm_pref, 16)
    tn = _pick(N, tn_pref, LANES)
    return pl.pallas_call(
        _mm_residual_kernel,
        grid=(N // tn, M // tm),
        in_specs=[pl.BlockSpec((tm, K), lambda j, i: (i, 0)),
                  pl.BlockSpec((K, tn), lambda j, i: (0, j)),
                  pl.BlockSpec((tm, tn), lambda j, i: (i, j))],
        out_specs=pl.BlockSpec((tm, tn), lambda j, i: (i, j)),
        out_shape=jax.ShapeDtypeStruct((M, N), F32),
        compiler_params=_params(("arbitrary", "arbitrary")),
        name="out_proj_residual",
    )(a, w, res)


def _merge_kernel(g_ref, at_ref, wv_ref, wg_ref, wf_ref, sa_ref, sb_ref, o_ref):
    g = g_ref[...]
    val = jnp.dot(g, wv_ref[...], preferred_element_type=F32)
    gate = jnp.dot(g, wg_ref[...], preferred_element_type=F32)
    y_b = jnp.dot(at_ref[...], wf_ref[...], preferred_element_type=F32)
    y_a = val * _sigmoid(gate)
    o_ref[...] = (sa_ref[...].astype(F32) * y_a + sb_ref[...].astype(F32) * y_b).astype(o_ref.dtype)


def _merge(g, attn, w_glu, w_fox_out, sig_gates, *, tm_pref=512, tn_pref=512):
    T, K = g.shape
    D = w_fox_out.shape[1]
    tm = _pick(T, tm_pref, 16)
    tn = _pick(D, tn_pref, LANES)
    nj = D // tn
    return pl.pallas_call(
        _merge_kernel,
        grid=(nj, T // tm),
        in_specs=[pl.BlockSpec((tm, K), lambda j, i: (i, 0)),
                  pl.BlockSpec((tm, K), lambda j, i: (i, 0)),
                  pl.BlockSpec((K, tn), lambda j, i: (0, j)),
                  pl.BlockSpec((K, tn), lambda j, i: (0, j + nj)),
                  pl.BlockSpec((K, tn), lambda j, i: (0, j)),
                  pl.BlockSpec((tm, tn), lambda j, i: (i, j)),
                  pl.BlockSpec((tm, tn), lambda j, i: (i, j + nj))],
        out_specs=pl.BlockSpec((tm, tn), lambda j, i: (i, j)),
        out_shape=jax.ShapeDtypeStruct((T, D), BF16),
        compiler_params=_params(("arbitrary", "arbitrary")),
        name="merge_branches",
    )(g, attn, w_glu, w_glu, w_fox_out, sig_gates, sig_gates)


def _s5_disc_kernel(are_ref, aim_ref, ldt_ref, bre_ref, bim_ref, lre_ref, lim_ref, bbre_ref, bbim_ref):
    lam_re = jnp.minimum(are_ref[...], LAM_RE_MAX)
    lam_im = aim_ref[...]
    dt = jnp.exp(ldt_ref[...])
    mag = jnp.exp(lam_re * dt)
    lb_re = mag * jnp.cos(lam_im * dt)
    lb_im = mag * jnp.sin(lam_im * dt)
    lre_ref[...] = lb_re
    lim_ref[...] = lb_im
    n_re = lb_re - 1.0
    den = lam_re * lam_re + lam_im * lam_im
    co_re = (n_re * lam_re + lb_im * lam_im) / den
    co_im = (lb_im * lam_re - n_re * lam_im) / den
    G, P = co_re.shape
    PH = bre_ref.shape[1]
    hg = PH // P
    rows = lax.broadcasted_iota(jnp.int32, (P, PH), 0)
    cols = lax.broadcasted_iota(jnp.int32, (P, PH), 1)
    expand = jnp.where(cols // hg == rows, 1.0, 0.0).astype(F32)
    ce_re = jnp.dot(co_re, expand, precision=HIGHEST, preferred_element_type=F32)
    ce_im = jnp.dot(co_im, expand, precision=HIGHEST, preferred_element_type=F32)
    b_re = bre_ref[...]
    b_im = bim_ref[...]
    bbre_ref[...] = ce_re * b_re - ce_im * b_im
    bbim_ref[...] = ce_re * b_im + ce_im * b_re


def _s5_discretise(a_re, a_im, log_dt, b_re, b_im):
    G, P = a_re.shape
    hg = b_re.shape[2]
    outs = pl.pallas_call(
        _s5_disc_kernel,
        out_shape=[jax.ShapeDtypeStruct((G, P), F32), jax.ShapeDtypeStruct((G, P), F32),
                   jax.ShapeDtypeStruct((G, P * hg), F32), jax.ShapeDtypeStruct((G, P * hg), F32)],
        name="s5_discretise",
    )(a_re, a_im, log_dt.reshape(G, 1), b_re.reshape(G, P * hg), b_im.reshape(G, P * hg))
    lam_re, lam_im, bb_re, bb_im = outs
    return lam_re, lam_im, bb_re.reshape(G, P, hg), bb_im.reshape(G, P, hg)


def _s5_pack_weights(lam_re, lam_im, bb_re, bb_im, c_re, c_im):
    G, P, hg = bb_re.shape
    nblk = G // SSM_BLOCK_GROUPS
    half_ch = SSM_BLOCK_CH // 2
    pairs_per_half = SSM_PAIRS // 2

    def pair_view(x):
        return x.reshape((nblk, SSM_PAIRS, 2) + x.shape[1:])

    bre = pair_view(bb_re)
    bim = pair_view(bb_im)
    zero = jnp.zeros_like(bre[:, :, 0])

    def rows_of(which):
        re = [bre[:, :, 0], zero] if which == 0 else [zero, bre[:, :, 1]]
        im = [bim[:, :, 0], zero] if which == 0 else [zero, bim[:, :, 1]]
        blk = jnp.concatenate(re + im, axis=2)
        return jnp.swapaxes(blk, 2, 3)

    compact = jnp.concatenate([rows_of(0), rows_of(1)], axis=2)
    local = jnp.arange(SSM_PAIRS) % pairs_per_half
    place = jax.nn.one_hot(local[:, None] * 2 * hg + jnp.arange(2 * hg)[None, :], half_ch, dtype=F32)
    bw = jnp.einsum('prc,bprs->bpcs', place, compact)
    bw_hi = bw.astype(BF16)
    bw_lo = (bw - bw_hi.astype(F32)).astype(BF16)
    bw_hi2 = jnp.concatenate([bw_hi, bw_hi], axis=2)

    cre = pair_view(c_re)
    cim = pair_view(c_im)
    zc = jnp.zeros_like(cre[:, :, 0])

    def cols_of(which):
        re = [cre[:, :, 0], zc] if which == 0 else [zc, cre[:, :, 1]]
        im = [-cim[:, :, 0], zc] if which == 0 else [zc, -cim[:, :, 1]]
        return jnp.concatenate(re + im, axis=3)

    ccompact = jnp.concatenate([cols_of(0), cols_of(1)], axis=2)
    cplace = jax.nn.one_hot(jnp.arange(SSM_PAIRS)[:, None] * 2 * hg + jnp.arange(2 * hg)[None, :],
                            SSM_BLOCK_CH, dtype=F32)
    cw = jnp.einsum('prc,bprs->bpsc', cplace, ccompact).astype(BF16)

    lre = pair_view(lam_re)
    lim = pair_view(lam_im)
    a_re = jnp.concatenate([lre[:, :, 0], lre[:, :, 1]], axis=-1)
    a_im = jnp.concatenate([lim[:, :, 0], lim[:, :, 1]], axis=-1)
    return bw_hi2, bw_lo, cw, jnp.concatenate([a_re, a_im], axis=-1)


def _pack_state(s_re, s_im):
    B, G, P = s_re.shape
    nblk = G // SSM_BLOCK_GROUPS
    re = s_re.reshape(B, nblk, SSM_PAIRS, 2 * P)
    im = s_im.reshape(B, nblk, SSM_PAIRS, 2 * P)
    return jnp.moveaxis(jnp.concatenate([re, im], axis=-1), 1, 0)


def _unpack_state(st):
    nblk, B, pairs, w = st.shape
    st = jnp.moveaxis(st, 0, 1)
    P = w // 4
    re = st[..., :2 * P].reshape(B, nblk * pairs * 2, P)
    im = st[..., 2 * P:].reshape(B, nblk * pairs * 2, P)
    return re, im


def _s5_kernel(u_ref, bw2_ref, bwlo_ref, cw_ref, lam_ref, d_ref, h0_ref, y_ref, hout_ref, sre_ref, sim_ref,
               st_ref, *, nseq, lc, rp):
    chunk = pl.program_id(2)
    rows = nseq * lc
    half = LANES

    @pl.when(chunk == 0)
    def _():
        st_ref[...] = h0_ref[0]

    lam = lam_ref[0]
    sw = lam.shape[1] // 2
    a_re = lam[:, :sw]
    a_im = lam[:, sw:]

    u = u_ref[...]
    for hf in range(2):
        uh = u[:, hf * half:(hf + 1) * half]
        hi = uh.astype(BF16)
        lo = (uh - hi.astype(F32)).astype(BF16)
        lhs = jnp.concatenate([hi, lo], axis=1)
        for q in range(SSM_PAIRS // 2):
            p = hf * (SSM_PAIRS // 2) + q
            bu = (jnp.dot(lhs, bw2_ref[0, p], preferred_element_type=F32)
                  + jnp.dot(hi, bwlo_ref[0, p], preferred_element_type=F32))
            sre_ref[pl.ds(p * rp, rows), :] = bu[:, :sw]
            sim_ref[pl.ds(p * rp, rows), :] = bu[:, sw:]

    def seq_body(s, carry):
        base = s * lc

        def step(t, c):
            x_re, x_im = c
            at = pl.ds(base + t, SSM_PAIRS, stride=rp)
            n_re = a_re * x_re - a_im * x_im + sre_ref[at, :]
            n_im = a_re * x_im + a_im * x_re + sim_ref[at, :]
            sre_ref[at, :] = n_re
            sim_ref[at, :] = n_im
            return n_re, n_im

        x0 = st_ref[s]
        x_re, x_im = lax.fori_loop(0, lc, step, (x0[:, :sw], x0[:, sw:]), unroll=8)
        st_ref[s] = jnp.concatenate([x_re, x_im], axis=1)
        return carry

    lax.fori_loop(0, nseq, seq_body, 0)
    hout_ref[0] = st_ref[...]

    y = d_ref[...] * u
    for p in range(SSM_PAIRS):
        at = pl.ds(p * rp, rows)
        states = jnp.concatenate([sre_ref[at, :], sim_ref[at, :]], axis=1).astype(BF16)
        y = y + jnp.dot(states, cw_ref[0, p], preferred_element_type=F32)
    inner = math.sqrt(2.0 / math.pi) * (y + 0.044715 * (y * y * y))
    y_ref[...] = (0.5 * y * (1.0 + jnp.tanh(inner))).astype(y_ref.dtype)


def _s5_mixer(z, row0, n_seq_total, seq_len, weights, d_skip, h0, *, nseq, lc):
    bw2, bwlo, cw, lam = weights
    nblk = bw2.shape[0]
    rows = nseq * lc
    nchunk = seq_len // lc
    nsb = n_seq_total // nseq
    rp = rows + SUBLANES
    r0 = row0 // rows
    sw4 = lam.shape[-1]
    kern = functools.partial(_s5_kernel, nseq=nseq, lc=lc, rp=rp)
    return pl.pallas_call(
        kern,
        grid=(nblk, nsb, nchunk),
        in_specs=[pl.BlockSpec((rows, SSM_BLOCK_CH), lambda b, s, c: (r0 + s * nchunk + c, b)),
                  pl.BlockSpec((1,) + bw2.shape[1:], lambda b, s, c: (b, 0, 0, 0)),
                  pl.BlockSpec((1,) + bwlo.shape[1:], lambda b, s, c: (b, 0, 0, 0)),
                  pl.BlockSpec((1,) + cw.shape[1:], lambda b, s, c: (b, 0, 0, 0)),
                  pl.BlockSpec((1, SSM_PAIRS, sw4), lambda b, s, c: (b, 0, 0)),
                  pl.BlockSpec((1, SSM_BLOCK_CH), lambda b, s, c: (0, b)),
                  pl.BlockSpec((1, nseq, SSM_PAIRS, sw4), lambda b, s, c: (b, s, 0, 0))],
        out_specs=[pl.BlockSpec((rows, SSM_BLOCK_CH), lambda b, s, c: (s * nchunk + c, b)),
                   pl.BlockSpec((1, nseq, SSM_PAIRS, sw4), lambda b, s, c: (b, s, 0, 0))],
        out_shape=[jax.ShapeDtypeStruct((n_seq_total * seq_len, nblk * SSM_BLOCK_CH), BF16),
                   jax.ShapeDtypeStruct((nblk, n_seq_total, SSM_PAIRS, sw4), F32)],
        scratch_shapes=[pltpu.VMEM((SSM_PAIRS * rp, sw4 // 2), F32),
                        pltpu.VMEM((SSM_PAIRS * rp, sw4 // 2), F32),
                        pltpu.VMEM((nseq, SSM_PAIRS, sw4), F32)],
        compiler_params=_params(("arbitrary", "arbitrary", "arbitrary")),
        name="s5_mixer",
    )(z, bw2, bwlo, cw, lam, d_skip.reshape(1, -1), h0)


def _cumsum_kernel(x_ref, o_ref, *, chunk):
    H, L = x_ref.shape[1], x_ref.shape[2]
    carry = jnp.zeros((H, 1), F32)
    for s in range(0, L, chunk):
        w = min(chunk, L - s)
        r = lax.broadcasted_iota(jnp.int32, (w, w), 0)
        c = lax.broadcasted_iota(jnp.int32, (w, w), 1)
        tri = jnp.where(r <= c, 1.0, 0.0).astype(F32)
        part = jnp.dot(x_ref[0, :, s:s + w], tri, precision=HIGHEST, preferred_element_type=F32) + carry
        o_ref[0, :, s:s + w] = part
        carry = part[:, w - 1:w]


def _cumsum_last(x):
    B, H, L = x.shape
    return pl.pallas_call(
        functools.partial(_cumsum_kernel, chunk=256),
        grid=(B,),
        in_specs=[pl.BlockSpec((1, H, L), lambda b: (b, 0, 0))],
        out_specs=pl.BlockSpec((1, H, L), lambda b: (b, 0, 0)),
        out_shape=jax.ShapeDtypeStruct((B, H, L), F32),
        compiler_params=_params(("arbitrary",)),
        name="logf_cumsum",
    )(x)


def _pick_head(c_ref, h):
    blk = c_ref[...]
    lane = lax.broadcasted_iota(jnp.int32, blk.shape, 1)
    return jnp.sum(jnp.where(lane == h, blk, 0.0), axis=1, keepdims=True)


def _nt_dot(a, b):
    return lax.dot_general(a, b, (((1,), (1,)), ((), ())), preferred_element_type=F32)


def _fox_prompt_kernel(q_ref, k_ref, v_ref, cq_ref, ck_ref, o_ref, m_ref, l_ref, acc_ref, *, tq, tk, scale):
    h = pl.program_id(1)
    qi = pl.program_id(2)
    ki = pl.program_id(3)

    @pl.when(ki == 0)
    def _():
        m_ref[...] = jnp.full(m_ref.shape, MASK_VALUE, F32)
        l_ref[...] = jnp.zeros(l_ref.shape, F32)
        acc_ref[...] = jnp.zeros(acc_ref.shape, F32)

    @pl.when(ki * tk <= qi * tq + (tq - 1))
    def _():
        q = (q_ref[...] * scale).astype(BF16)
        s = _nt_dot(q, k_ref[...].astype(BF16))
        s = s + _pick_head(cq_ref, h) - ck_ref[0, pl.ds(h, 1), :]
        rows = qi * tq + lax.broadcasted_iota(jnp.int32, (tq, tk), 0)
        cols = ki * tk + lax.broadcasted_iota(jnp.int32, (tq, tk), 1)
        s = jnp.where(cols <= rows, s, MASK_VALUE)
        m_prev = m_ref[...]
        m_new = jnp.maximum(m_prev, jnp.max(s, axis=1, keepdims=True))
        alpha = jnp.exp(m_prev - m_new)
        p = jnp.exp(s - m_new)
        l_ref[...] = alpha * l_ref[...] + jnp.sum(p, axis=1, keepdims=True)
        acc_ref[...] = alpha * acc_ref[...] + jnp.dot(p.astype(BF16), v_ref[...].astype(BF16),
                                                      preferred_element_type=F32)
        m_ref[...] = m_new

    @pl.when(ki == pl.num_programs(3) - 1)
    def _():
        o_ref[...] = (acc_ref[...] / l_ref[...]).astype(o_ref.dtype)


def _fox_prompt(z, c_col, c_row, B, L, H, col_q, col_k, col_v, scale):
    dh = FOX_HEAD_DIM
    tq = _pick(L, 512)
    tk = tq
    nq = L // tq
    nk = L // tk

    def kv_blk(qi, ki):
        return jnp.minimum(ki, (qi * tq + tq - 1) // tk)

    return pl.pallas_call(
        functools.partial(_fox_prompt_kernel, tq=tq, tk=tk, scale=scale),
        grid=(B, H, nq, nk),
        in_specs=[pl.BlockSpec((tq, dh), lambda b, h, qi, ki: (b * nq + qi, col_q + h)),
                  pl.BlockSpec((tk, dh), lambda b, h, qi, ki: (b * nk + kv_blk(qi, ki), col_k + h)),
                  pl.BlockSpec((tk, dh), lambda b, h, qi, ki: (b * nk + kv_blk(qi, ki), col_v + h)),
                  pl.BlockSpec((tq, H), lambda b, h, qi, ki: (b * nq + qi, 0)),
                  pl.BlockSpec((1, H, tk), lambda b, h, qi, ki: (b, 0, kv_blk(qi, ki)))],
        out_specs=pl.BlockSpec((tq, dh), lambda b, h, qi, ki: (b * nq + qi, h)),
        out_shape=jax.ShapeDtypeStruct((B * L, H * dh), BF16),
        scratch_shapes=[pltpu.VMEM((tq, 1), F32), pltpu.VMEM((tq, 1), F32), pltpu.VMEM((tq, dh), F32)],
        compiler_params=_params(("arbitrary",) * 4),
        name="fox_prompt",
    )(z, z, z, c_col, c_row)


def _fox_sample_kernel(q_ref, kn_ref, vn_ref, kc_ref, vc_ref, cq_ref, ck_ref, o_ref, *, past, scale):
    h = pl.program_id(1)
    q = (q_ref[...] * scale).astype(BF16)
    n = q.shape[0]
    c_all = ck_ref[0, pl.ds(h, 1), :]
    cq = _pick_head(cq_ref, h)
    s_old = _nt_dot(q, kc_ref[0].astype(BF16)) + cq - c_all[:, :past]
    s_new = _nt_dot(q, kn_ref[...].astype(BF16)) + cq - c_all[:, past:]
    r = lax.broadcasted_iota(jnp.int32, (n, n), 0)
    c = lax.broadcasted_iota(jnp.int32, (n, n), 1)
    s_new = jnp.where(c <= r, s_new, MASK_VALUE)
    m = jnp.maximum(jnp.max(s_old, axis=1, keepdims=True), jnp.max(s_new, axis=1, keepdims=True))
    p_old = jnp.exp(s_old - m)
    p_new = jnp.exp(s_new - m)
    l = jnp.sum(p_old, axis=1, keepdims=True) + jnp.sum(p_new, axis=1, keepdims=True)
    acc = (jnp.dot(p_old.astype(BF16), vc_ref[0].astype(BF16), preferred_element_type=F32)
           + jnp.dot(p_new.astype(BF16), vn_ref[...].astype(BF16), preferred_element_type=F32))
    o_ref[...] = (acc / l).astype(o_ref.dtype)


def _fox_sample(z, row0, cache_k, cache_v, c_col, c_row, B, n, H, col_q, col_k, col_v, scale):
    dh = FOX_HEAD_DIM
    past = cache_k.shape[1]
    r0 = row0 // n
    return pl.pallas_call(
        functools.partial(_fox_sample_kernel, past=past, scale=scale),
        grid=(B, H),
        in_specs=[pl.BlockSpec((n, dh), lambda b, h: (r0 + b, col_q + h)),
                  pl.BlockSpec((n, dh), lambda b, h: (r0 + b, col_k + h)),
                  pl.BlockSpec((n, dh), lambda b, h: (r0 + b, col_v + h)),
                  pl.BlockSpec((1, past, dh), lambda b, h: (b, 0, h)),
                  pl.BlockSpec((1, past, dh), lambda b, h: (b, 0, h)),
                  pl.BlockSpec((n, H), lambda b, h: (b, 0)),
                  pl.BlockSpec((1, H, past + n), lambda b, h: (b, 0, 0))],
        out_specs=pl.BlockSpec((n, dh), lambda b, h: (b, h)),
        out_shape=jax.ShapeDtypeStruct((B * n, H * dh), BF16),
        compiler_params=_params(("arbitrary", "arbitrary")),
        name="fox_sample",
    )(z, z, z, cache_k, cache_v, c_col, c_row)


def _gather_rows_kernel(tok_ref, h_hbm, o_ref, buf, sem, *, tm):
    i = pl.program_id(0)

    def row_copy(r):
        t = tok_ref[i * tm + r]
        return pltpu.make_async_copy(h_hbm.at[pl.ds(t, 1), :], buf.at[pl.ds(r, 1), :], sem)

    def issue(r, c):
        row_copy(r).start()
        return c

    def drain(r, c):
        row_copy(r).wait()
        return c

    lax.fori_loop(0, tm, issue, 0)
    lax.fori_loop(0, tm, drain, 0)
    o_ref[...] = buf[...].astype(o_ref.dtype)


def _gather_rows(h, slot_tok, tm):
    n_slots = slot_tok.shape[0]
    D = h.shape[1]
    return pl.pallas_call(
        functools.partial(_gather_rows_kernel, tm=tm),
        grid_spec=pltpu.PrefetchScalarGridSpec(
            num_scalar_prefetch=1,
            grid=(n_slots // tm,),
            in_specs=[pl.BlockSpec(memory_space=pl.ANY)],
            out_specs=pl.BlockSpec((tm, D), lambda i, tok: (i, 0)),
            scratch_shapes=[pltpu.VMEM((tm, D), h.dtype), pltpu.SemaphoreType.DMA(())]),
        out_shape=jax.ShapeDtypeStruct((n_slots, D), BF16),
        compiler_params=_params(("arbitrary",)),
        name="moe_dispatch_gather",
    )(slot_tok, h)


def _expert_up_kernel(be_ref, nu_ref, x_ref, wg_ref, wu_ref, o_ref, wg_s, wu_s):
    i = pl.program_id(1)
    fresh = jnp.logical_or(i == 0, be_ref[i] != be_ref[jnp.maximum(i - 1, 0)])

    @pl.when(fresh)
    def _():
        wg_s[...] = wg_ref[0].astype(BF16)
        wu_s[...] = wu_ref[0].astype(BF16)

    @pl.when(i < nu_ref[0])
    def _():
        x = x_ref[...]
        a = jnp.dot(x, wg_s[...], preferred_element_type=F32)
        b = jnp.dot(x, wu_s[...], preferred_element_type=F32)
        o_ref[...] = (a * _sigmoid(a) * b).astype(o_ref.dtype)

    @pl.when(i >= nu_ref[0])
    def _():
        o_ref[...] = jnp.zeros(o_ref.shape, o_ref.dtype)


def _expert_up(x, w_gate, w_up, block_e, n_used, tm, tf_pref=512):
    n_rows, D = x.shape
    F = w_gate.shape[2]
    tf = _pick(F, tf_pref, LANES)
    return pl.pallas_call(
        _expert_up_kernel,
        grid_spec=pltpu.PrefetchScalarGridSpec(
            num_scalar_prefetch=2,
            grid=(F // tf, n_rows // tm),
            in_specs=[pl.BlockSpec((tm, D), lambda f, i, be, nu: (i, 0)),
                      pl.BlockSpec((1, D, tf), lambda f, i, be, nu: (be[i], 0, f)),
                      pl.BlockSpec((1, D, tf), lambda f, i, be, nu: (be[i], 0, f))],
            out_specs=pl.BlockSpec((tm, tf), lambda f, i, be, nu: (i, f)),
            scratch_shapes=[pltpu.VMEM((D, tf), BF16), pltpu.VMEM((D, tf), BF16)]),
        out_shape=jax.ShapeDtypeStruct((n_rows, F), BF16),
        compiler_params=_params(("arbitrary", "arbitrary")),
        name="expert_up",
    )(block_e, n_used, x, w_gate, w_up)


def _expert_down_kernel(be_ref, nu_ref, a_ref, wd_ref, o_ref, wd_s):
    i = pl.program_id(1)
    fresh = jnp.logical_or(i == 0, be_ref[i] != be_ref[jnp.maximum(i - 1, 0)])

    @pl.when(fresh)
    def _():
        wd_s[...] = wd_ref[0].astype(BF16)

    @pl.when(i < nu_ref[0])
    def _():
        o_ref[...] = jnp.dot(a_ref[...], wd_s[...], preferred_element_type=F32)

    @pl.when(i >= nu_ref[0])
    def _():
        o_ref[...] = jnp.zeros(o_ref.shape, o_ref.dtype)


def _expert_down(act, w_down, block_e, n_used, tm, tn_pref=2048):
    n_rows, F = act.shape
    D = w_down.shape[2]
    tn = _pick(D, tn_pref, LANES)
    return pl.pallas_call(
        _expert_down_kernel,
        grid_spec=pltpu.PrefetchScalarGridSpec(
            num_scalar_prefetch=2,
            grid=(D // tn, n_rows // tm),
            in_specs=[pl.BlockSpec((tm, F), lambda n, i, be, nu: (i, 0)),
                      pl.BlockSpec((1, F, tn), lambda n, i, be, nu: (be[i], 0, n))],
            out_specs=pl.BlockSpec((tm, tn), lambda n, i, be, nu: (i, n)),
            scratch_shapes=[pltpu.VMEM((F, tn), BF16)]),
        out_shape=jax.ShapeDtypeStruct((n_rows, D), F32),
        compiler_params=_params(("arbitrary", "arbitrary")),
        name="expert_down",
    )(block_e, n_used, act, w_down)


def _combine_kernel(dest_ref, x_ref, sh_ref, gate_ref, g_ref, rows_hbm, o_ref, buf, sem, *, tc, final_norm):
    i = pl.program_id(0)

    def row_copy(j):
        r = j // TOP_K
        k = j % TOP_K
        slot = dest_ref[(i * tc + r) * TOP_K + k]
        return pltpu.make_async_copy(rows_hbm.at[pl.ds(slot, 1), :], buf.at[k, pl.ds(r, 1), :], sem)

    def issue(j, c):
        row_copy(j).start()
        return c

    def drain(j, c):
        row_copy(j).wait()
        return c

    lax.fori_loop(0, tc * TOP_K, issue, 0)
    lax.fori_loop(0, tc * TOP_K, drain, 0)
    y = x_ref[...] + sh_ref[...]
    gates = gate_ref[...]
    for k in range(TOP_K):
        y = y + gates[:, k:k + 1] * buf[k]
    o_ref[...] = _rms(y, g_ref[...]) if final_norm else y


def _combine(x, shared, gates, dest, rows, g_final, final_norm, tc=32):
    T, D = x.shape
    tc = _pick(T, tc)
    return pl.pallas_call(
        functools.partial(_combine_kernel, tc=tc, final_norm=final_norm),
        grid_spec=pltpu.PrefetchScalarGridSpec(
            num_scalar_prefetch=1,
            grid=(T // tc,),
            in_specs=[pl.BlockSpec((tc, D), lambda i, d: (i, 0)),
                      pl.BlockSpec((tc, D), lambda i, d: (i, 0)),
                      pl.BlockSpec((tc, TOP_K), lambda i, d: (i, 0)),
                      pl.BlockSpec((1, D), lambda i, d: (0, 0)),
                      pl.BlockSpec(memory_space=pl.ANY)],
            out_specs=pl.BlockSpec((tc, D), lambda i, d: (i, 0)),
            scratch_shapes=[pltpu.VMEM((TOP_K, tc, D), F32), pltpu.SemaphoreType.DMA(())]),
        out_shape=jax.ShapeDtypeStruct((T, D), F32),
        compiler_params=_params(("arbitrary",)),
        name="moe_combine_norm",
    )(dest, x, shared, gates, g_final.reshape(1, D), rows)


def _routing_tables(idx, n_experts, tm):
    T = idx.shape[0]
    n_assign = T * TOP_K
    n_blocks = (n_assign + n_experts * (tm - 1) + tm - 1) // tm
    flat_e = idx.reshape(n_assign)
    onehot = (flat_e[:, None] == jnp.arange(n_experts, dtype=jnp.int32)[None, :]).astype(jnp.int32)
    before = jnp.cumsum(onehot, axis=0) - onehot
    rank = jnp.sum(before * onehot, axis=1)
    counts = jnp.sum(onehot, axis=0)
    padded = (counts + tm - 1) // tm * tm
    pad_end = jnp.cumsum(padded)
    pad_start = pad_end - padded
    dest = (pad_start[flat_e] + rank).astype(jnp.int32)
    slot_tok = jnp.zeros((n_blocks * tm,), jnp.int32).at[dest].set(
        jnp.arange(n_assign, dtype=jnp.int32) // TOP_K)
    block_e = jnp.minimum(
        jnp.searchsorted(pad_end, jnp.arange(n_blocks, dtype=jnp.int32) * tm, side='right'),
        n_experts - 1).astype(jnp.int32)
    n_used = (pad_end[-1] // tm).astype(jnp.int32).reshape(1)
    return dest, slot_tok, block_e, n_used


def kernel(x_prompt, x_sample, cache_k, cache_v, cache_logf, state_ssm_re, state_ssm_im, norm_mix, w_in, b_forget, ssm_a_re, ssm_a_im, ssm_log_dt, ssm_b_re, ssm_b_im, ssm_c_re, ssm_c_im, ssm_d, w_glu, w_fox_out, w_out, norm_ffn, w_router, router_bias, w_exp_gate, w_exp_up, w_exp_down, w_sh_gate, w_sh_up, w_sh_down, norm_final):
    Bp, Lp, D = x_prompt.shape
    Bs, Ls, _ = x_sample.shape
    depth = w_in.shape[0]
    past = cache_k.shape[2]
    H = cache_k.shape[3]
    dh = cache_k.shape[4]
    assert dh == FOX_HEAD_DIM
    d_fox = H * dh
    d_ssm = ssm_d.shape[1]
    n_experts = w_router.shape[2]
    Tp = Bp * Lp
    Ts = Bs * Ls
    T = Tp + Ts
    scale = dh ** -0.5
    n_uqkv = d_ssm + 3 * d_fox
    col_q = d_ssm // dh
    col_k = col_q + H
    col_v = col_k + H
    moe_tm = 256

    x = jnp.concatenate([x_prompt.reshape(Tp, D), x_sample.reshape(Ts, D)], axis=0)
    outs = {name: [] for name in ("kp", "vp", "fp", "rp", "ip", "ks", "vs", "fs", "rs", "is")}
    for l in range(depth):
        w_uqkv = w_in[l][:, :n_uqkv].astype(BF16)
        w_f = w_in[l][:, n_uqkv:n_uqkv + H]
        w_gates = w_in[l][:, n_uqkv + H:].astype(BF16)

        h, logf = _norm_forget(x, norm_mix[l], w_f, b_forget[l])
        z = _matmul(h, w_uqkv, F32, name="in_proj")
        sig_gates = _matmul(h, w_gates, BF16, sigmoid=True, name="gate_proj")

        lam_re, lam_im, bb_re, bb_im = _s5_discretise(ssm_a_re[l], ssm_a_im[l], ssm_log_dt[l],
                                                      ssm_b_re[l], ssm_b_im[l])
        s5_w = _s5_pack_weights(lam_re, lam_im, bb_re, bb_im, ssm_c_re[l], ssm_c_im[l])
        zero_state = jnp.zeros((Bp,) + state_ssm_re.shape[2:], F32)
        g_p, st_p = _s5_mixer(z, 0, Bp, Lp, s5_w, ssm_d[l], _pack_state(zero_state, zero_state),
                              nseq=1, lc=_pick(Lp, 256))
        g_s, st_s = _s5_mixer(z, Tp, Bs, Ls, s5_w, ssm_d[l],
                              _pack_state(state_ssm_re[l].astype(F32), state_ssm_im[l].astype(F32)),
                              nseq=Bs, lc=Ls)
        g = jnp.concatenate([g_p, g_s], axis=0)

        logf_p = logf[:Tp].reshape(Bp, Lp, H)
        logf_s = logf[Tp:].reshape(Bs, Ls, H)
        crow_p = _cumsum_last(jnp.swapaxes(logf_p, 1, 2))
        ccol_p = jnp.swapaxes(crow_p, 1, 2).reshape(Tp, H)
        attn_p = _fox_prompt(z, ccol_p, crow_p, Bp, Lp, H, col_q, col_k, col_v, scale)
        lf_all = jnp.concatenate([cache_logf[l].astype(F32), logf_s], axis=1)
        crow_s = _cumsum_last(jnp.swapaxes(lf_all, 1, 2))
        ccol_s = jnp.swapaxes(crow_s[:, :, past:], 1, 2).reshape(Ts, H)
        attn_s = _fox_sample(z, Tp, cache_k[l].reshape(Bs, past, d_fox), cache_v[l].reshape(Bs, past, d_fox),
                             ccol_s, crow_s, Bs, Ls, H, col_q, col_k, col_v, scale)
        attn = jnp.concatenate([attn_p, attn_s], axis=0)

        merged = _merge(g, attn, w_glu[l].astype(BF16), w_fox_out[l].astype(BF16), sig_gates)
        x = _matmul_residual(merged, w_out[l].astype(BF16), x)

        norm_out = norm_ffn[l]
        h2, h2f, idx, gates = _norm_router(x, norm_out, w_router[l], router_bias[l])
        dest, slot_tok, block_e, n_used = _routing_tables(idx, n_experts, moe_tm)
        xs = _gather_rows(h2f, slot_tok, moe_tm)
        act = _expert_up(xs, w_exp_gate[l], w_exp_up[l], block_e, n_used, moe_tm)
        routed = _expert_down(act, w_exp_down[l], block_e, n_used, moe_tm)
        sh_tm = _pick(T, moe_tm, 16)
        sh_be = jnp.zeros((T // sh_tm,), jnp.int32)
        sh_nu = jnp.full((1,), T // sh_tm, jnp.int32)
        sh_act = _expert_up(h2, w_sh_gate[l][None], w_sh_up[l][None], sh_be, sh_nu, sh_tm)
        shared = _expert_down(sh_act, w_sh_down[l][None], sh_be, sh_nu, sh_tm)
        x = _combine(x, shared, gates, dest, routed, norm_final, final_norm=(l == depth - 1))

        k_all = z[:, d_ssm + d_fox:d_ssm + 2 * d_fox]
        v_all = z[:, d_ssm + 2 * d_fox:d_ssm + 3 * d_fox]
        re_p, im_p = _unpack_state(st_p)
        re_s, im_s = _unpack_state(st_s)
        outs["kp"].append(k_all[:Tp].reshape(Bp, Lp, H, dh))
        outs["vp"].append(v_all[:Tp].reshape(Bp, Lp, H, dh))
        outs["fp"].append(logf_p)
        outs["rp"].append(re_p)
        outs["ip"].append(im_p)
        outs["ks"].append(k_all[Tp:].reshape(Bs, Ls, H, dh))
        outs["vs"].append(v_all[Tp:].reshape(Bs, Ls, H, dh))
        outs["fs"].append(logf_s)
        outs["rs"].append(re_s)
        outs["is"].append(im_s)

    return (x[:Tp].reshape(Bp, Lp, D), x[Tp:].reshape(Bs, Ls, D),
            jnp.stack(outs["kp"]), jnp.stack(outs["vp"]), jnp.stack(outs["fp"]),
            jnp.stack(outs["rp"]), jnp.stack(outs["ip"]),
            jnp.stack(outs["ks"]), jnp.stack(outs["vs"]), jnp.stack(outs["fs"]),
            jnp.stack(outs["rs"]), jnp.stack(outs["is"]))
```

```python
import functools
import math

import jax
import jax.numpy as jnp
from jax import lax
from jax.experimental import pallas as pl
from jax.experimental.pallas import tpu as pltpu

F32 = jnp.float32
BF16 = jnp.bfloat16
HIGHEST = lax.Precision.HIGHEST

SSM_GROUP = 16
FOX_HEAD_DIM = 128
TOP_K = 8
ROUTED_SCALE = 2.5
NORM_EPS = 1e-6
LAM_RE_MAX = -1e-4

V7X_VMEM_BYTES = 64 * 1024 * 1024
VMEM_LIMIT = V7X_VMEM_BYTES - 8 * 1024 * 1024
LANES = 128
SUBLANES = 8
SSM_BLOCK_GROUPS = 16
SSM_BLOCK_CH = SSM_BLOCK_GROUPS * SSM_GROUP
SSM_PAIRS = SSM_BLOCK_GROUPS // 2
MASK_VALUE = -1e30


def _pick(n, pref, mult=SUBLANES):
    best = None
    d = mult
    while d <= min(n, pref):
        if n % d == 0:
            best = d
        d += mult
    return best if best is not None else n


def _params(sem, vmem=VMEM_LIMIT):
    return pltpu.CompilerParams(dimension_semantics=sem, vmem_limit_bytes=vmem)


def _sigmoid(x):
    return 1.0 / (1.0 + jnp.exp(-x))


def _rms(x, g):
    r = lax.rsqrt(jnp.mean(x * x, axis=-1, keepdims=True) + NORM_EPS)
    return x * r * g


def _norm_forget_kernel(x_ref, g_ref, w_ref, b_ref, h_ref, logf_ref):
    h = _rms(x_ref[...], g_ref[...])
    h_ref[...] = h.astype(h_ref.dtype)
    z = jnp.dot(h, w_ref[...], precision=HIGHEST, preferred_element_type=F32) + b_ref[...]
    logf_ref[...] = jnp.minimum(z, 0.0) - jnp.log1p(jnp.exp(-jnp.abs(z)))


def _norm_forget(x, g, w_f, b_f):
    T, D = x.shape
    H = w_f.shape[1]
    tm = _pick(T, 256)
    return pl.pallas_call(
        _norm_forget_kernel,
        grid=(T // tm,),
        in_specs=[pl.BlockSpec((tm, D), lambda i: (i, 0)),
                  pl.BlockSpec((1, D), lambda i: (0, 0)),
                  pl.BlockSpec((D, H), lambda i: (0, 0)),
                  pl.BlockSpec((1, H), lambda i: (0, 0))],
        out_specs=[pl.BlockSpec((tm, D), lambda i: (i, 0)),
                   pl.BlockSpec((tm, H), lambda i: (i, 0))],
        out_shape=[jax.ShapeDtypeStruct((T, D), BF16), jax.ShapeDtypeStruct((T, H), F32)],
        compiler_params=_params(("arbitrary",)),
        name="norm_forget",
    )(x, g.reshape(1, D), w_f, b_f.reshape(1, H))


def _norm_router_kernel(x_ref, g_ref, w_ref, b_ref, h_ref, hf_ref, idx_ref, gate_ref, rank_ref, cnt_ref):
    @pl.when(pl.program_id(0) == 0)
    def _():
        cnt_ref[...] = jnp.zeros(cnt_ref.shape, F32)

    h = _rms(x_ref[...], g_ref[...])
    h_ref[...] = h.astype(h_ref.dtype)
    hf_ref[...] = h
    scores = _sigmoid(jnp.dot(h, w_ref[...], precision=HIGHEST, preferred_element_type=F32))
    tm, E = scores.shape
    lane = lax.broadcasted_iota(jnp.int32, (tm, E), 1).astype(F32)
    col = lax.broadcasted_iota(jnp.int32, (tm, TOP_K), 1)
    work = scores + b_ref[...]
    idx = jnp.zeros((tm, TOP_K), F32)
    sel = jnp.zeros((tm, TOP_K), F32)
    chosen = jnp.zeros((tm, E), F32)
    hits = []
    for k in range(TOP_K):
        m = jnp.max(work, axis=1, keepdims=True)
        am = jnp.min(jnp.where(work == m, lane, float(E)), axis=1, keepdims=True)
        hit = lane == am
        hits.append(hit)
        sk = jnp.sum(jnp.where(hit, scores, 0.0), axis=1, keepdims=True)
        idx = jnp.where(col == k, am, idx)
        sel = jnp.where(col == k, sk, sel)
        chosen = jnp.where(hit, 1.0, chosen)
        work = jnp.where(hit, -jnp.inf, work)
    idx_ref[...] = idx.astype(jnp.int32)
    gate_ref[...] = ROUTED_SCALE * sel / jnp.sum(sel, axis=1, keepdims=True)

    r = lax.broadcasted_iota(jnp.int32, (tm, tm), 0)
    c = lax.broadcasted_iota(jnp.int32, (tm, tm), 1)
    lower = jnp.where(c < r, 1.0, 0.0).astype(BF16)
    before = jnp.dot(lower, chosen.astype(BF16), preferred_element_type=F32) + cnt_ref[...]
    rank = jnp.zeros((tm, TOP_K), F32)
    for k in range(TOP_K):
        rank = jnp.where(col == k, jnp.sum(jnp.where(hits[k], before, 0.0), axis=1, keepdims=True), rank)
    rank_ref[...] = rank.astype(jnp.int32)
    cnt_ref[...] = cnt_ref[...] + jnp.sum(chosen, axis=0, keepdims=True)


def _norm_router(x, g, w_router, router_bias):
    T, D = x.shape
    E = w_router.shape[1]
    tm = _pick(T, 256)
    return pl.pallas_call(
        _norm_router_kernel,
        grid=(T // tm,),
        in_specs=[pl.BlockSpec((tm, D), lambda i: (i, 0)),
                  pl.BlockSpec((1, D), lambda i: (0, 0)),
                  pl.BlockSpec((D, E), lambda i: (0, 0)),
                  pl.BlockSpec((1, E), lambda i: (0, 0))],
        out_specs=[pl.BlockSpec((tm, D), lambda i: (i, 0)),
                   pl.BlockSpec((tm, D), lambda i: (i, 0)),
                   pl.BlockSpec((tm, TOP_K), lambda i: (i, 0)),
                   pl.BlockSpec((tm, TOP_K), lambda i: (i, 0)),
                   pl.BlockSpec((tm, TOP_K), lambda i: (i, 0)),
                   pl.BlockSpec((1, E), lambda i: (0, 0))],
        out_shape=[jax.ShapeDtypeStruct((T, D), BF16), jax.ShapeDtypeStruct((T, D), F32),
                   jax.ShapeDtypeStruct((T, TOP_K), jnp.int32), jax.ShapeDtypeStruct((T, TOP_K), F32),
                   jax.ShapeDtypeStruct((T, TOP_K), jnp.int32), jax.ShapeDtypeStruct((1, E), F32)],
        compiler_params=_params(("arbitrary",)),
        name="norm_router",
    )(x, g.reshape(1, D), w_router, router_bias.reshape(1, E))


def _mm_kernel(a_ref, w_ref, o_ref, *, sigmoid):
    acc = jnp.dot(a_ref[...], w_ref[...], preferred_element_type=F32)
    if sigmoid:
        acc = _sigmoid(acc)
    o_ref[...] = acc.astype(o_ref.dtype)


def _matmul(a, w, out_dtype, *, sigmoid=False, tm_pref=512, tn_pref=1024, name="matmul"):
    M, K = a.shape
    N = w.shape[1]
    tm = _pick(M, tm_pref, 16)
    tn = _pick(N, tn_pref, LANES)
    return pl.pallas_call(
        functools.partial(_mm_kernel, sigmoid=sigmoid),
        grid=(N // tn, M // tm),
        in_specs=[pl.BlockSpec((tm, K), lambda j, i: (i, 0)),
                  pl.BlockSpec((K, tn), lambda j, i: (0, j))],
        out_specs=pl.BlockSpec((tm, tn), lambda j, i: (i, j)),
        out_shape=jax.ShapeDtypeStruct((M, N), out_dtype),
        compiler_params=_params(("arbitrary", "arbitrary")),
        name=name,
    )(a, w)


def _mm_residual_kernel(a_ref, w_ref, r_ref, o_ref):
    o_ref[...] = r_ref[...] + jnp.dot(a_ref[...], w_ref[...], preferred_element_type=F32)


def _matmul_residual(a, w, res, *, tm_pref=512, tn_pref=1024):
    M, K = a.shape
    N = w.shape[1]
    tm = _pick(M, tm_pref, 16)
    tn = _pick(N, tn_pref, LANES)
    return pl.pallas_call(
        _mm_residual_kernel,
        grid=(N // tn, M // tm),
        in_specs=[pl.BlockSpec((tm, K), lambda j, i: (i, 0)),
                  pl.BlockSpec((K, tn), lambda j, i: (0, j)),
                  pl.BlockSpec((tm, tn), lambda j, i: (i, j))],
        out_specs=pl.BlockSpec((tm, tn), lambda j, i: (i, j)),
        out_shape=jax.ShapeDtypeStruct((M, N), F32),
        compiler_params=_params(("arbitrary", "arbitrary")),
        name="out_proj_residual",
    )(a, w, res)


def _merge_kernel(g_ref, at_ref, wv_ref, wg_ref, wf_ref, sa_ref, sb_ref, o_ref):
    g = g_ref[...]
    val = jnp.dot(g, wv_ref[...], preferred_element_type=F32)
    gate = jnp.dot(g, wg_ref[...], preferred_element_type=F32)
    y_b = jnp.dot(at_ref[...], wf_ref[...], preferred_element_type=F32)
    y_a = val * _sigmoid(gate)
    o_ref[...] = (sa_ref[...].astype(F32) * y_a + sb_ref[...].astype(F32) * y_b).astype(o_ref.dtype)


def _merge(g, attn, w_glu, w_fox_out, sig_gates, *, tm_pref=512, tn_pref=512):
    T, K = g.shape
    D = w_fox_out.shape[1]
    tm = _pick(T, tm_pref, 16)
    tn = _pick(D, tn_pref, LANES)
    nj = D // tn
    return pl.pallas_call(
        _merge_kernel,
        grid=(nj, T // tm),
        in_specs=[pl.BlockSpec((tm, K), lambda j, i: (i, 0)),
                  pl.BlockSpec((tm, K), lambda j, i: (i, 0)),
                  pl.BlockSpec((K, tn), lambda j, i: (0, j)),
                  pl.BlockSpec((K, tn), lambda j, i: (0, j + nj)),
                  pl.BlockSpec((K, tn), lambda j, i: (0, j)),
                  pl.BlockSpec((tm, tn), lambda j, i: (i, j)),
                  pl.BlockSpec((tm, tn), lambda j, i: (i, j + nj))],
        out_specs=pl.BlockSpec((tm, tn), lambda j, i: (i, j)),
        out_shape=jax.ShapeDtypeStruct((T, D), BF16),
        compiler_params=_params(("arbitrary", "arbitrary")),
        name="merge_branches",
    )(g, attn, w_glu, w_glu, w_fox_out, sig_gates, sig_gates)


def _s5_disc_kernel(are_ref, aim_ref, ldt_ref, bre_ref, bim_ref, lre_ref, lim_ref, bbre_ref, bbim_ref):
    lam_re = jnp.minimum(are_ref[...], LAM_RE_MAX)
    lam_im = aim_ref[...]
    dt = jnp.exp(ldt_ref[...])
    mag = jnp.exp(lam_re * dt)
    lb_re = mag * jnp.cos(lam_im * dt)
    lb_im = mag * jnp.sin(lam_im * dt)
    lre_ref[...] = lb_re
    lim_ref[...] = lb_im
    n_re = lb_re - 1.0
    den = lam_re * lam_re + lam_im * lam_im
    co_re = (n_re * lam_re + lb_im * lam_im) / den
    co_im = (lb_im * lam_re - n_re * lam_im) / den
    G, P = co_re.shape
    PH = bre_ref.shape[1]
    hg = PH // P
    rows = lax.broadcasted_iota(jnp.int32, (P, PH), 0)
    cols = lax.broadcasted_iota(jnp.int32, (P, PH), 1)
    expand = jnp.where(cols // hg == rows, 1.0, 0.0).astype(F32)
    ce_re = jnp.dot(co_re, expand, precision=HIGHEST, preferred_element_type=F32)
    ce_im = jnp.dot(co_im, expand, precision=HIGHEST, preferred_element_type=F32)
    b_re = bre_ref[...]
    b_im = bim_ref[...]
    bbre_ref[...] = ce_re * b_re - ce_im * b_im
    bbim_ref[...] = ce_re * b_im + ce_im * b_re


def _s5_discretise(a_re, a_im, log_dt, b_re, b_im):
    G, P = a_re.shape
    hg = b_re.shape[2]
    outs = pl.pallas_call(
        _s5_disc_kernel,
        out_shape=[jax.ShapeDtypeStruct((G, P), F32), jax.ShapeDtypeStruct((G, P), F32),
                   jax.ShapeDtypeStruct((G, P * hg), F32), jax.ShapeDtypeStruct((G, P * hg), F32)],
        name="s5_discretise",
    )(a_re, a_im, log_dt.reshape(G, 1), b_re.reshape(G, P * hg), b_im.reshape(G, P * hg))
    lam_re, lam_im, bb_re, bb_im = outs
    return lam_re, lam_im, bb_re.reshape(G, P, hg), bb_im.reshape(G, P, hg)


def _s5_pack_weights(lam_re, lam_im, bb_re, bb_im, c_re, c_im):
    G, P, hg = bb_re.shape
    nblk = G // SSM_BLOCK_GROUPS
    half_ch = SSM_BLOCK_CH // 2
    pairs_per_half = SSM_PAIRS // 2

    def pair_view(x):
        return x.reshape((nblk, SSM_PAIRS, 2) + x.shape[1:])

    bre = pair_view(bb_re)
    bim = pair_view(bb_im)
    zero = jnp.zeros_like(bre[:, :, 0])

    def rows_of(which):
        re = [bre[:, :, 0], zero] if which == 0 else [zero, bre[:, :, 1]]
        im = [bim[:, :, 0], zero] if which == 0 else [zero, bim[:, :, 1]]
        blk = jnp.concatenate(re + im, axis=2)
        return jnp.swapaxes(blk, 2, 3)

    compact = jnp.concatenate([rows_of(0), rows_of(1)], axis=2)
    local = jnp.arange(SSM_PAIRS) % pairs_per_half
    place = jax.nn.one_hot(local[:, None] * 2 * hg + jnp.arange(2 * hg)[None, :], half_ch, dtype=F32)
    bw = jnp.einsum('prc,bprs->bpcs', place, compact)
    bw_hi = bw.astype(BF16)
    bw_lo = (bw - bw_hi.astype(F32)).astype(BF16)
    bw_hi2 = jnp.concatenate([bw_hi, bw_hi], axis=2)

    cre = pair_view(c_re)
    cim = pair_view(c_im)
    zc = jnp.zeros_like(cre[:, :, 0])

    def cols_of(which):
        re = [cre[:, :, 0], zc] if which == 0 else [zc, cre[:, :, 1]]
        im = [-cim[:, :, 0], zc] if which == 0 else [zc, -cim[:, :, 1]]
        return jnp.concatenate(re + im, axis=3)

    ccompact = jnp.concatenate([cols_of(0), cols_of(1)], axis=2)
    cplace = jax.nn.one_hot(jnp.arange(SSM_PAIRS)[:, None] * 2 * hg + jnp.arange(2 * hg)[None, :],
                            SSM_BLOCK_CH, dtype=F32)
    cw = jnp.einsum('prc,bprs->bpsc', cplace, ccompact).astype(BF16)

    lre = pair_view(lam_re)
    lim = pair_view(lam_im)
    a_re = jnp.concatenate([lre[:, :, 0], lre[:, :, 1]], axis=-1)
    a_im = jnp.concatenate([lim[:, :, 0], lim[:, :, 1]], axis=-1)
    return bw_hi2, bw_lo, cw, jnp.concatenate([a_re, a_im], axis=-1)


def _pack_state(s_re, s_im):
    B, G, P = s_re.shape
    nblk = G // SSM_BLOCK_GROUPS
    re = s_re.reshape(B, nblk, SSM_PAIRS, 2 * P)
    im = s_im.reshape(B, nblk, SSM_PAIRS, 2 * P)
    return jnp.moveaxis(jnp.concatenate([re, im], axis=-1), 1, 0)


def _unpack_state(st):
    nblk, B, pairs, w = st.shape
    st = jnp.moveaxis(st, 0, 1)
    P = w // 4
    re = st[..., :2 * P].reshape(B, nblk * pairs * 2, P)
    im = st[..., 2 * P:].reshape(B, nblk * pairs * 2, P)
    return re, im


def _s5_kernel(*refs, n_u, nseq, lc, rp, ilp):
    u_refs = refs[:n_u]
    bw2_ref, bwlo_ref, cw_ref, lam_ref, d_ref, h0_ref, y_ref, hout_ref, sre_ref, sim_ref, st_ref = refs[n_u:]
    chunk = pl.program_id(1)
    rows = nseq * lc
    half = LANES

    @pl.when(chunk == 0)
    def _():
        st_ref[...] = h0_ref[0]

    lam = lam_ref[0]
    sw = lam.shape[1] // 2
    a_re = lam[:, :sw]
    a_im = lam[:, sw:]

    u = u_refs[0][...] if n_u == 1 else jnp.concatenate([r[...] for r in u_refs], axis=0)
    for hf in range(2):
        uh = u[:, hf * half:(hf + 1) * half]
        hi = uh.astype(BF16)
        lo = (uh - hi.astype(F32)).astype(BF16)
        lhs = jnp.concatenate([hi, lo], axis=1)
        for q in range(SSM_PAIRS // 2):
            p = hf * (SSM_PAIRS // 2) + q
            bu = (jnp.dot(lhs, bw2_ref[0, p], preferred_element_type=F32)
                  + jnp.dot(hi, bwlo_ref[0, p], preferred_element_type=F32))
            sre_ref[pl.ds(p * rp, rows), :] = bu[:, :sw]
            sim_ref[pl.ds(p * rp, rows), :] = bu[:, sw:]

    def group_body(g, carry):
        seqs = [g * ilp + j for j in range(ilp)]

        def step(t, c):
            out = []
            for j, s in enumerate(seqs):
                x_re, x_im = c[2 * j], c[2 * j + 1]
                at = pl.ds(s * lc + t, SSM_PAIRS, stride=rp)
                n_re = a_re * x_re - a_im * x_im + sre_ref[at, :]
                n_im = a_re * x_im + a_im * x_re + sim_ref[at, :]
                sre_ref[at, :] = n_re
                sim_ref[at, :] = n_im
                out += [n_re, n_im]
            return tuple(out)

        init = []
        for s in seqs:
            x0 = st_ref[s]
            init += [x0[:, :sw], x0[:, sw:]]
        fin = lax.fori_loop(0, lc, step, tuple(init), unroll=4)
        for j, s in enumerate(seqs):
            st_ref[s] = jnp.concatenate([fin[2 * j], fin[2 * j + 1]], axis=1)
        return carry

    lax.fori_loop(0, nseq // ilp, group_body, 0)
    hout_ref[0] = st_ref[...]

    y = d_ref[...] * u
    for p in range(SSM_PAIRS):
        at = pl.ds(p * rp, rows)
        states = jnp.concatenate([sre_ref[at, :], sim_ref[at, :]], axis=1).astype(BF16)
        y = y + jnp.dot(states, cw_ref[0, p], preferred_element_type=F32)
    inner = math.sqrt(2.0 / math.pi) * (y + 0.044715 * (y * y * y))
    y = (0.5 * y * (1.0 + jnp.tanh(inner))).astype(y_ref.dtype)
    rows_u = rows // n_u
    for j in range(n_u):
        y_ref[j] = y[j * rows_u:(j + 1) * rows_u]


def _s5_mixer(z, row0, n_u, seq_per_u, seq_len, weights, d_skip, h0, *, lc):
    bw2, bwlo, cw, lam = weights
    nblk = bw2.shape[0]
    nseq = n_u * seq_per_u
    rows_u = seq_per_u * lc
    rows = n_u * rows_u
    nchunk = seq_len // lc
    assert nchunk == 1 or seq_per_u == 1
    rp = rows + SUBLANES
    r0 = row0 // rows_u
    sw4 = lam.shape[-1]
    ilp = 4 if nseq % 4 == 0 else 1
    kern = functools.partial(_s5_kernel, n_u=n_u, nseq=nseq, lc=lc, rp=rp, ilp=ilp)

    def u_spec(j):
        return pl.BlockSpec((rows_u, SSM_BLOCK_CH), lambda b, c: (r0 + j * nchunk + c, b))

    return pl.pallas_call(
        kern,
        grid=(nblk, nchunk),
        in_specs=[u_spec(j) for j in range(n_u)] + [
            pl.BlockSpec((1,) + bw2.shape[1:], lambda b, c: (b, 0, 0, 0)),
            pl.BlockSpec((1,) + bwlo.shape[1:], lambda b, c: (b, 0, 0, 0)),
            pl.BlockSpec((1,) + cw.shape[1:], lambda b, c: (b, 0, 0, 0)),
            pl.BlockSpec((1, SSM_PAIRS, sw4), lambda b, c: (b, 0, 0)),
            pl.BlockSpec((1, SSM_BLOCK_CH), lambda b, c: (0, b)),
            pl.BlockSpec((1, nseq, SSM_PAIRS, sw4), lambda b, c: (b, 0, 0, 0))],
        out_specs=[pl.BlockSpec((n_u, rows_u, SSM_BLOCK_CH), lambda b, c: (0, c, b)),
                   pl.BlockSpec((1, nseq, SSM_PAIRS, sw4), lambda b, c: (b, 0, 0, 0))],
        out_shape=[jax.ShapeDtypeStruct((n_u, nchunk * rows_u, nblk * SSM_BLOCK_CH), BF16),
                   jax.ShapeDtypeStruct((nblk, nseq, SSM_PAIRS, sw4), F32)],
        scratch_shapes=[pltpu.VMEM((SSM_PAIRS * rp, sw4 // 2), F32),
                        pltpu.VMEM((SSM_PAIRS * rp, sw4 // 2), F32),
                        pltpu.VMEM((nseq, SSM_PAIRS, sw4), F32)],
        compiler_params=_params(("arbitrary", "arbitrary")),
        name="s5_mixer",
    )(*([z] * n_u), bw2, bwlo, cw, lam, d_skip.reshape(1, -1), h0)


def _cumsum_kernel(x_ref, o_ref, *, chunk):
    H, L = x_ref.shape[1], x_ref.shape[2]
    carry = jnp.zeros((H, 1), F32)
    for s in range(0, L, chunk):
        w = min(chunk, L - s)
        r = lax.broadcasted_iota(jnp.int32, (w, w), 0)
        c = lax.broadcasted_iota(jnp.int32, (w, w), 1)
        tri = jnp.where(r <= c, 1.0, 0.0).astype(F32)
        part = jnp.dot(x_ref[0, :, s:s + w], tri, precision=HIGHEST, preferred_element_type=F32) + carry
        o_ref[0, :, s:s + w] = part
        carry = part[:, w - 1:w]


def _cumsum_last(x):
    B, H, L = x.shape
    return pl.pallas_call(
        functools.partial(_cumsum_kernel, chunk=256),
        grid=(B,),
        in_specs=[pl.BlockSpec((1, H, L), lambda b: (b, 0, 0))],
        out_specs=pl.BlockSpec((1, H, L), lambda b: (b, 0, 0)),
        out_shape=jax.ShapeDtypeStruct((B, H, L), F32),
        compiler_params=_params(("arbitrary",)),
        name="logf_cumsum",
    )(x)


def _pick_head(c_ref, h):
    blk = c_ref[...]
    lane = lax.broadcasted_iota(jnp.int32, blk.shape, 1)
    return jnp.sum(jnp.where(lane == h, blk, 0.0), axis=1, keepdims=True)


def _nt_dot(a, b):
    return lax.dot_general(a, b, (((1,), (1,)), ((), ())), preferred_element_type=F32)


def _fox_prompt_kernel(q_ref, k_ref, v_ref, cq_ref, ck_ref, o_ref, *, tq, nq, scale):
    h = pl.program_id(1)
    qi = pl.program_id(2)
    q = (q_ref[...] * scale).astype(BF16)
    cq = _pick_head(cq_ref, h)

    def block(ki, carry, diagonal):
        m_prev, l_prev, acc = carry
        at = pl.ds(pl.multiple_of(ki * tq, tq), tq)
        s = _nt_dot(q, k_ref[at, :].astype(BF16)) + cq - ck_ref[0, pl.ds(h * nq + ki, 1), :]
        if diagonal:
            r = lax.broadcasted_iota(jnp.int32, (tq, tq), 0)
            c = lax.broadcasted_iota(jnp.int32, (tq, tq), 1)
            s = jnp.where(c <= r, s, MASK_VALUE)
        m_new = jnp.maximum(m_prev, jnp.max(s, axis=1, keepdims=True))
        alpha = jnp.exp(m_prev - m_new)
        p = jnp.exp(s - m_new)
        l_new = alpha * l_prev + jnp.sum(p, axis=1, keepdims=True)
        acc = alpha * acc + jnp.dot(p.astype(BF16), v_ref[at, :].astype(BF16), preferred_element_type=F32)
        return m_new, l_new, acc

    init = (jnp.full((tq, 1), MASK_VALUE, F32), jnp.zeros((tq, 1), F32), jnp.zeros(o_ref.shape, F32))
    carry = lax.fori_loop(0, qi, lambda ki, c: block(ki, c, False), init)
    _, l, acc = block(qi, carry, True)
    o_ref[...] = (acc / l).astype(o_ref.dtype)


def _fox_prompt(z, c_col, c_row, B, L, H, col_q, col_k, col_v, scale):
    dh = FOX_HEAD_DIM
    tq = _pick(L, 512)
    nq = L // tq
    return pl.pallas_call(
        functools.partial(_fox_prompt_kernel, tq=tq, nq=nq, scale=scale),
        grid=(B, H, nq),
        in_specs=[pl.BlockSpec((tq, dh), lambda b, h, qi: (b * nq + qi, col_q + h)),
                  pl.BlockSpec((L, dh), lambda b, h, qi: (b, col_k + h)),
                  pl.BlockSpec((L, dh), lambda b, h, qi: (b, col_v + h)),
                  pl.BlockSpec((tq, H), lambda b, h, qi: (b * nq + qi, 0)),
                  pl.BlockSpec((1, H * nq, tq), lambda b, h, qi: (b, 0, 0))],
        out_specs=pl.BlockSpec((tq, dh), lambda b, h, qi: (b * nq + qi, h)),
        out_shape=jax.ShapeDtypeStruct((B * L, H * dh), BF16),
        compiler_params=_params(("arbitrary",) * 3),
        name="fox_prompt",
    )(z, z, z, c_col, c_row.reshape(B, H * nq, tq))


def _fox_sample_kernel(q_ref, kn_ref, vn_ref, kc_ref, vc_ref, cq_ref, ck_ref, o_ref, *, past, n_heads, scale):
    dh = FOX_HEAD_DIM
    n = q_ref.shape[0]
    r = lax.broadcasted_iota(jnp.int32, (n, n), 0)
    c = lax.broadcasted_iota(jnp.int32, (n, n), 1)
    causal = c <= r
    cq_all = cq_ref[...]
    outs = []
    for h in range(n_heads):
        cols = slice(h * dh, (h + 1) * dh)
        q = (q_ref[:, cols] * scale).astype(BF16)
        head_rows = pl.ds(h, past, stride=n_heads)
        c_all = ck_ref[0, h:h + 1, :]
        cq = cq_all[:, h:h + 1]
        s_old = _nt_dot(q, kc_ref[0, head_rows, :].astype(BF16)) + cq - c_all[:, :past]
        s_new = _nt_dot(q, kn_ref[:, cols].astype(BF16)) + cq - c_all[:, past:]
        s_new = jnp.where(causal, s_new, MASK_VALUE)
        m = jnp.maximum(jnp.max(s_old, axis=1, keepdims=True), jnp.max(s_new, axis=1, keepdims=True))
        p_old = jnp.exp(s_old - m)
        p_new = jnp.exp(s_new - m)
        l = jnp.sum(p_old, axis=1, keepdims=True) + jnp.sum(p_new, axis=1, keepdims=True)
        acc = (jnp.dot(p_old.astype(BF16), vc_ref[0, head_rows, :].astype(BF16), preferred_element_type=F32)
               + jnp.dot(p_new.astype(BF16), vn_ref[:, cols].astype(BF16), preferred_element_type=F32))
        outs.append((acc / l).astype(o_ref.dtype))
    o_ref[...] = jnp.concatenate(outs, axis=1)


def _fox_sample(z, row0, cache_k, cache_v, c_col, c_row, B, n, H, col_q, col_k, col_v, scale):
    dh = FOX_HEAD_DIM
    past = cache_k.shape[1] // H
    r0 = row0 // n
    width = H * dh
    assert (col_q * dh) % width == 0 and (col_k * dh) % width == 0 and (col_v * dh) % width == 0
    return pl.pallas_call(
        functools.partial(_fox_sample_kernel, past=past, n_heads=H, scale=scale),
        grid=(B,),
        in_specs=[pl.BlockSpec((n, width), lambda b: (r0 + b, col_q * dh // width)),
                  pl.BlockSpec((n, width), lambda b: (r0 + b, col_k * dh // width)),
                  pl.BlockSpec((n, width), lambda b: (r0 + b, col_v * dh // width)),
                  pl.BlockSpec((1, past * H, dh), lambda b: (b, 0, 0)),
                  pl.BlockSpec((1, past * H, dh), lambda b: (b, 0, 0)),
                  pl.BlockSpec((n, H), lambda b: (b, 0)),
                  pl.BlockSpec((1, H, past + n), lambda b: (b, 0, 0))],
        out_specs=pl.BlockSpec((n, width), lambda b: (b, 0)),
        out_shape=jax.ShapeDtypeStruct((B * n, width), BF16),
        compiler_params=_params(("arbitrary",)),
        name="fox_sample",
    )(z, z, z, cache_k, cache_v, c_col, c_row)


def _gather_rows_kernel(tok_ref, used_ref, h_hbm, o_ref, buf, sem, *, tm):
    i = pl.program_id(0)
    n = used_ref[0]

    def row_copy(blk, slot, r):
        t = tok_ref[blk * tm + r]
        return pltpu.make_async_copy(h_hbm.at[pl.ds(t, 1), :], buf.at[slot, pl.ds(r, 1), :], sem.at[slot])

    def issue(blk, slot):
        def body(r, c):
            row_copy(blk, slot, r).start()
            return c
        lax.fori_loop(0, tm, body, 0, unroll=8)

    def drain(blk, slot):
        def body(r, c):
            row_copy(blk, slot, r).wait()
            return c
        lax.fori_loop(0, tm, body, 0, unroll=8)

    @pl.when(jnp.logical_and(i == 0, n > 0))
    def _():
        issue(0, 0)

    @pl.when(i + 1 < n)
    def _():
        issue(i + 1, (i + 1) % 2)

    @pl.when(i < n)
    def _():
        slot = i % 2
        drain(i, slot)
        o_ref[...] = buf[slot].astype(o_ref.dtype)

    @pl.when(i >= n)
    def _():
        o_ref[...] = jnp.zeros(o_ref.shape, o_ref.dtype)


def _gather_rows(h, slot_tok, n_used, tm):
    n_slots = slot_tok.shape[0]
    D = h.shape[1]
    return pl.pallas_call(
        functools.partial(_gather_rows_kernel, tm=tm),
        grid_spec=pltpu.PrefetchScalarGridSpec(
            num_scalar_prefetch=2,
            grid=(n_slots // tm,),
            in_specs=[pl.BlockSpec(memory_space=pl.ANY)],
            out_specs=pl.BlockSpec((tm, D), lambda i, tok, used: (i, 0)),
            scratch_shapes=[pltpu.VMEM((2, tm, D), h.dtype), pltpu.SemaphoreType.DMA((2,))]),
        out_shape=jax.ShapeDtypeStruct((n_slots, D), BF16),
        compiler_params=_params(("arbitrary",)),
        name="moe_dispatch_gather",
    )(slot_tok, n_used, h)


def _grouped_kernel(blk0_ref, nblk_ref, x_hbm, *refs, n_w, tm, tn):
    w_refs = refs[:n_w]
    o_hbm = refs[n_w]
    w_bf = refs[n_w + 1:2 * n_w + 1]
    xbuf, obuf, tbuf, in_sem, out_sem, tail_sem = refs[2 * n_w + 1:]
    col = pl.program_id(0)
    e = pl.program_id(1)
    nblk = nblk_ref[e]
    blk0 = blk0_ref[e]
    npair = nblk // 2

    def x_copy(j, slot):
        return pltpu.make_async_copy(x_hbm.at[pl.ds((blk0 + j) * tm, tm), :], xbuf.at[slot], in_sem.at[slot])

    def pair_out(p, slot):
        return pltpu.make_async_copy(obuf.at[slot],
                                     o_hbm.at[pl.ds((blk0 + 2 * p) * tm, 2 * tm), pl.ds(col * tn, tn)],
                                     out_sem.at[slot])

    def tail_out():
        return pltpu.make_async_copy(tbuf, o_hbm.at[pl.ds((blk0 + nblk - 1) * tm, tm), pl.ds(col * tn, tn)],
                                     tail_sem)

    def apply(x):
        if n_w == 2:
            a = jnp.dot(x, w_bf[0][...], preferred_element_type=F32)
            b = jnp.dot(x, w_bf[1][...], preferred_element_type=F32)
            return a * _sigmoid(a) * b
        return jnp.dot(x, w_bf[0][...], preferred_element_type=F32)

    @pl.when(nblk > 0)
    def _():
        x_copy(0, 0).start()

        @pl.when(nblk > 1)
        def _():
            x_copy(1, 1).start()

        for w_ref, dst in zip(w_refs, w_bf):
            dst[...] = w_ref[0].astype(BF16)

        def pair_body(p, carry):
            s0 = (p % 2) * 2

            @pl.when(2 * p + 2 < nblk)
            def _():
                x_copy(2 * p + 2, 2 - s0).start()

            @pl.when(2 * p + 3 < nblk)
            def _():
                x_copy(2 * p + 3, 3 - s0).start()

            x_copy(2 * p, s0).wait()
            x_copy(2 * p + 1, s0 + 1).wait()
            y = apply(xbuf[pl.ds(s0, 2)].reshape(2 * tm, xbuf.shape[2]))
            oslot = p % 2

            @pl.when(p >= 2)
            def _():
                pair_out(p - 2, oslot).wait()

            obuf[oslot] = y.astype(obuf.dtype)
            pair_out(p, oslot).start()
            return carry

        lax.fori_loop(0, npair, pair_body, 0)

        @pl.when(nblk % 2 == 1)
        def _():
            slot = (npair % 2) * 2
            x_copy(nblk - 1, slot).wait()
            tbuf[...] = apply(xbuf[slot]).astype(tbuf.dtype)
            tail_out().start()

        @pl.when(npair >= 2)
        def _():
            pair_out(npair - 2, npair % 2).wait()

        @pl.when(npair >= 1)
        def _():
            pair_out(npair - 1, (npair - 1) % 2).wait()

        @pl.when(nblk % 2 == 1)
        def _():
            tail_out().wait()

    @pl.when(e == pl.num_programs(1) - 1)
    def _():
        first = blk0 + nblk
        total = o_hbm.shape[0] // tm

        def zero_copy(j):
            return pltpu.make_async_copy(tbuf, o_hbm.at[pl.ds(j * tm, tm), pl.ds(col * tn, tn)], tail_sem)

        @pl.when(first < total)
        def _():
            tbuf[...] = jnp.zeros(tbuf.shape, tbuf.dtype)

        def start(j, c):
            zero_copy(j).start()
            return c

        def wait(j, c):
            zero_copy(j).wait()
            return c

        lax.fori_loop(first, total, start, 0)
        lax.fori_loop(first, total, wait, 0)


def _grouped_matmul(x, weights, blk0, nblk, tm, tn_pref, out_dtype, name):
    n_rows, K = x.shape
    E, _, N = weights[0].shape
    n_w = len(weights)
    tn = _pick(N, tn_pref, LANES)
    return pl.pallas_call(
        functools.partial(_grouped_kernel, n_w=n_w, tm=tm, tn=tn),
        grid_spec=pltpu.PrefetchScalarGridSpec(
            num_scalar_prefetch=2,
            grid=(N // tn, E),
            in_specs=[pl.BlockSpec(memory_space=pl.ANY)]
            + [pl.BlockSpec((1, K, tn), lambda c, e, b0, nb: (e, 0, c)) for _ in weights],
            out_specs=pl.BlockSpec(memory_space=pl.ANY),
            scratch_shapes=[pltpu.VMEM((K, tn), BF16) for _ in weights] + [
                pltpu.VMEM((4, tm, K), x.dtype),
                pltpu.VMEM((2, 2 * tm, tn), out_dtype),
                pltpu.VMEM((tm, tn), out_dtype),
                pltpu.SemaphoreType.DMA((4,)),
                pltpu.SemaphoreType.DMA((2,)),
                pltpu.SemaphoreType.DMA(())]),
        out_shape=jax.ShapeDtypeStruct((n_rows, N), out_dtype),
        compiler_params=_params(("arbitrary", "arbitrary")),
        name=name,
    )(blk0, nblk, x, *weights)


def _combine_kernel(dest_ref, x_ref, sh_ref, gate_ref, g_ref, rows_hbm, o_ref, buf, sem, *, tc, final_norm,
                    row0):
    i = pl.program_id(0)
    n = pl.num_programs(0)

    def row_copy(blk, slot, r, k):
        src = dest_ref[(row0 + blk * tc + r) * TOP_K + k]
        return pltpu.make_async_copy(rows_hbm.at[pl.ds(src, 1), :], buf.at[slot, k, pl.ds(r, 1), :],
                                     sem.at[slot])

    def issue(blk, slot):
        def body(r, c):
            for k in range(TOP_K):
                row_copy(blk, slot, r, k).start()
            return c
        lax.fori_loop(0, tc, body, 0)

    def drain(blk, slot):
        def body(r, c):
            for k in range(TOP_K):
                row_copy(blk, slot, r, k).wait()
            return c
        lax.fori_loop(0, tc, body, 0)

    @pl.when(i == 0)
    def _():
        issue(0, 0)

    @pl.when(i + 1 < n)
    def _():
        issue(i + 1, (i + 1) % 2)

    slot = i % 2
    drain(i, slot)
    y = x_ref[...] + sh_ref[...]
    gates = gate_ref[...]
    for k in range(TOP_K):
        y = y + gates[:, k:k + 1] * buf[slot, k]
    o_ref[...] = _rms(y, g_ref[...]) if final_norm else y


def _combine(x, shared, gates, dest, rows, g_final, final_norm, row0, n_rows, tc=32):
    D = x.shape[1]
    tc = _pick(math.gcd(row0, n_rows) if row0 else n_rows, tc)
    b0 = row0 // tc
    return pl.pallas_call(
        functools.partial(_combine_kernel, tc=tc, final_norm=final_norm, row0=row0),
        grid_spec=pltpu.PrefetchScalarGridSpec(
            num_scalar_prefetch=1,
            grid=(n_rows // tc,),
            in_specs=[pl.BlockSpec((tc, D), lambda i, d: (b0 + i, 0)),
                      pl.BlockSpec((tc, D), lambda i, d: (b0 + i, 0)),
                      pl.BlockSpec((tc, TOP_K), lambda i, d: (b0 + i, 0)),
                      pl.BlockSpec((1, D), lambda i, d: (0, 0)),
                      pl.BlockSpec(memory_space=pl.ANY)],
            out_specs=pl.BlockSpec((tc, D), lambda i, d: (i, 0)),
            scratch_shapes=[pltpu.VMEM((2, TOP_K, tc, D), F32), pltpu.SemaphoreType.DMA((2,))]),
        out_shape=jax.ShapeDtypeStruct((n_rows, D), F32),
        compiler_params=_params(("arbitrary",)),
        name="moe_combine_norm",
    )(dest, x, shared, gates, g_final.reshape(1, D), rows)


def _slot_assign_kernel(start_ref, idx_ref, rank_ref, dest_ref, *, n_experts):
    idx = idx_ref[...]
    base = jnp.zeros(idx.shape, jnp.int32)
    for e in range(n_experts):
        base = jnp.where(idx == e, start_ref[e], base)
    dest_ref[...] = base + rank_ref[...]


def _routing_tables(idx, rank, counts, n_experts, tm):
    T = idx.shape[0]
    n_assign = T * TOP_K
    n_blocks = (n_assign + n_experts * (tm - 1) + tm - 1) // tm
    counts = counts.reshape(n_experts).astype(jnp.int32)
    padded = (counts + tm - 1) // tm * tm
    pad_end = jnp.cumsum(padded)
    pad_start = (pad_end - padded).astype(jnp.int32)
    shape2d = (n_assign // LANES, LANES) if n_assign % LANES == 0 else (T, TOP_K)
    dest = pl.pallas_call(
        functools.partial(_slot_assign_kernel, n_experts=n_experts),
        grid_spec=pltpu.PrefetchScalarGridSpec(
            num_scalar_prefetch=1, grid=(1,),
            in_specs=[pl.BlockSpec(shape2d, lambda i, s: (0, 0)), pl.BlockSpec(shape2d, lambda i, s: (0, 0))],
            out_specs=pl.BlockSpec(shape2d, lambda i, s: (0, 0))),
        out_shape=jax.ShapeDtypeStruct(shape2d, jnp.int32),
        name="moe_slot_assign",
    )(pad_start, idx.reshape(shape2d), rank.reshape(shape2d)).reshape(n_assign)
    slot_tok = jnp.zeros((n_blocks * tm,), jnp.int32).at[dest].set(
        jnp.arange(n_assign, dtype=jnp.int32) // TOP_K)
    return dest, slot_tok, (pad_start // tm).astype(jnp.int32), (padded // tm).astype(jnp.int32)


def kernel(x_prompt, x_sample, cache_k, cache_v, cache_logf, state_ssm_re, state_ssm_im, norm_mix, w_in, b_forget, ssm_a_re, ssm_a_im, ssm_log_dt, ssm_b_re, ssm_b_im, ssm_c_re, ssm_c_im, ssm_d, w_glu, w_fox_out, w_out, norm_ffn, w_router, router_bias, w_exp_gate, w_exp_up, w_exp_down, w_sh_gate, w_sh_up, w_sh_down, norm_final):
    Bp, Lp, D = x_prompt.shape
    Bs, Ls, _ = x_sample.shape
    depth = w_in.shape[0]
    past = cache_k.shape[2]
    H = cache_k.shape[3]
    dh = cache_k.shape[4]
    assert dh == FOX_HEAD_DIM
    d_fox = H * dh
    d_ssm = ssm_d.shape[1]
    n_experts = w_router.shape[2]
    Tp = Bp * Lp
    Ts = Bs * Ls
    T = Tp + Ts
    scale = dh ** -0.5
    n_uqkv = d_ssm + 3 * d_fox
    col_q = d_ssm // dh
    col_k = col_q + H
    col_v = col_k + H
    moe_tm = 256

    x = jnp.concatenate([x_prompt.reshape(Tp, D), x_sample.reshape(Ts, D)], axis=0)
    outs = {name: [] for name in ("kp", "vp", "fp", "rp", "ip", "ks", "vs", "fs", "rs", "is")}
    for l in range(depth):
        w_uqkv = w_in[l][:, :n_uqkv].astype(BF16)
        w_f = w_in[l][:, n_uqkv:n_uqkv + H]
        w_gates = w_in[l][:, n_uqkv + H:].astype(BF16)

        h, logf = _norm_forget(x, norm_mix[l], w_f, b_forget[l])
        z = _matmul(h, w_uqkv, F32, name="in_proj")
        sig_gates = _matmul(h, w_gates, BF16, sigmoid=True, name="gate_proj")

        lam_re, lam_im, bb_re, bb_im = _s5_discretise(ssm_a_re[l], ssm_a_im[l], ssm_log_dt[l],
                                                      ssm_b_re[l], ssm_b_im[l])
        s5_w = _s5_pack_weights(lam_re, lam_im, bb_re, bb_im, ssm_c_re[l], ssm_c_im[l])
        zero_state = jnp.zeros((Bp,) + state_ssm_re.shape[2:], F32)
        g_p, st_p = _s5_mixer(z, 0, Bp, 1, Lp, s5_w, ssm_d[l], _pack_state(zero_state, zero_state),
                              lc=_pick(Lp, 256))
        g_s, st_s = _s5_mixer(z, Tp, 1, Bs, Ls, s5_w, ssm_d[l],
                              _pack_state(state_ssm_re[l].astype(F32), state_ssm_im[l].astype(F32)), lc=Ls)
        g = jnp.concatenate([g_p.reshape(Tp, d_ssm), g_s.reshape(Ts, d_ssm)], axis=0)

        logf_p = logf[:Tp].reshape(Bp, Lp, H)
        logf_s = logf[Tp:].reshape(Bs, Ls, H)
        crow_p = _cumsum_last(jnp.swapaxes(logf_p, 1, 2))
        ccol_p = jnp.swapaxes(crow_p, 1, 2).reshape(Tp, H)
        attn_p = _fox_prompt(z, ccol_p, crow_p, Bp, Lp, H, col_q, col_k, col_v, scale)
        lf_all = jnp.concatenate([cache_logf[l].astype(F32), logf_s], axis=1)
        crow_s = _cumsum_last(jnp.swapaxes(lf_all, 1, 2))
        ccol_s = jnp.swapaxes(crow_s[:, :, past:], 1, 2).reshape(Ts, H)
        attn_s = _fox_sample(z, Tp, cache_k[l].reshape(Bs, past * H, dh), cache_v[l].reshape(Bs, past * H, dh),
                             ccol_s, crow_s, Bs, Ls, H, col_q, col_k, col_v, scale)
        attn = jnp.concatenate([attn_p, attn_s], axis=0)

        merged = _merge(g, attn, w_glu[l].astype(BF16), w_fox_out[l].astype(BF16), sig_gates)
        x = _matmul_residual(merged, w_out[l].astype(BF16), x)

        norm_out = norm_ffn[l]
        h2, h2f, idx, gates, rank, counts = _norm_router(x, norm_out, w_router[l], router_bias[l])
        dest, slot_tok, e_blk0, e_nblk = _routing_tables(idx, rank, counts, n_experts, moe_tm)
        xs = _gather_rows(h2f, slot_tok, (e_blk0[-1:] + e_nblk[-1:]), moe_tm)
        act = _grouped_matmul(xs, (w_exp_gate[l], w_exp_up[l]), e_blk0, e_nblk, moe_tm, 512, BF16, "expert_up")
        routed = _grouped_matmul(act, (w_exp_down[l],), e_blk0, e_nblk, moe_tm, 2048, F32, "expert_down")
        sh_tm = _pick(T, moe_tm, 16)
        sh_blk0 = jnp.zeros((1,), jnp.int32)
        sh_nblk = jnp.full((1,), T // sh_tm, jnp.int32)
        sh_act = _grouped_matmul(h2, (w_sh_gate[l][None], w_sh_up[l][None]), sh_blk0, sh_nblk, sh_tm, 512, BF16,
                                 "shared_up")
        shared = _grouped_matmul(sh_act, (w_sh_down[l][None],), sh_blk0, sh_nblk, sh_tm, 2048, F32,
                                 "shared_down")
        if l == depth - 1:
            y_p = _combine(x, shared, gates, dest, routed, norm_final, True, 0, Tp)
            y_s = _combine(x, shared, gates, dest, routed, norm_final, True, Tp, Ts)
        else:
            x = _combine(x, shared, gates, dest, routed, norm_final, False, 0, T)

        k_all = z[:, d_ssm + d_fox:d_ssm + 2 * d_fox]
        v_all = z[:, d_ssm + 2 * d_fox:d_ssm + 3 * d_fox]
        re_p, im_p = _unpack_state(st_p)
        re_s, im_s = _unpack_state(st_s)
        outs["kp"].append(k_all[:Tp].reshape(Bp, Lp, H, dh))
        outs["vp"].append(v_all[:Tp].reshape(Bp, Lp, H, dh))
        outs["fp"].append(logf_p)
        outs["rp"].append(re_p)
        outs["ip"].append(im_p)
        outs["ks"].append(k_all[Tp:].reshape(Bs, Ls, H, dh))
        outs["vs"].append(v_all[Tp:].reshape(Bs, Ls, H, dh))
        outs["fs"].append(logf_s)
        outs["rs"].append(re_s)
        outs["is"].append(im_s)

    return (y_p.reshape(Bp, Lp, D), y_s.reshape(Bs, Ls, D),
            jnp.stack(outs["kp"]), jnp.stack(outs["vp"]), jnp.stack(outs["fp"]),
            jnp.stack(outs["rp"]), jnp.stack(outs["ip"]),
            jnp.stack(outs["ks"]), jnp.stack(outs["vs"]), jnp.stack(outs["fs"]),
            jnp.stack(outs["rs"]), jnp.stack(outs["is"]))
```

```python
import functools
import math

import jax
import jax.numpy as jnp
from jax import lax
from jax.experimental import pallas as pl
from jax.experimental.pallas import tpu as pltpu

F32 = jnp.float32
BF16 = jnp.bfloat16
HIGHEST = lax.Precision.HIGHEST

SSM_GROUP = 16
FOX_HEAD_DIM = 128
TOP_K = 8
ROUTED_SCALE = 2.5
NORM_EPS = 1e-6
LAM_RE_MAX = -1e-4

V7X_VMEM_BYTES = 64 * 1024 * 1024
VMEM_LIMIT = V7X_VMEM_BYTES - 8 * 1024 * 1024
LANES = 128
SUBLANES = 8
SSM_BLOCK_GROUPS = 16
SSM_BLOCK_CH = SSM_BLOCK_GROUPS * SSM_GROUP
SSM_PAIRS = SSM_BLOCK_GROUPS // 2
MASK_VALUE = -1e30


def _pick(n, pref, mult=SUBLANES):
    best = None
    d = mult
    while d <= min(n, pref):
        if n % d == 0:
            best = d
        d += mult
    return best if best is not None else n


def _params(sem, vmem=VMEM_LIMIT):
    return pltpu.CompilerParams(dimension_semantics=sem, vmem_limit_bytes=vmem)


def _sigmoid(x):
    return 1.0 / (1.0 + jnp.exp(-x))


def _rms(x, g):
    r = lax.rsqrt(jnp.mean(x * x, axis=-1, keepdims=True) + NORM_EPS)
    return x * r * g


def _norm_forget_kernel(x_ref, g_ref, w_ref, b_ref, h_ref, logf_ref):
    h = _rms(x_ref[...], g_ref[...])
    h_ref[...] = h.astype(h_ref.dtype)
    z = lax.dot_general(h, w_ref[...], (((1,), (1,)), ((), ())), precision=HIGHEST,
                        preferred_element_type=F32) + b_ref[...]
    logf_ref[...] = jnp.minimum(z, 0.0) - jnp.log1p(jnp.exp(-jnp.abs(z)))


def _norm_forget(x, g, wt_f, b_f):
    T, D = x.shape
    H = wt_f.shape[0]
    tm = _pick(T, 256)
    return pl.pallas_call(
        _norm_forget_kernel,
        grid=(T // tm,),
        in_specs=[pl.BlockSpec((tm, D), lambda i: (i, 0)),
                  pl.BlockSpec((1, D), lambda i: (0, 0)),
                  pl.BlockSpec((H, D), lambda i: (0, 0)),
                  pl.BlockSpec((1, H), lambda i: (0, 0))],
        out_specs=[pl.BlockSpec((tm, D), lambda i: (i, 0)),
                   pl.BlockSpec((tm, H), lambda i: (i, 0))],
        out_shape=[jax.ShapeDtypeStruct((T, D), BF16), jax.ShapeDtypeStruct((T, H), F32)],
        compiler_params=_params(("arbitrary",)),
        name="norm_forget",
    )(x, g.reshape(1, D), wt_f, b_f.reshape(1, H))


def _norm_router_kernel(x_ref, g_ref, w_ref, b_ref, h_ref, hp_ref, idx_ref, gate_ref, rank_ref, cnt_ref):
    @pl.when(pl.program_id(0) == 0)
    def _():
        cnt_ref[...] = jnp.zeros(cnt_ref.shape, F32)

    h = _rms(x_ref[...], g_ref[...])
    hb = h.astype(BF16)
    h_ref[...] = hb
    bits = lax.bitcast_convert_type(hb.astype(F32), jnp.uint32)
    half = bits.shape[1] // 2
    hp_ref[...] = (bits[:, half:] & jnp.uint32(0xFFFF0000)) | (bits[:, :half] >> 16)
    scores = _sigmoid(jnp.dot(h, w_ref[...], precision=HIGHEST, preferred_element_type=F32))
    tm, E = scores.shape
    lane = lax.broadcasted_iota(jnp.int32, (tm, E), 1).astype(F32)
    col = lax.broadcasted_iota(jnp.int32, (tm, TOP_K), 1)
    work = scores + b_ref[...]
    idx = jnp.zeros((tm, TOP_K), F32)
    sel = jnp.zeros((tm, TOP_K), F32)
    chosen = jnp.zeros((tm, E), F32)
    hits = []
    for k in range(TOP_K):
        m = jnp.max(work, axis=1, keepdims=True)
        am = jnp.min(jnp.where(work == m, lane, float(E)), axis=1, keepdims=True)
        hit = lane == am
        hits.append(hit)
        sk = jnp.sum(jnp.where(hit, scores, 0.0), axis=1, keepdims=True)
        idx = jnp.where(col == k, am, idx)
        sel = jnp.where(col == k, sk, sel)
        chosen = jnp.where(hit, 1.0, chosen)
        work = jnp.where(hit, -jnp.inf, work)
    idx_ref[...] = idx.astype(jnp.int32)
    gate_ref[...] = ROUTED_SCALE * sel / jnp.sum(sel, axis=1, keepdims=True)

    r = lax.broadcasted_iota(jnp.int32, (tm, tm), 0)
    c = lax.broadcasted_iota(jnp.int32, (tm, tm), 1)
    lower = jnp.where(c < r, 1.0, 0.0).astype(BF16)
    before = jnp.dot(lower, chosen.astype(BF16), preferred_element_type=F32) + cnt_ref[...]
    rank = jnp.zeros((tm, TOP_K), F32)
    for k in range(TOP_K):
        rank = jnp.where(col == k, jnp.sum(jnp.where(hits[k], before, 0.0), axis=1, keepdims=True), rank)
    rank_ref[...] = rank.astype(jnp.int32)
    cnt_ref[...] = cnt_ref[...] + jnp.sum(chosen, axis=0, keepdims=True)


def _norm_router(x, g, w_router, router_bias):
    T, D = x.shape
    E = w_router.shape[1]
    tm = _pick(T, 256)
    return pl.pallas_call(
        _norm_router_kernel,
        grid=(T // tm,),
        in_specs=[pl.BlockSpec((tm, D), lambda i: (i, 0)),
                  pl.BlockSpec((1, D), lambda i: (0, 0)),
                  pl.BlockSpec((D, E), lambda i: (0, 0)),
                  pl.BlockSpec((1, E), lambda i: (0, 0))],
        out_specs=[pl.BlockSpec((tm, D), lambda i: (i, 0)),
                   pl.BlockSpec((tm, D // 2), lambda i: (i, 0)),
                   pl.BlockSpec((tm, TOP_K), lambda i: (i, 0)),
                   pl.BlockSpec((tm, TOP_K), lambda i: (i, 0)),
                   pl.BlockSpec((tm, TOP_K), lambda i: (i, 0)),
                   pl.BlockSpec((1, E), lambda i: (0, 0))],
        out_shape=[jax.ShapeDtypeStruct((T, D), BF16), jax.ShapeDtypeStruct((T, D // 2), jnp.uint32),
                   jax.ShapeDtypeStruct((T, TOP_K), jnp.int32), jax.ShapeDtypeStruct((T, TOP_K), F32),
                   jax.ShapeDtypeStruct((T, TOP_K), jnp.int32), jax.ShapeDtypeStruct((1, E), F32)],
        compiler_params=_params(("arbitrary",)),
        name="norm_router",
    )(x, g.reshape(1, D), w_router, router_bias.reshape(1, E))


def _mm_kernel(a_ref, w_ref, o_ref, *, sigmoid):
    acc = jnp.dot(a_ref[...], w_ref[...], preferred_element_type=F32)
    if sigmoid:
        acc = _sigmoid(acc)
    o_ref[...] = acc.astype(o_ref.dtype)


def _matmul(a, w, out_dtype, *, sigmoid=False, tm_pref=512, tn_pref=1024, name="matmul"):
    M, K = a.shape
    N = w.shape[1]
    tm = _pick(M, tm_pref, 16)
    tn = _pick(N, tn_pref, LANES)
    return pl.pallas_call(
        functools.partial(_mm_kernel, sigmoid=sigmoid),
        grid=(N // tn, M // tm),
        in_specs=[pl.BlockSpec((tm, K), lambda j, i: (i, 0)),
                  pl.BlockSpec((K, tn), lambda j, i: (0, j))],
        out_specs=pl.BlockSpec((tm, tn), lambda j, i: (i, j)),
        out_shape=jax.ShapeDtypeStruct((M, N), out_dtype),
        compiler_params=_params(("arbitrary", "arbitrary")),
        name=name,
    )(a, w)


def _mm_wt_kernel(a_ref, wt_ref, o_ref, w_s, *, sigmoid):
    @pl.when(pl.program_id(1) == 0)
    def _():
        w_s[...] = wt_ref[...].T.astype(w_s.dtype)

    acc = jnp.dot(a_ref[...], w_s[...], preferred_element_type=F32)
    if sigmoid:
        acc = _sigmoid(acc)
    o_ref[...] = acc.astype(o_ref.dtype)


def _matmul_wt(a, wt, n_rows_w, out_dtype, *, sigmoid=False, tm_pref=512, tn_pref=512, name="matmul_wt"):
    M, K = a.shape
    tm = _pick(M, tm_pref, 16)
    tn = _pick(n_rows_w, tn_pref, LANES)
    return pl.pallas_call(
        functools.partial(_mm_wt_kernel, sigmoid=sigmoid),
        grid=(n_rows_w // tn, M // tm),
        in_specs=[pl.BlockSpec((tm, K), lambda j, i: (i, 0)),
                  pl.BlockSpec((tn, K), lambda j, i: (j, 0))],
        out_specs=pl.BlockSpec((tm, tn), lambda j, i: (i, j)),
        out_shape=jax.ShapeDtypeStruct((M, n_rows_w), out_dtype),
        scratch_shapes=[pltpu.VMEM((K, tn), BF16)],
        compiler_params=_params(("arbitrary", "arbitrary")),
        name=name,
    )(a, wt)


def _mm_residual_kernel(a_ref, w_ref, r_ref, o_ref):
    o_ref[...] = r_ref[...] + jnp.dot(a_ref[...], w_ref[...], preferred_element_type=F32)


def _matmul_residual(a, w, res, *, tm_pref=512, tn_pref=1024):
    M, K = a.shape
    N = w.shape[1]
    tm = _pick(M, tm_pref, 16)
    tn = _pick(N, tn_pref, LANES)
    return pl.pallas_call(
        _mm_residual_kernel,
        grid=(N // tn, M // tm),
        in_specs=[pl.BlockSpec((tm, K), lambda j, i: (i, 0)),
                  pl.BlockSpec((K, tn), lambda j, i: (0, j)),
                  pl.BlockSpec((tm, tn), lambda j, i: (i, j))],
        out_specs=pl.BlockSpec((tm, tn), lambda j, i: (i, j)),
        out_shape=jax.ShapeDtypeStruct((M, N), F32),
        compiler_params=_params(("arbitrary", "arbitrary")),
        name="out_proj_residual",
    )(a, w, res)


def _merge_kernel(g_ref, at_ref, wv_ref, wg_ref, wf_ref, sa_ref, sb_ref, o_ref):
    g = g_ref[...]
    val = jnp.dot(g, wv_ref[...], preferred_element_type=F32)
    gate = jnp.dot(g, wg_ref[...], preferred_element_type=F32)
    y_b = jnp.dot(at_ref[...], wf_ref[...], preferred_element_type=F32)
    y_a = val * _sigmoid(gate)
    o_ref[...] = (sa_ref[...].astype(F32) * y_a + sb_ref[...].astype(F32) * y_b).astype(o_ref.dtype)


def _merge(g, attn, w_glu, w_fox_out, sig_gates, *, tm_pref=512, tn_pref=512):
    T, K = g.shape
    D = w_fox_out.shape[1]
    tm = _pick(T, tm_pref, 16)
    tn = _pick(D, tn_pref, LANES)
    nj = D // tn
    return pl.pallas_call(
        _merge_kernel,
        grid=(nj, T // tm),
        in_specs=[pl.BlockSpec((tm, K), lambda j, i: (i, 0)),
                  pl.BlockSpec((tm, K), lambda j, i: (i, 0)),
                  pl.BlockSpec((K, tn), lambda j, i: (0, j)),
                  pl.BlockSpec((K, tn), lambda j, i: (0, j + nj)),
                  pl.BlockSpec((K, tn), lambda j, i: (0, j)),
                  pl.BlockSpec((tm, tn), lambda j, i: (i, j)),
                  pl.BlockSpec((tm, tn), lambda j, i: (i, j + nj))],
        out_specs=pl.BlockSpec((tm, tn), lambda j, i: (i, j)),
        out_shape=jax.ShapeDtypeStruct((T, D), BF16),
        compiler_params=_params(("arbitrary", "arbitrary")),
        name="merge_branches",
    )(g, attn, w_glu, w_glu, w_fox_out, sig_gates, sig_gates)


def _s5_disc_kernel(are_ref, aim_ref, ldt_ref, bre_ref, bim_ref, lre_ref, lim_ref, bbre_ref, bbim_ref):
    lam_re = jnp.minimum(are_ref[...], LAM_RE_MAX)
    lam_im = aim_ref[...]
    dt = jnp.exp(ldt_ref[...])
    mag = jnp.exp(lam_re * dt)
    lb_re = mag * jnp.cos(lam_im * dt)
    lb_im = mag * jnp.sin(lam_im * dt)
    lre_ref[...] = lb_re
    lim_ref[...] = lb_im
    n_re = lb_re - 1.0
    den = lam_re * lam_re + lam_im * lam_im
    co_re = (n_re * lam_re + lb_im * lam_im) / den
    co_im = (lb_im * lam_re - n_re * lam_im) / den
    G, P = co_re.shape
    PH = bre_ref.shape[1]
    hg = PH // P
    rows = lax.broadcasted_iota(jnp.int32, (P, PH), 0)
    cols = lax.broadcasted_iota(jnp.int32, (P, PH), 1)
    expand = jnp.where(cols // hg == rows, 1.0, 0.0).astype(F32)
    ce_re = jnp.dot(co_re, expand, precision=HIGHEST, preferred_element_type=F32)
    ce_im = jnp.dot(co_im, expand, precision=HIGHEST, preferred_element_type=F32)
    b_re = bre_ref[...]
    b_im = bim_ref[...]
    bbre_ref[...] = ce_re * b_re - ce_im * b_im
    bbim_ref[...] = ce_re * b_im + ce_im * b_re


def _s5_discretise(a_re, a_im, log_dt, b_re, b_im):
    G, P = a_re.shape
    hg = b_re.shape[2]
    outs = pl.pallas_call(
        _s5_disc_kernel,
        out_shape=[jax.ShapeDtypeStruct((G, P), F32), jax.ShapeDtypeStruct((G, P), F32),
                   jax.ShapeDtypeStruct((G, P * hg), F32), jax.ShapeDtypeStruct((G, P * hg), F32)],
        name="s5_discretise",
    )(a_re, a_im, log_dt.reshape(G, 1), b_re.reshape(G, P * hg), b_im.reshape(G, P * hg))
    lam_re, lam_im, bb_re, bb_im = outs
    return lam_re, lam_im, bb_re.reshape(G, P, hg), bb_im.reshape(G, P, hg)


def _s5_pack_weights(lam_re, lam_im, bb_re, bb_im, c_re, c_im):
    G, P, hg = bb_re.shape
    nblk = G // SSM_BLOCK_GROUPS
    half_ch = SSM_BLOCK_CH // 2
    pairs_per_half = SSM_PAIRS // 2

    def pair_view(x):
        return x.reshape((nblk, SSM_PAIRS, 2) + x.shape[1:])

    bre = pair_view(bb_re)
    bim = pair_view(bb_im)
    zero = jnp.zeros_like(bre[:, :, 0])

    def rows_of(which):
        re = [bre[:, :, 0], zero] if which == 0 else [zero, bre[:, :, 1]]
        im = [bim[:, :, 0], zero] if which == 0 else [zero, bim[:, :, 1]]
        blk = jnp.concatenate(re + im, axis=2)
        return jnp.swapaxes(blk, 2, 3)

    compact = jnp.concatenate([rows_of(0), rows_of(1)], axis=2)
    local = jnp.arange(SSM_PAIRS) % pairs_per_half
    place = jax.nn.one_hot(local[:, None] * 2 * hg + jnp.arange(2 * hg)[None, :], half_ch, dtype=F32)
    bw = jnp.einsum('prc,bprs->bpcs', place, compact)
    bw_hi = bw.astype(BF16)
    bw_lo = (bw - bw_hi.astype(F32)).astype(BF16)
    bw_hi2 = jnp.concatenate([bw_hi, bw_hi], axis=2)

    cre = pair_view(c_re)
    cim = pair_view(c_im)
    zc = jnp.zeros_like(cre[:, :, 0])

    def cols_of(which):
        re = [cre[:, :, 0], zc] if which == 0 else [zc, cre[:, :, 1]]
        im = [-cim[:, :, 0], zc] if which == 0 else [zc, -cim[:, :, 1]]
        return jnp.concatenate(re + im, axis=3)

    ccompact = jnp.concatenate([cols_of(0), cols_of(1)], axis=2)
    cplace = jax.nn.one_hot(jnp.arange(SSM_PAIRS)[:, None] * 2 * hg + jnp.arange(2 * hg)[None, :],
                            SSM_BLOCK_CH, dtype=F32)
    cw = jnp.einsum('prc,bprs->bpsc', cplace, ccompact).astype(BF16)

    lre = pair_view(lam_re)
    lim = pair_view(lam_im)
    a_re = jnp.concatenate([lre[:, :, 0], lre[:, :, 1]], axis=-1)
    a_im = jnp.concatenate([lim[:, :, 0], lim[:, :, 1]], axis=-1)
    return bw_hi2, bw_lo, cw, jnp.concatenate([a_re, a_im], axis=-1)


def _pack_state(s_re, s_im):
    B, G, P = s_re.shape
    nblk = G // SSM_BLOCK_GROUPS
    re = s_re.reshape(B, nblk, SSM_PAIRS, 2 * P)
    im = s_im.reshape(B, nblk, SSM_PAIRS, 2 * P)
    return jnp.moveaxis(jnp.concatenate([re, im], axis=-1), 1, 0)


def _unpack_state(st):
    nblk, B, pairs, w = st.shape
    st = jnp.moveaxis(st, 0, 1)
    P = w // 4
    re = st[..., :2 * P].reshape(B, nblk * pairs * 2, P)
    im = st[..., 2 * P:].reshape(B, nblk * pairs * 2, P)
    return re, im


def _s5_kernel(*refs, n_u, nseq, lc, rp, ilp):
    u_refs = refs[:n_u]
    bw2_ref, bwlo_ref, cw_ref, lam_ref, d_ref, h0_ref, y_ref, hout_ref, sre_ref, sim_ref, st_ref = refs[n_u:]
    chunk = pl.program_id(1)
    rows = nseq * lc
    half = LANES

    @pl.when(chunk == 0)
    def _():
        st_ref[...] = h0_ref[0]

    lam = lam_ref[0]
    sw = lam.shape[1] // 2
    a_re = lam[:, :sw]
    a_im = lam[:, sw:]

    u = u_refs[0][...] if n_u == 1 else jnp.concatenate([r[...] for r in u_refs], axis=0)
    for hf in range(2):
        uh = u[:, hf * half:(hf + 1) * half]
        hi = uh.astype(BF16)
        lo = (uh - hi.astype(F32)).astype(BF16)
        lhs = jnp.concatenate([hi, lo], axis=1)
        for q in range(SSM_PAIRS // 2):
            p = hf * (SSM_PAIRS // 2) + q
            bu = (jnp.dot(lhs, bw2_ref[0, p], preferred_element_type=F32)
                  + jnp.dot(hi, bwlo_ref[0, p], preferred_element_type=F32))
            sre_ref[pl.ds(p * rp, rows), :] = bu[:, :sw]
            sim_ref[pl.ds(p * rp, rows), :] = bu[:, sw:]

    def group_body(g, carry):
        seqs = [g * ilp + j for j in range(ilp)]

        def step(t, c):
            out = []
            for j, s in enumerate(seqs):
                x_re, x_im = c[2 * j], c[2 * j + 1]
                at = pl.ds(s * lc + t, SSM_PAIRS, stride=rp)
                n_re = a_re * x_re - a_im * x_im + sre_ref[at, :]
                n_im = a_re * x_im + a_im * x_re + sim_ref[at, :]
                sre_ref[at, :] = n_re
                sim_ref[at, :] = n_im
                out += [n_re, n_im]
            return tuple(out)

        init = []
        for s in seqs:
            x0 = st_ref[s]
            init += [x0[:, :sw], x0[:, sw:]]
        fin = lax.fori_loop(0, lc, step, tuple(init), unroll=4)
        for j, s in enumerate(seqs):
            st_ref[s] = jnp.concatenate([fin[2 * j], fin[2 * j + 1]], axis=1)
        return carry

    lax.fori_loop(0, nseq // ilp, group_body, 0)
    hout_ref[0] = st_ref[...]

    y = d_ref[...] * u
    for p in range(SSM_PAIRS):
        at = pl.ds(p * rp, rows)
        states = jnp.concatenate([sre_ref[at, :], sim_ref[at, :]], axis=1).astype(BF16)
        y = y + jnp.dot(states, cw_ref[0, p], preferred_element_type=F32)
    inner = math.sqrt(2.0 / math.pi) * (y + 0.044715 * (y * y * y))
    y = (0.5 * y * (1.0 + jnp.tanh(inner))).astype(y_ref.dtype)
    rows_u = rows // n_u
    for j in range(n_u):
        y_ref[j] = y[j * rows_u:(j + 1) * rows_u]


def _s5_mixer(z, row0, n_u, seq_per_u, seq_len, weights, d_skip, h0, *, lc):
    bw2, bwlo, cw, lam = weights
    nblk = bw2.shape[0]
    nseq = n_u * seq_per_u
    rows_u = seq_per_u * lc
    rows = n_u * rows_u
    nchunk = seq_len // lc
    assert nchunk == 1 or seq_per_u == 1
    rp = rows + SUBLANES
    r0 = row0 // rows_u
    sw4 = lam.shape[-1]
    ilp = 4 if nseq % 4 == 0 else 1
    kern = functools.partial(_s5_kernel, n_u=n_u, nseq=nseq, lc=lc, rp=rp, ilp=ilp)

    def u_spec(j):
        return pl.BlockSpec((rows_u, SSM_BLOCK_CH), lambda b, c: (r0 + j * nchunk + c, b))

    return pl.pallas_call(
        kern,
        grid=(nblk, nchunk),
        in_specs=[u_spec(j) for j in range(n_u)] + [
            pl.BlockSpec((1,) + bw2.shape[1:], lambda b, c: (b, 0, 0, 0)),
            pl.BlockSpec((1,) + bwlo.shape[1:], lambda b, c: (b, 0, 0, 0)),
            pl.BlockSpec((1,) + cw.shape[1:], lambda b, c: (b, 0, 0, 0)),
            pl.BlockSpec((1, SSM_PAIRS, sw4), lambda b, c: (b, 0, 0)),
            pl.BlockSpec((1, SSM_BLOCK_CH), lambda b, c: (0, b)),
            pl.BlockSpec((1, nseq, SSM_PAIRS, sw4), lambda b, c: (b, 0, 0, 0))],
        out_specs=[pl.BlockSpec((n_u, rows_u, SSM_BLOCK_CH), lambda b, c: (0, c, b)),
                   pl.BlockSpec((1, nseq, SSM_PAIRS, sw4), lambda b, c: (b, 0, 0, 0))],
        out_shape=[jax.ShapeDtypeStruct((n_u, nchunk * rows_u, nblk * SSM_BLOCK_CH), BF16),
                   jax.ShapeDtypeStruct((nblk, nseq, SSM_PAIRS, sw4), F32)],
        scratch_shapes=[pltpu.VMEM((SSM_PAIRS * rp, sw4 // 2), F32),
                        pltpu.VMEM((SSM_PAIRS * rp, sw4 // 2), F32),
                        pltpu.VMEM((nseq, SSM_PAIRS, sw4), F32)],
        compiler_params=_params(("arbitrary", "arbitrary")),
        name="s5_mixer",
    )(*([z] * n_u), bw2, bwlo, cw, lam, d_skip.reshape(1, -1), h0)


def _cumsum_kernel(x_ref, o_ref, *, chunk):
    H, L = x_ref.shape[1], x_ref.shape[2]
    carry = jnp.zeros((H, 1), F32)
    for s in range(0, L, chunk):
        w = min(chunk, L - s)
        r = lax.broadcasted_iota(jnp.int32, (w, w), 0)
        c = lax.broadcasted_iota(jnp.int32, (w, w), 1)
        tri = jnp.where(r <= c, 1.0, 0.0).astype(F32)
        part = jnp.dot(x_ref[0, :, s:s + w], tri, precision=HIGHEST, preferred_element_type=F32) + carry
        o_ref[0, :, s:s + w] = part
        carry = part[:, w - 1:w]


def _cumsum_last(x):
    B, H, L = x.shape
    return pl.pallas_call(
        functools.partial(_cumsum_kernel, chunk=256),
        grid=(B,),
        in_specs=[pl.BlockSpec((1, H, L), lambda b: (b, 0, 0))],
        out_specs=pl.BlockSpec((1, H, L), lambda b: (b, 0, 0)),
        out_shape=jax.ShapeDtypeStruct((B, H, L), F32),
        compiler_params=_params(("arbitrary",)),
        name="logf_cumsum",
    )(x)


def _pick_head(c_ref, h):
    blk = c_ref[...]
    lane = lax.broadcasted_iota(jnp.int32, blk.shape, 1)
    return jnp.sum(jnp.where(lane == h, blk, 0.0), axis=1, keepdims=True)


def _nt_dot(a, b):
    return lax.dot_general(a, b, (((1,), (1,)), ((), ())), preferred_element_type=F32)


def _fox_prompt_kernel(q_ref, k_ref, v_ref, cq_ref, ck_ref, o_ref, *, tq, nq, scale):
    h = pl.program_id(1)
    qi = pl.program_id(2)
    q = (q_ref[...] * scale).astype(BF16)
    cq = _pick_head(cq_ref, h)

    def block(ki, carry, diagonal):
        m_prev, l_prev, acc = carry
        at = pl.ds(pl.multiple_of(ki * tq, tq), tq)
        s = _nt_dot(q, k_ref[at, :].astype(BF16)) + cq - ck_ref[0, pl.ds(h * nq + ki, 1), :]
        if diagonal:
            r = lax.broadcasted_iota(jnp.int32, (tq, tq), 0)
            c = lax.broadcasted_iota(jnp.int32, (tq, tq), 1)
            s = jnp.where(c <= r, s, MASK_VALUE)
        m_new = jnp.maximum(m_prev, jnp.max(s, axis=1, keepdims=True))
        alpha = jnp.exp(m_prev - m_new)
        p = jnp.exp(s - m_new)
        l_new = alpha * l_prev + jnp.sum(p, axis=1, keepdims=True)
        acc = alpha * acc + jnp.dot(p.astype(BF16), v_ref[at, :].astype(BF16), preferred_element_type=F32)
        return m_new, l_new, acc

    init = (jnp.full((tq, 1), MASK_VALUE, F32), jnp.zeros((tq, 1), F32), jnp.zeros(o_ref.shape, F32))
    carry = lax.fori_loop(0, qi, lambda ki, c: block(ki, c, False), init)
    _, l, acc = block(qi, carry, True)
    o_ref[...] = (acc / l).astype(o_ref.dtype)


def _fox_prompt(z, c_col, c_row, B, L, H, col_q, col_k, col_v, scale):
    dh = FOX_HEAD_DIM
    tq = _pick(L, 512)
    nq = L // tq
    return pl.pallas_call(
        functools.partial(_fox_prompt_kernel, tq=tq, nq=nq, scale=scale),
        grid=(B, H, nq),
        in_specs=[pl.BlockSpec((tq, dh), lambda b, h, qi: (b * nq + qi, col_q + h)),
                  pl.BlockSpec((L, dh), lambda b, h, qi: (b, col_k + h)),
                  pl.BlockSpec((L, dh), lambda b, h, qi: (b, col_v + h)),
                  pl.BlockSpec((tq, H), lambda b, h, qi: (b * nq + qi, 0)),
                  pl.BlockSpec((1, H * nq, tq), lambda b, h, qi: (b, 0, 0))],
        out_specs=pl.BlockSpec((tq, dh), lambda b, h, qi: (b * nq + qi, h)),
        out_shape=jax.ShapeDtypeStruct((B * L, H * dh), BF16),
        compiler_params=_params(("arbitrary",) * 3),
        name="fox_prompt",
    )(z, z, z, c_col, c_row.reshape(B, H * nq, tq))


def _fox_sample_kernel(q_ref, kn_ref, vn_ref, kc_ref, vc_ref, cq_ref, ck_ref, o_ref, *, past, n_heads, scale):
    dh = FOX_HEAD_DIM
    n = q_ref.shape[0]
    r = lax.broadcasted_iota(jnp.int32, (n, n), 0)
    c = lax.broadcasted_iota(jnp.int32, (n, n), 1)
    causal = c <= r
    cq_all = cq_ref[...]
    outs = []
    for h in range(n_heads):
        cols = slice(h * dh, (h + 1) * dh)
        q = (q_ref[:, cols] * scale).astype(BF16)
        head_rows = pl.ds(h, past, stride=n_heads)
        c_all = ck_ref[0, h:h + 1, :]
        cq = cq_all[:, h:h + 1]
        s_old = _nt_dot(q, kc_ref[0, head_rows, :].astype(BF16)) + cq - c_all[:, :past]
        s_new = _nt_dot(q, kn_ref[:, cols].astype(BF16)) + cq - c_all[:, past:]
        s_new = jnp.where(causal, s_new, MASK_VALUE)
        m = jnp.maximum(jnp.max(s_old, axis=1, keepdims=True), jnp.max(s_new, axis=1, keepdims=True))
        p_old = jnp.exp(s_old - m)
        p_new = jnp.exp(s_new - m)
        l = jnp.sum(p_old, axis=1, keepdims=True) + jnp.sum(p_new, axis=1, keepdims=True)
        acc = (jnp.dot(p_old.astype(BF16), vc_ref[0, head_rows, :].astype(BF16), preferred_element_type=F32)
               + jnp.dot(p_new.astype(BF16), vn_ref[:, cols].astype(BF16), preferred_element_type=F32))
        outs.append((acc / l).astype(o_ref.dtype))
    o_ref[...] = jnp.concatenate(outs, axis=1)


def _fox_sample(z, row0, cache_k, cache_v, c_col, c_row, B, n, H, col_q, col_k, col_v, scale):
    dh = FOX_HEAD_DIM
    past = cache_k.shape[1] // H
    r0 = row0 // n
    width = H * dh
    assert (col_q * dh) % width == 0 and (col_k * dh) % width == 0 and (col_v * dh) % width == 0
    return pl.pallas_call(
        functools.partial(_fox_sample_kernel, past=past, n_heads=H, scale=scale),
        grid=(B,),
        in_specs=[pl.BlockSpec((n, width), lambda b: (r0 + b, col_q * dh // width)),
                  pl.BlockSpec((n, width), lambda b: (r0 + b, col_k * dh // width)),
                  pl.BlockSpec((n, width), lambda b: (r0 + b, col_v * dh // width)),
                  pl.BlockSpec((1, past * H, dh), lambda b: (b, 0, 0)),
                  pl.BlockSpec((1, past * H, dh), lambda b: (b, 0, 0)),
                  pl.BlockSpec((n, H), lambda b: (b, 0)),
                  pl.BlockSpec((1, H, past + n), lambda b: (b, 0, 0))],
        out_specs=pl.BlockSpec((n, width), lambda b: (b, 0)),
        out_shape=jax.ShapeDtypeStruct((B * n, width), BF16),
        compiler_params=_params(("arbitrary",)),
        name="fox_sample",
    )(z, z, z, cache_k, cache_v, c_col, c_row)


def _gather_rows_kernel(tok_ref, used_ref, h_hbm, o_ref, buf, sem, *, tm):
    i = pl.program_id(0)
    n = used_ref[0]

    def row_copy(blk, slot, r):
        t = tok_ref[blk * tm + r]
        return pltpu.make_async_copy(h_hbm.at[pl.ds(t, 1), :], buf.at[slot, pl.ds(r, 1), :], sem.at[slot])

    def issue(blk, slot):
        def body(r, c):
            row_copy(blk, slot, r).start()
            return c
        lax.fori_loop(0, tm, body, 0, unroll=8)

    def drain(blk, slot):
        def body(r, c):
            row_copy(blk, slot, r).wait()
            return c
        lax.fori_loop(0, tm, body, 0, unroll=8)

    @pl.when(jnp.logical_and(i == 0, n > 0))
    def _():
        issue(0, 0)

    @pl.when(i + 1 < n)
    def _():
        issue(i + 1, (i + 1) % 2)

    @pl.when(i < n)
    def _():
        slot = i % 2
        drain(i, slot)
        words = buf[slot]
        lo = lax.bitcast_convert_type(words << 16, F32)
        hi = lax.bitcast_convert_type(words & jnp.uint32(0xFFFF0000), F32)
        o_ref[...] = jnp.concatenate([lo, hi], axis=1).astype(o_ref.dtype)

    @pl.when(i >= n)
    def _():
        o_ref[...] = jnp.zeros(o_ref.shape, o_ref.dtype)


def _gather_rows(h_packed, slot_tok, n_used, tm):
    h = h_packed
    n_slots = slot_tok.shape[0]
    D = 2 * h.shape[1]
    return pl.pallas_call(
        functools.partial(_gather_rows_kernel, tm=tm),
        grid_spec=pltpu.PrefetchScalarGridSpec(
            num_scalar_prefetch=2,
            grid=(n_slots // tm,),
            in_specs=[pl.BlockSpec(memory_space=pl.ANY)],
            out_specs=pl.BlockSpec((tm, D), lambda i, tok, used: (i, 0)),
            scratch_shapes=[pltpu.VMEM((2, tm, D // 2), h.dtype), pltpu.SemaphoreType.DMA((2,))]),
        out_shape=jax.ShapeDtypeStruct((n_slots, D), BF16),
        compiler_params=_params(("arbitrary",)),
        name="moe_dispatch_gather",
    )(slot_tok, n_used, h)


def _grouped_kernel(blk0_ref, nblk_ref, x_hbm, *refs, n_w, tm, tn):
    w_refs = refs[:n_w]
    o_hbm = refs[n_w]
    w_bf = refs[n_w + 1:2 * n_w + 1]
    xbuf, obuf, tbuf, primed_ref, in_sem, out_sem, tail_sem = refs[2 * n_w + 1:]
    col = pl.program_id(0)
    e = pl.program_id(1)
    n_col = pl.num_programs(0)
    n_exp = pl.num_programs(1)
    nblk = nblk_ref[e]
    blk0 = blk0_ref[e]
    npair = nblk // 2

    def block_copy(first, j, slot):
        return pltpu.make_async_copy(x_hbm.at[pl.ds((first + j) * tm, tm), :], xbuf.at[slot], in_sem.at[slot])

    def x_copy(j, slot):
        return block_copy(blk0, j, slot)

    def prime(first, count):
        block_copy(first, 0, 0).start()

        @pl.when(count > 1)
        def _():
            block_copy(first, 1, 1).start()

    @pl.when(jnp.logical_and(col == 0, e == 0))
    def _():
        primed_ref[0] = 0

    def pair_out(p, slot):
        return pltpu.make_async_copy(obuf.at[slot],
                                     o_hbm.at[pl.ds((blk0 + 2 * p) * tm, 2 * tm), pl.ds(col * tn, tn)],
                                     out_sem.at[slot])

    def tail_out():
        return pltpu.make_async_copy(tbuf, o_hbm.at[pl.ds((blk0 + nblk - 1) * tm, tm), pl.ds(col * tn, tn)],
                                     tail_sem)

    def apply(x):
        if n_w == 2:
            a = jnp.dot(x, w_bf[0][...], preferred_element_type=F32)
            b = jnp.dot(x, w_bf[1][...], preferred_element_type=F32)
            return a * _sigmoid(a) * b
        return jnp.dot(x, w_bf[0][...], preferred_element_type=F32)

    @pl.when(nblk > 0)
    def _():
        @pl.when(primed_ref[0] == 0)
        def _():
            prime(blk0, nblk)

        for w_ref, dst in zip(w_refs, w_bf):
            dst[...] = w_ref[0].astype(BF16)

        def pair_body(p, carry):
            s0 = (p % 2) * 2

            @pl.when(2 * p + 2 < nblk)
            def _():
                x_copy(2 * p + 2, 2 - s0).start()

            @pl.when(2 * p + 3 < nblk)
            def _():
                x_copy(2 * p + 3, 3 - s0).start()

            x_copy(2 * p, s0).wait()
            x_copy(2 * p + 1, s0 + 1).wait()
            y = apply(xbuf[pl.ds(s0, 2)].reshape(2 * tm, xbuf.shape[2]))
            oslot = p % 2

            @pl.when(p >= 2)
            def _():
                pair_out(p - 2, oslot).wait()

            obuf[oslot] = y.astype(obuf.dtype)
            pair_out(p, oslot).start()
            return carry

        lax.fori_loop(0, npair, pair_body, 0)

        @pl.when(nblk % 2 == 1)
        def _():
            slot = (npair % 2) * 2
            x_copy(nblk - 1, slot).wait()
            tbuf[...] = apply(xbuf[slot]).astype(tbuf.dtype)
            tail_out().start()

        @pl.when(npair >= 2)
        def _():
            pair_out(npair - 2, npair % 2).wait()

        @pl.when(npair >= 1)
        def _():
            pair_out(npair - 1, (npair - 1) % 2).wait()

        @pl.when(nblk % 2 == 1)
        def _():
            tail_out().wait()

    has_next = jnp.logical_or(e + 1 < n_exp, col + 1 < n_col)
    e_next = jnp.where(e + 1 < n_exp, e + 1, 0)
    nblk_next = nblk_ref[e_next]
    prime_next = jnp.logical_and(has_next, nblk_next > 0)

    @pl.when(prime_next)
    def _():
        prime(blk0_ref[e_next], nblk_next)

    primed_ref[0] = prime_next.astype(jnp.int32)

    @pl.when(e == n_exp - 1)
    def _():
        first = blk0 + nblk
        total = o_hbm.shape[0] // tm

        def zero_copy(j):
            return pltpu.make_async_copy(tbuf, o_hbm.at[pl.ds(j * tm, tm), pl.ds(col * tn, tn)], tail_sem)

        @pl.when(first < total)
        def _():
            tbuf[...] = jnp.zeros(tbuf.shape, tbuf.dtype)

        def start(j, c):
            zero_copy(j).start()
            return c

        def wait(j, c):
            zero_copy(j).wait()
            return c

        lax.fori_loop(first, total, start, 0)
        lax.fori_loop(first, total, wait, 0)


def _grouped_matmul(x, weights, blk0, nblk, tm, tn_pref, out_dtype, name):
    n_rows, K = x.shape
    E, _, N = weights[0].shape
    n_w = len(weights)
    tn = _pick(N, tn_pref, LANES)
    return pl.pallas_call(
        functools.partial(_grouped_kernel, n_w=n_w, tm=tm, tn=tn),
        grid_spec=pltpu.PrefetchScalarGridSpec(
            num_scalar_prefetch=2,
            grid=(N // tn, E),
            in_specs=[pl.BlockSpec(memory_space=pl.ANY)]
            + [pl.BlockSpec((1, K, tn), lambda c, e, b0, nb: (e, 0, c)) for _ in weights],
            out_specs=pl.BlockSpec(memory_space=pl.ANY),
            scratch_shapes=[pltpu.VMEM((K, tn), BF16) for _ in weights] + [
                pltpu.VMEM((4, tm, K), x.dtype),
                pltpu.VMEM((2, 2 * tm, tn), out_dtype),
                pltpu.VMEM((tm, tn), out_dtype),
                pltpu.SMEM((1,), jnp.int32),
                pltpu.SemaphoreType.DMA((4,)),
                pltpu.SemaphoreType.DMA((2,)),
                pltpu.SemaphoreType.DMA(())]),
        out_shape=jax.ShapeDtypeStruct((n_rows, N), out_dtype),
        compiler_params=_params(("arbitrary", "arbitrary")),
        name=name,
    )(blk0, nblk, x, *weights)


def _combine_kernel(dest_ref, x_ref, sh_ref, gate_ref, g_ref, rows_hbm, o_ref, buf, sem, *, tc, final_norm,
                    row0):
    i = pl.program_id(0)
    n = pl.num_programs(0)

    def row_copy(blk, slot, r, k):
        src = dest_ref[(row0 + blk * tc + r) * TOP_K + k]
        return pltpu.make_async_copy(rows_hbm.at[pl.ds(src, 1), :], buf.at[slot, k, pl.ds(r, 1), :],
                                     sem.at[slot])

    def issue(blk, slot):
        def body(r, c):
            for k in range(TOP_K):
                row_copy(blk, slot, r, k).start()
            return c
        lax.fori_loop(0, tc, body, 0)

    def drain(blk, slot):
        def body(r, c):
            for k in range(TOP_K):
                row_copy(blk, slot, r, k).wait()
            return c
        lax.fori_loop(0, tc, body, 0)

    @pl.when(i == 0)
    def _():
        issue(0, 0)

    @pl.when(i + 1 < n)
    def _():
        issue(i + 1, (i + 1) % 2)

    slot = i % 2
    drain(i, slot)
    y = x_ref[...] + sh_ref[...]
    gates = gate_ref[...]
    for k in range(TOP_K):
        y = y + gates[:, k:k + 1] * buf[slot, k]
    o_ref[...] = _rms(y, g_ref[...]) if final_norm else y


def _combine(x, shared, gates, dest, rows, g_final, final_norm, row0, n_rows, tc=32):
    D = x.shape[1]
    tc = _pick(math.gcd(row0, n_rows) if row0 else n_rows, tc)
    b0 = row0 // tc
    return pl.pallas_call(
        functools.partial(_combine_kernel, tc=tc, final_norm=final_norm, row0=row0),
        grid_spec=pltpu.PrefetchScalarGridSpec(
            num_scalar_prefetch=1,
            grid=(n_rows // tc,),
            in_specs=[pl.BlockSpec((tc, D), lambda i, d: (b0 + i, 0)),
                      pl.BlockSpec((tc, D), lambda i, d: (b0 + i, 0)),
                      pl.BlockSpec((tc, TOP_K), lambda i, d: (b0 + i, 0)),
                      pl.BlockSpec((1, D), lambda i, d: (0, 0)),
                      pl.BlockSpec(memory_space=pl.ANY)],
            out_specs=pl.BlockSpec((tc, D), lambda i, d: (i, 0)),
            scratch_shapes=[pltpu.VMEM((2, TOP_K, tc, D), F32), pltpu.SemaphoreType.DMA((2,))]),
        out_shape=jax.ShapeDtypeStruct((n_rows, D), F32),
        compiler_params=_params(("arbitrary",)),
        name="moe_combine_norm",
    )(dest, x, shared, gates, g_final.reshape(1, D), rows)


def _slot_assign_kernel(start_ref, idx_ref, rank_ref, dest_ref, *, n_experts):
    idx = idx_ref[...]
    base = jnp.zeros(idx.shape, jnp.int32)
    for e in range(n_experts):
        base = jnp.where(idx == e, start_ref[e], base)
    dest_ref[...] = base + rank_ref[...]


def _routing_tables(idx, rank, counts, n_experts, tm):
    T = idx.shape[0]
    n_assign = T * TOP_K
    n_blocks = (n_assign + n_experts * (tm - 1) + tm - 1) // tm
    counts = counts.reshape(n_experts).astype(jnp.int32)
    padded = (counts + tm - 1) // tm * tm
    pad_end = jnp.cumsum(padded)
    pad_start = (pad_end - padded).astype(jnp.int32)
    shape2d = (n_assign // LANES, LANES) if n_assign % LANES == 0 else (T, TOP_K)
    dest = pl.pallas_call(
        functools.partial(_slot_assign_kernel, n_experts=n_experts),
        grid_spec=pltpu.PrefetchScalarGridSpec(
            num_scalar_prefetch=1, grid=(1,),
            in_specs=[pl.BlockSpec(shape2d, lambda i, s: (0, 0)), pl.BlockSpec(shape2d, lambda i, s: (0, 0))],
            out_specs=pl.BlockSpec(shape2d, lambda i, s: (0, 0))),
        out_shape=jax.ShapeDtypeStruct(shape2d, jnp.int32),
        name="moe_slot_assign",
    )(pad_start, idx.reshape(shape2d), rank.reshape(shape2d)).reshape(n_assign)
    slot_tok = jnp.zeros((n_blocks * tm,), jnp.int32).at[dest].set(
        jnp.arange(n_assign, dtype=jnp.int32) // TOP_K)
    return dest, slot_tok, (pad_start // tm).astype(jnp.int32), (padded // tm).astype(jnp.int32)


def kernel(x_prompt, x_sample, cache_k, cache_v, cache_logf, state_ssm_re, state_ssm_im, norm_mix, w_in, b_forget, ssm_a_re, ssm_a_im, ssm_log_dt, ssm_b_re, ssm_b_im, ssm_c_re, ssm_c_im, ssm_d, w_glu, w_fox_out, w_out, norm_ffn, w_router, router_bias, w_exp_gate, w_exp_up, w_exp_down, w_sh_gate, w_sh_up, w_sh_down, norm_final):
    Bp, Lp, D = x_prompt.shape
    Bs, Ls, _ = x_sample.shape
    depth = w_in.shape[0]
    past = cache_k.shape[2]
    H = cache_k.shape[3]
    dh = cache_k.shape[4]
    assert dh == FOX_HEAD_DIM
    d_fox = H * dh
    d_ssm = ssm_d.shape[1]
    n_experts = w_router.shape[2]
    Tp = Bp * Lp
    Ts = Bs * Ls
    T = Tp + Ts
    scale = dh ** -0.5
    n_uqkv = d_ssm + 3 * d_fox
    col_q = d_ssm // dh
    col_k = col_q + H
    col_v = col_k + H
    moe_tm = 256

    x = jnp.concatenate([x_prompt.reshape(Tp, D), x_sample.reshape(Ts, D)], axis=0)
    outs = {name: [] for name in ("kp", "vp", "fp", "rp", "ip", "ks", "vs", "fs", "rs", "is")}
    for l in range(depth):
        wt_in = jnp.swapaxes(w_in[l], 0, 1)
        wt_f = wt_in[n_uqkv:n_uqkv + H]
        wt_gates = wt_in[n_uqkv + H:]

        h, logf = _norm_forget(x, norm_mix[l], wt_f, b_forget[l])
        z = _matmul_wt(h, wt_in, n_uqkv, F32, name="in_proj")
        sig_gates = _matmul_wt(h, wt_gates, wt_gates.shape[0], BF16, sigmoid=True, name="gate_proj")

        lam_re, lam_im, bb_re, bb_im = _s5_discretise(ssm_a_re[l], ssm_a_im[l], ssm_log_dt[l],
                                                      ssm_b_re[l], ssm_b_im[l])
        s5_w = _s5_pack_weights(lam_re, lam_im, bb_re, bb_im, ssm_c_re[l], ssm_c_im[l])
        zero_state = jnp.zeros((Bp,) + state_ssm_re.shape[2:], F32)
        g_p, st_p = _s5_mixer(z, 0, Bp, 1, Lp, s5_w, ssm_d[l], _pack_state(zero_state, zero_state),
                              lc=_pick(Lp, 256))
        g_s, st_s = _s5_mixer(z, Tp, 1, Bs, Ls, s5_w, ssm_d[l],
                              _pack_state(state_ssm_re[l].astype(F32), state_ssm_im[l].astype(F32)), lc=Ls)
        g = jnp.concatenate([g_p.reshape(Tp, d_ssm), g_s.reshape(Ts, d_ssm)], axis=0)

        logf_p = logf[:Tp].reshape(Bp, Lp, H)
        logf_s = logf[Tp:].reshape(Bs, Ls, H)
        crow_p = _cumsum_last(jnp.swapaxes(logf_p, 1, 2))
        ccol_p = jnp.swapaxes(crow_p, 1, 2).reshape(Tp, H)
        attn_p = _fox_prompt(z, ccol_p, crow_p, Bp, Lp, H, col_q, col_k, col_v, scale)
        lf_all = jnp.concatenate([cache_logf[l].astype(F32), logf_s], axis=1)
        crow_s = _cumsum_last(jnp.swapaxes(lf_all, 1, 2))
        ccol_s = jnp.swapaxes(crow_s[:, :, past:], 1, 2).reshape(Ts, H)
        attn_s = _fox_sample(z, Tp, cache_k[l].reshape(Bs, past * H, dh), cache_v[l].reshape(Bs, past * H, dh),
                             ccol_s, crow_s, Bs, Ls, H, col_q, col_k, col_v, scale)
        attn = jnp.concatenate([attn_p, attn_s], axis=0)

        merged = _merge(g, attn, w_glu[l].astype(BF16), w_fox_out[l].astype(BF16), sig_gates)
        x = _matmul_residual(merged, w_out[l].astype(BF16), x)

        norm_out = norm_ffn[l]
        h2, h2p, idx, gates, rank, counts = _norm_router(x, norm_out, w_router[l], router_bias[l])
        dest, slot_tok, e_blk0, e_nblk = _routing_tables(idx, rank, counts, n_experts, moe_tm)
        xs = _gather_rows(h2p, slot_tok, (e_blk0[-1:] + e_nblk[-1:]), moe_tm)
        act = _grouped_matmul(xs, (w_exp_gate[l], w_exp_up[l]), e_blk0, e_nblk, moe_tm, 512, BF16, "expert_up")
        routed = _grouped_matmul(act, (w_exp_down[l],), e_blk0, e_nblk, moe_tm, 2048, F32, "expert_down")
        sh_tm = _pick(T, moe_tm, 16)
        sh_blk0 = jnp.zeros((1,), jnp.int32)
        sh_nblk = jnp.full((1,), T // sh_tm, jnp.int32)
        sh_act = _grouped_matmul(h2, (w_sh_gate[l][None], w_sh_up[l][None]), sh_blk0, sh_nblk, sh_tm, 512, BF16,
                                 "shared_up")
        shared = _grouped_matmul(sh_act, (w_sh_down[l][None],), sh_blk0, sh_nblk, sh_tm, 2048, F32,
                                 "shared_down")
        if l == depth - 1:
            y_p = _combine(x, shared, gates, dest, routed, norm_final, True, 0, Tp)
            y_s = _combine(x, shared, gates, dest, routed, norm_final, True, Tp, Ts)
        else:
            x = _combine(x, shared, gates, dest, routed, norm_final, False, 0, T)

        k_all = z[:, d_ssm + d_fox:d_ssm + 2 * d_fox]
        v_all = z[:, d_ssm + 2 * d_fox:d_ssm + 3 * d_fox]
        re_p, im_p = _unpack_state(st_p)
        re_s, im_s = _unpack_state(st_s)
        outs["kp"].append(k_all[:Tp].reshape(Bp, Lp, H, dh))
        outs["vp"].append(v_all[:Tp].reshape(Bp, Lp, H, dh))
        outs["fp"].append(logf_p)
        outs["rp"].append(re_p)
        outs["ip"].append(im_p)
        outs["ks"].append(k_all[Tp:].reshape(Bs, Ls, H, dh))
        outs["vs"].append(v_all[Tp:].reshape(Bs, Ls, H, dh))
        outs["fs"].append(logf_s)
        outs["rs"].append(re_s)
        outs["is"].append(im_s)

    return (y_p.reshape(Bp, Lp, D), y_s.reshape(Bs, Ls, D),
            jnp.stack(outs["kp"]), jnp.stack(outs["vp"]), jnp.stack(outs["fp"]),
            jnp.stack(outs["rp"]), jnp.stack(outs["ip"]),
            jnp.stack(outs["ks"]), jnp.stack(outs["vs"]), jnp.stack(outs["fs"]),
            jnp.stack(outs["rs"]), jnp.stack(outs["is"]))
```

```python
import functools
import math

import jax
import jax.numpy as jnp
from jax import lax
from jax.experimental import pallas as pl
from jax.experimental.pallas import tpu as pltpu

F32 = jnp.float32
BF16 = jnp.bfloat16
HIGHEST = lax.Precision.HIGHEST

SSM_GROUP = 16
FOX_HEAD_DIM = 128
TOP_K = 8
ROUTED_SCALE = 2.5
NORM_EPS = 1e-6
LAM_RE_MAX = -1e-4

V7X_VMEM_BYTES = 64 * 1024 * 1024
VMEM_LIMIT = V7X_VMEM_BYTES - 6 * 1024 * 1024
LANES = 128
SUBLANES = 8
SSM_BLOCK_GROUPS = 16
SSM_BLOCK_CH = SSM_BLOCK_GROUPS * SSM_GROUP
SSM_PAIRS = SSM_BLOCK_GROUPS // 2
MASK_VALUE = -1e30
PRIME_SLOT = 4


def _pick(n, pref, mult=SUBLANES):
    best = None
    d = mult
    while d <= min(n, pref):
        if n % d == 0:
            best = d
        d += mult
    return best if best is not None else n


def _params(sem, vmem=VMEM_LIMIT):
    return pltpu.CompilerParams(dimension_semantics=sem, vmem_limit_bytes=vmem)


def _sigmoid(x):
    return 1.0 / (1.0 + jnp.exp(-x))


def _rms(x, g):
    r = lax.rsqrt(jnp.mean(x * x, axis=-1, keepdims=True) + NORM_EPS)
    return x * r * g


def _norm_forget_kernel(x_ref, g_ref, w_ref, b_ref, h_ref, logf_ref):
    h = _rms(x_ref[...], g_ref[...])
    h_ref[...] = h.astype(h_ref.dtype)
    z = lax.dot_general(h, w_ref[...], (((1,), (1,)), ((), ())), precision=HIGHEST,
                        preferred_element_type=F32) + b_ref[...]
    logf_ref[...] = jnp.minimum(z, 0.0) - jnp.log1p(jnp.exp(-jnp.abs(z)))


def _norm_forget(x, g, wt_f, b_f):
    T, D = x.shape
    H = wt_f.shape[0]
    tm = _pick(T, 256)
    return pl.pallas_call(
        _norm_forget_kernel,
        grid=(T // tm,),
        in_specs=[pl.BlockSpec((tm, D), lambda i: (i, 0)),
                  pl.BlockSpec((1, D), lambda i: (0, 0)),
                  pl.BlockSpec((H, D), lambda i: (0, 0)),
                  pl.BlockSpec((1, H), lambda i: (0, 0))],
        out_specs=[pl.BlockSpec((tm, D), lambda i: (i, 0)),
                   pl.BlockSpec((tm, H), lambda i: (i, 0))],
        out_shape=[jax.ShapeDtypeStruct((T, D), BF16), jax.ShapeDtypeStruct((T, H), F32)],
        compiler_params=_params(("arbitrary",)),
        name="norm_forget",
    )(x, g.reshape(1, D), wt_f, b_f.reshape(1, H))


def _norm_router_kernel(x_ref, g_ref, w_ref, b_ref, h_ref, hp_ref, idx_ref, gate_ref, rank_ref, cnt_ref):
    @pl.when(pl.program_id(0) == 0)
    def _():
        cnt_ref[...] = jnp.zeros(cnt_ref.shape, F32)

    h = _rms(x_ref[...], g_ref[...])
    hb = h.astype(BF16)
    h_ref[...] = hb
    bits = lax.bitcast_convert_type(hb.astype(F32), jnp.uint32)
    half = bits.shape[1] // 2
    hp_ref[...] = (bits[:, half:] & jnp.uint32(0xFFFF0000)) | (bits[:, :half] >> 16)
    scores = _sigmoid(jnp.dot(h, w_ref[...], precision=HIGHEST, preferred_element_type=F32))
    tm, E = scores.shape
    lane = lax.broadcasted_iota(jnp.int32, (tm, E), 1).astype(F32)
    col = lax.broadcasted_iota(jnp.int32, (tm, TOP_K), 1)
    work = scores + b_ref[...]
    idx = jnp.zeros((tm, TOP_K), F32)
    sel = jnp.zeros((tm, TOP_K), F32)
    chosen = jnp.zeros((tm, E), F32)
    hits = []
    for k in range(TOP_K):
        m = jnp.max(work, axis=1, keepdims=True)
        am = jnp.min(jnp.where(work == m, lane, float(E)), axis=1, keepdims=True)
        hit = lane == am
        hits.append(hit)
        sk = jnp.sum(jnp.where(hit, scores, 0.0), axis=1, keepdims=True)
        idx = jnp.where(col == k, am, idx)
        sel = jnp.where(col == k, sk, sel)
        chosen = jnp.where(hit, 1.0, chosen)
        work = jnp.where(hit, -jnp.inf, work)
    idx_ref[...] = idx.astype(jnp.int32)
    gate_ref[...] = ROUTED_SCALE * sel / jnp.sum(sel, axis=1, keepdims=True)

    r = lax.broadcasted_iota(jnp.int32, (tm, tm), 0)
    c = lax.broadcasted_iota(jnp.int32, (tm, tm), 1)
    lower = jnp.where(c < r, 1.0, 0.0).astype(BF16)
    before = jnp.dot(lower, chosen.astype(BF16), preferred_element_type=F32) + cnt_ref[...]
    rank = jnp.zeros((tm, TOP_K), F32)
    for k in range(TOP_K):
        rank = jnp.where(col == k, jnp.sum(jnp.where(hits[k], before, 0.0), axis=1, keepdims=True), rank)
    rank_ref[...] = rank.astype(jnp.int32)
    cnt_ref[...] = cnt_ref[...] + jnp.sum(chosen, axis=0, keepdims=True)


def _norm_router(x, g, w_router, router_bias):
    T, D = x.shape
    E = w_router.shape[1]
    tm = _pick(T, 256)
    return pl.pallas_call(
        _norm_router_kernel,
        grid=(T // tm,),
        in_specs=[pl.BlockSpec((tm, D), lambda i: (i, 0)),
                  pl.BlockSpec((1, D), lambda i: (0, 0)),
                  pl.BlockSpec((D, E), lambda i: (0, 0)),
                  pl.BlockSpec((1, E), lambda i: (0, 0))],
        out_specs=[pl.BlockSpec((tm, D), lambda i: (i, 0)),
                   pl.BlockSpec((tm, D // 2), lambda i: (i, 0)),
                   pl.BlockSpec((tm, TOP_K), lambda i: (i, 0)),
                   pl.BlockSpec((tm, TOP_K), lambda i: (i, 0)),
                   pl.BlockSpec((tm, TOP_K), lambda i: (i, 0)),
                   pl.BlockSpec((1, E), lambda i: (0, 0))],
        out_shape=[jax.ShapeDtypeStruct((T, D), BF16), jax.ShapeDtypeStruct((T, D // 2), jnp.uint32),
                   jax.ShapeDtypeStruct((T, TOP_K), jnp.int32), jax.ShapeDtypeStruct((T, TOP_K), F32),
                   jax.ShapeDtypeStruct((T, TOP_K), jnp.int32), jax.ShapeDtypeStruct((1, E), F32)],
        compiler_params=_params(("arbitrary",)),
        name="norm_router",
    )(x, g.reshape(1, D), w_router, router_bias.reshape(1, E))


def _mm_kernel(a_ref, w_ref, o_ref, *, sigmoid):
    acc = jnp.dot(a_ref[...], w_ref[...], preferred_element_type=F32)
    if sigmoid:
        acc = _sigmoid(acc)
    o_ref[...] = acc.astype(o_ref.dtype)


def _matmul(a, w, out_dtype, *, sigmoid=False, tm_pref=512, tn_pref=1024, name="matmul"):
    M, K = a.shape
    N = w.shape[1]
    tm = _pick(M, tm_pref, 16)
    tn = _pick(N, tn_pref, LANES)
    return pl.pallas_call(
        functools.partial(_mm_kernel, sigmoid=sigmoid),
        grid=(N // tn, M // tm),
        in_specs=[pl.BlockSpec((tm, K), lambda j, i: (i, 0)),
                  pl.BlockSpec((K, tn), lambda j, i: (0, j))],
        out_specs=pl.BlockSpec((tm, tn), lambda j, i: (i, j)),
        out_shape=jax.ShapeDtypeStruct((M, N), out_dtype),
        compiler_params=_params(("arbitrary", "arbitrary")),
        name=name,
    )(a, w)


def _mm_wt_kernel(a_ref, wt_ref, o_ref, w_s, *, sigmoid):
    @pl.when(pl.program_id(1) == 0)
    def _():
        w_s[...] = wt_ref[...].T.astype(w_s.dtype)

    acc = jnp.dot(a_ref[...], w_s[...], preferred_element_type=F32)
    if sigmoid:
        acc = _sigmoid(acc)
    o_ref[...] = acc.astype(o_ref.dtype)


def _matmul_wt(a, wt, row0, n_rows_w, out_dtype, *, sigmoid=False, tm_pref=1088, tn_pref=512, name="matmul_wt"):
    M, K = a.shape
    tm = _pick(M, tm_pref, 16)
    tn = _pick(n_rows_w, tn_pref, LANES)
    if row0 % tn and row0 % SUBLANES:
        wt, row0 = wt[row0:row0 + n_rows_w], 0
    if row0 % tn == 0:
        w_spec = pl.BlockSpec((tn, K), lambda j, i: (row0 // tn + j, 0))
    else:
        w_spec = pl.BlockSpec((pl.Element(tn), pl.Element(K)),
                              lambda j, i: (pl.multiple_of(row0 + j * tn, SUBLANES), 0))
    return pl.pallas_call(
        functools.partial(_mm_wt_kernel, sigmoid=sigmoid),
        grid=(n_rows_w // tn, M // tm),
        in_specs=[pl.BlockSpec((tm, K), lambda j, i: (i, 0)), w_spec],
        out_specs=pl.BlockSpec((tm, tn), lambda j, i: (i, j)),
        out_shape=jax.ShapeDtypeStruct((M, n_rows_w), out_dtype),
        scratch_shapes=[pltpu.VMEM((K, tn), BF16)],
        compiler_params=_params(("arbitrary", "arbitrary")),
        name=name,
    )(a, wt)


def _mm_residual_kernel(a_ref, w_ref, r_ref, o_ref):
    o_ref[...] = r_ref[...] + jnp.dot(a_ref[...], w_ref[...], preferred_element_type=F32)


def _matmul_residual(a, w, res, *, tm_pref=512, tn_pref=1024):
    M, K = a.shape
    N = w.shape[1]
    tm = _pick(M, tm_pref, 16)
    tn = _pick(N, tn_pref, LANES)
    return pl.pallas_call(
        _mm_residual_kernel,
        grid=(N // tn, M // tm),
        in_specs=[pl.BlockSpec((tm, K), lambda j, i: (i, 0)),
                  pl.BlockSpec((K, tn), lambda j, i: (0, j)),
                  pl.BlockSpec((tm, tn), lambda j, i: (i, j))],
        out_specs=pl.BlockSpec((tm, tn), lambda j, i: (i, j)),
        out_shape=jax.ShapeDtypeStruct((M, N), F32),
        compiler_params=_params(("arbitrary", "arbitrary")),
        name="out_proj_residual",
    )(a, w, res)


def _merge_kernel(g_ref, at_ref, wv_ref, wg_ref, wf_ref, sa_ref, sb_ref, o_ref):
    g = g_ref[...]
    val = jnp.dot(g, wv_ref[...], preferred_element_type=F32)
    gate = jnp.dot(g, wg_ref[...], preferred_element_type=F32)
    y_b = jnp.dot(at_ref[...], wf_ref[...], preferred_element_type=F32)
    y_a = val * _sigmoid(gate)
    o_ref[...] = (sa_ref[...].astype(F32) * y_a + sb_ref[...].astype(F32) * y_b).astype(o_ref.dtype)


def _merge(g, attn, w_glu, w_fox_out, sig_gates, *, tm_pref=512, tn_pref=512):
    T, K = g.shape
    D = w_fox_out.shape[1]
    tm = _pick(T, tm_pref, 16)
    tn = _pick(D, tn_pref, LANES)
    nj = D // tn
    return pl.pallas_call(
        _merge_kernel,
        grid=(nj, T // tm),
        in_specs=[pl.BlockSpec((tm, K), lambda j, i: (i, 0)),
                  pl.BlockSpec((tm, K), lambda j, i: (i, 0)),
                  pl.BlockSpec((K, tn), lambda j, i: (0, j)),
                  pl.BlockSpec((K, tn), lambda j, i: (0, j + nj)),
                  pl.BlockSpec((K, tn), lambda j, i: (0, j)),
                  pl.BlockSpec((tm, tn), lambda j, i: (i, j)),
                  pl.BlockSpec((tm, tn), lambda j, i: (i, j + nj))],
        out_specs=pl.BlockSpec((tm, tn), lambda j, i: (i, j)),
        out_shape=jax.ShapeDtypeStruct((T, D), BF16),
        compiler_params=_params(("arbitrary", "arbitrary")),
        name="merge_branches",
    )(g, attn, w_glu, w_glu, w_fox_out, sig_gates, sig_gates)


def _s5_disc_kernel(are_ref, aim_ref, ldt_ref, bre_ref, bim_ref, lre_ref, lim_ref, bbre_ref, bbim_ref):
    lam_re = jnp.minimum(are_ref[...], LAM_RE_MAX)
    lam_im = aim_ref[...]
    dt = jnp.exp(ldt_ref[...])
    mag = jnp.exp(lam_re * dt)
    lb_re = mag * jnp.cos(lam_im * dt)
    lb_im = mag * jnp.sin(lam_im * dt)
    lre_ref[...] = lb_re
    lim_ref[...] = lb_im
    n_re = lb_re - 1.0
    den = lam_re * lam_re + lam_im * lam_im
    co_re = (n_re * lam_re + lb_im * lam_im) / den
    co_im = (lb_im * lam_re - n_re * lam_im) / den
    G, P = co_re.shape
    PH = bre_ref.shape[1]
    hg = PH // P
    rows = lax.broadcasted_iota(jnp.int32, (P, PH), 0)
    cols = lax.broadcasted_iota(jnp.int32, (P, PH), 1)
    expand = jnp.where(cols // hg == rows, 1.0, 0.0).astype(F32)
    ce_re = jnp.dot(co_re, expand, precision=HIGHEST, preferred_element_type=F32)
    ce_im = jnp.dot(co_im, expand, precision=HIGHEST, preferred_element_type=F32)
    b_re = bre_ref[...]
    b_im = bim_ref[...]
    bbre_ref[...] = ce_re * b_re - ce_im * b_im
    bbim_ref[...] = ce_re * b_im + ce_im * b_re


def _s5_discretise(a_re, a_im, log_dt, b_re, b_im):
    G, P = a_re.shape
    hg = b_re.shape[2]
    outs = pl.pallas_call(
        _s5_disc_kernel,
        out_shape=[jax.ShapeDtypeStruct((G, P), F32), jax.ShapeDtypeStruct((G, P), F32),
                   jax.ShapeDtypeStruct((G, P * hg), F32), jax.ShapeDtypeStruct((G, P * hg), F32)],
        name="s5_discretise",
    )(a_re, a_im, log_dt.reshape(G, 1), b_re.reshape(G, P * hg), b_im.reshape(G, P * hg))
    lam_re, lam_im, bb_re, bb_im = outs
    return lam_re, lam_im, bb_re.reshape(G, P, hg), bb_im.reshape(G, P, hg)


def _s5_pack_weights(lam_re, lam_im, bb_re, bb_im, c_re, c_im):
    G, P, hg = bb_re.shape
    nblk = G // SSM_BLOCK_GROUPS
    half_ch = SSM_BLOCK_CH // 2
    pairs_per_half = SSM_PAIRS // 2

    def pair_view(x):
        return x.reshape((nblk, SSM_PAIRS, 2) + x.shape[1:])

    bre = pair_view(bb_re)
    bim = pair_view(bb_im)
    zero = jnp.zeros_like(bre[:, :, 0])

    def rows_of(which):
        re = [bre[:, :, 0], zero] if which == 0 else [zero, bre[:, :, 1]]
        im = [bim[:, :, 0], zero] if which == 0 else [zero, bim[:, :, 1]]
        blk = jnp.concatenate(re + im, axis=2)
        return jnp.swapaxes(blk, 2, 3)

    compact = jnp.concatenate([rows_of(0), rows_of(1)], axis=2)
    local = jnp.arange(SSM_PAIRS) % pairs_per_half
    place = jax.nn.one_hot(local[:, None] * 2 * hg + jnp.arange(2 * hg)[None, :], half_ch, dtype=F32)
    bw = jnp.einsum('prc,bprs->bpcs', place, compact)
    bw_hi = bw.astype(BF16)
    bw_lo = (bw - bw_hi.astype(F32)).astype(BF16)
    bw_hi2 = jnp.concatenate([bw_hi, bw_hi], axis=2)

    cre = pair_view(c_re)
    cim = pair_view(c_im)
    zc = jnp.zeros_like(cre[:, :, 0])

    def cols_of(which):
        re = [cre[:, :, 0], zc] if which == 0 else [zc, cre[:, :, 1]]
        im = [-cim[:, :, 0], zc] if which == 0 else [zc, -cim[:, :, 1]]
        return jnp.concatenate(re + im, axis=3)

    ccompact = jnp.concatenate([cols_of(0), cols_of(1)], axis=2)
    cplace = jax.nn.one_hot(jnp.arange(SSM_PAIRS)[:, None] * 2 * hg + jnp.arange(2 * hg)[None, :],
                            SSM_BLOCK_CH, dtype=F32)
    cw = jnp.einsum('prc,bprs->bpsc', cplace, ccompact).astype(BF16)

    lre = pair_view(lam_re)
    lim = pair_view(lam_im)
    a_re = jnp.concatenate([lre[:, :, 0], lre[:, :, 1]], axis=-1)
    a_im = jnp.concatenate([lim[:, :, 0], lim[:, :, 1]], axis=-1)
    return bw_hi2, bw_lo, cw, jnp.concatenate([a_re, a_im], axis=-1)


def _pack_state(s_re, s_im):
    B, G, P = s_re.shape
    nblk = G // SSM_BLOCK_GROUPS
    re = s_re.reshape(B, nblk, SSM_PAIRS, 2 * P)
    im = s_im.reshape(B, nblk, SSM_PAIRS, 2 * P)
    return jnp.moveaxis(jnp.concatenate([re, im], axis=-1), 1, 0)


def _unpack_state(st):
    nblk, B, pairs, w = st.shape
    st = jnp.moveaxis(st, 0, 1)
    P = w // 4
    re = st[..., :2 * P].reshape(B, nblk * pairs * 2, P)
    im = st[..., 2 * P:].reshape(B, nblk * pairs * 2, P)
    return re, im


def _s5_kernel(*refs, n_u, nseq, lc, rp, ilp):
    u_refs = refs[:n_u]
    bw2_ref, bwlo_ref, cw_ref, lam_ref, d_ref, h0_ref, y_ref, hout_ref, sre_ref, sim_ref, st_ref = refs[n_u:]
    chunk = pl.program_id(1)
    rows = nseq * lc
    half = LANES

    @pl.when(chunk == 0)
    def _():
        st_ref[...] = h0_ref[0]

    lam = lam_ref[0]
    sw = lam.shape[1] // 2
    a_re = lam[:, :sw]
    a_im = lam[:, sw:]

    u = u_refs[0][...] if n_u == 1 else jnp.concatenate([r[...] for r in u_refs], axis=0)
    for hf in range(2):
        uh = u[:, hf * half:(hf + 1) * half]
        hi = uh.astype(BF16)
        lo = (uh - hi.astype(F32)).astype(BF16)
        lhs = jnp.concatenate([hi, lo], axis=1)
        for q in range(SSM_PAIRS // 2):
            p = hf * (SSM_PAIRS // 2) + q
            bu = (jnp.dot(lhs, bw2_ref[0, p], preferred_element_type=F32)
                  + jnp.dot(hi, bwlo_ref[0, p], preferred_element_type=F32))
            sre_ref[pl.ds(p * rp, rows), :] = bu[:, :sw]
            sim_ref[pl.ds(p * rp, rows), :] = bu[:, sw:]

    def group_body(g, carry):
        seqs = [g * ilp + j for j in range(ilp)]

        def step(t, c):
            out = []
            for j, s in enumerate(seqs):
                x_re, x_im = c[2 * j], c[2 * j + 1]
                at = pl.ds(s * lc + t, SSM_PAIRS, stride=rp)
                n_re = a_re * x_re - a_im * x_im + sre_ref[at, :]
                n_im = a_re * x_im + a_im * x_re + sim_ref[at, :]
                sre_ref[at, :] = n_re
                sim_ref[at, :] = n_im
                out += [n_re, n_im]
            return tuple(out)

        init = []
        for s in seqs:
            x0 = st_ref[s]
            init += [x0[:, :sw], x0[:, sw:]]
        fin = lax.fori_loop(0, lc, step, tuple(init), unroll=4)
        for j, s in enumerate(seqs):
            st_ref[s] = jnp.concatenate([fin[2 * j], fin[2 * j + 1]], axis=1)
        return carry

    lax.fori_loop(0, nseq // ilp, group_body, 0)
    hout_ref[0] = st_ref[...]

    y = d_ref[...] * u
    for p in range(SSM_PAIRS):
        at = pl.ds(p * rp, rows)
        states = jnp.concatenate([sre_ref[at, :], sim_ref[at, :]], axis=1).astype(BF16)
        y = y + jnp.dot(states, cw_ref[0, p], preferred_element_type=F32)
    inner = math.sqrt(2.0 / math.pi) * (y + 0.044715 * (y * y * y))
    y = (0.5 * y * (1.0 + jnp.tanh(inner))).astype(y_ref.dtype)
    rows_u = rows // n_u
    for j in range(n_u):
        y_ref[j] = y[j * rows_u:(j + 1) * rows_u]


def _s5_mixer(z, row0, n_u, seq_per_u, seq_len, weights, d_skip, h0, *, lc):
    bw2, bwlo, cw, lam = weights
    nblk = bw2.shape[0]
    nseq = n_u * seq_per_u
    rows_u = seq_per_u * lc
    rows = n_u * rows_u
    nchunk = seq_len // lc
    assert nchunk == 1 or seq_per_u == 1
    rp = rows + SUBLANES
    r0 = row0 // rows_u
    sw4 = lam.shape[-1]
    ilp = 4 if nseq % 4 == 0 else 1
    kern = functools.partial(_s5_kernel, n_u=n_u, nseq=nseq, lc=lc, rp=rp, ilp=ilp)

    def u_spec(j):
        return pl.BlockSpec((rows_u, SSM_BLOCK_CH), lambda b, c: (r0 + j * nchunk + c, b))

    return pl.pallas_call(
        kern,
        grid=(nblk, nchunk),
        in_specs=[u_spec(j) for j in range(n_u)] + [
            pl.BlockSpec((1,) + bw2.shape[1:], lambda b, c: (b, 0, 0, 0)),
            pl.BlockSpec((1,) + bwlo.shape[1:], lambda b, c: (b, 0, 0, 0)),
            pl.BlockSpec((1,) + cw.shape[1:], lambda b, c: (b, 0, 0, 0)),
            pl.BlockSpec((1, SSM_PAIRS, sw4), lambda b, c: (b, 0, 0)),
            pl.BlockSpec((1, SSM_BLOCK_CH), lambda b, c: (0, b)),
            pl.BlockSpec((1, nseq, SSM_PAIRS, sw4), lambda b, c: (b, 0, 0, 0))],
        out_specs=[pl.BlockSpec((n_u, rows_u, SSM_BLOCK_CH), lambda b, c: (0, c, b)),
                   pl.BlockSpec((1, nseq, SSM_PAIRS, sw4), lambda b, c: (b, 0, 0, 0))],
        out_shape=[jax.ShapeDtypeStruct((n_u, nchunk * rows_u, nblk * SSM_BLOCK_CH), BF16),
                   jax.ShapeDtypeStruct((nblk, nseq, SSM_PAIRS, sw4), F32)],
        scratch_shapes=[pltpu.VMEM((SSM_PAIRS * rp, sw4 // 2), F32),
                        pltpu.VMEM((SSM_PAIRS * rp, sw4 // 2), F32),
                        pltpu.VMEM((nseq, SSM_PAIRS, sw4), F32)],
        compiler_params=_params(("arbitrary", "arbitrary")),
        name="s5_mixer",
    )(*([z] * n_u), bw2, bwlo, cw, lam, d_skip.reshape(1, -1), h0)


def _cumsum_kernel(x_ref, o_ref, *, chunk):
    H, L = x_ref.shape[1], x_ref.shape[2]
    carry = jnp.zeros((H, 1), F32)
    for s in range(0, L, chunk):
        w = min(chunk, L - s)
        r = lax.broadcasted_iota(jnp.int32, (w, w), 0)
        c = lax.broadcasted_iota(jnp.int32, (w, w), 1)
        tri = jnp.where(r <= c, 1.0, 0.0).astype(F32)
        part = jnp.dot(x_ref[0, :, s:s + w], tri, precision=HIGHEST, preferred_element_type=F32) + carry
        o_ref[0, :, s:s + w] = part
        carry = part[:, w - 1:w]


def _cumsum_last(x):
    B, H, L = x.shape
    return pl.pallas_call(
        functools.partial(_cumsum_kernel, chunk=256),
        grid=(B,),
        in_specs=[pl.BlockSpec((1, H, L), lambda b: (b, 0, 0))],
        out_specs=pl.BlockSpec((1, H, L), lambda b: (b, 0, 0)),
        out_shape=jax.ShapeDtypeStruct((B, H, L), F32),
        compiler_params=_params(("arbitrary",)),
        name="logf_cumsum",
    )(x)


def _pick_head(c_ref, h):
    blk = c_ref[...]
    lane = lax.broadcasted_iota(jnp.int32, blk.shape, 1)
    return jnp.sum(jnp.where(lane == h, blk, 0.0), axis=1, keepdims=True)


def _nt_dot(a, b):
    return lax.dot_general(a, b, (((1,), (1,)), ((), ())), preferred_element_type=F32)


def _fox_prompt_kernel(q_ref, k_ref, v_ref, cq_ref, ck_ref, o_ref, *, tq, nq, scale):
    h = pl.program_id(1)
    qi = pl.program_id(2)
    q = (q_ref[...] * scale).astype(BF16)
    cq = _pick_head(cq_ref, h)

    def block(ki, carry, diagonal):
        m_prev, l_prev, acc = carry
        at = pl.ds(pl.multiple_of(ki * tq, tq), tq)
        s = _nt_dot(q, k_ref[at, :].astype(BF16)) + cq - ck_ref[0, pl.ds(h * nq + ki, 1), :]
        if diagonal:
            r = lax.broadcasted_iota(jnp.int32, (tq, tq), 0)
            c = lax.broadcasted_iota(jnp.int32, (tq, tq), 1)
            s = jnp.where(c <= r, s, MASK_VALUE)
        m_new = jnp.maximum(m_prev, jnp.max(s, axis=1, keepdims=True))
        alpha = jnp.exp(m_prev - m_new)
        p = jnp.exp(s - m_new)
        l_new = alpha * l_prev + jnp.sum(p, axis=1, keepdims=True)
        acc = alpha * acc + jnp.dot(p.astype(BF16), v_ref[at, :].astype(BF16), preferred_element_type=F32)
        return m_new, l_new, acc

    init = (jnp.full((tq, 1), MASK_VALUE, F32), jnp.zeros((tq, 1), F32), jnp.zeros(o_ref.shape, F32))
    carry = lax.fori_loop(0, qi, lambda ki, c: block(ki, c, False), init)
    _, l, acc = block(qi, carry, True)
    o_ref[...] = (acc / l).astype(o_ref.dtype)


def _fox_prompt(z, c_col, c_row, B, L, H, col_q, col_k, col_v, scale):
    dh = FOX_HEAD_DIM
    tq = _pick(L, 512)
    nq = L // tq
    return pl.pallas_call(
        functools.partial(_fox_prompt_kernel, tq=tq, nq=nq, scale=scale),
        grid=(B, H, nq),
        in_specs=[pl.BlockSpec((tq, dh), lambda b, h, qi: (b * nq + qi, col_q + h)),
                  pl.BlockSpec((L, dh), lambda b, h, qi: (b, col_k + h)),
                  pl.BlockSpec((L, dh), lambda b, h, qi: (b, col_v + h)),
                  pl.BlockSpec((tq, H), lambda b, h, qi: (b * nq + qi, 0)),
                  pl.BlockSpec((1, H * nq, tq), lambda b, h, qi: (b, 0, 0))],
        out_specs=pl.BlockSpec((tq, dh), lambda b, h, qi: (b * nq + qi, h)),
        out_shape=jax.ShapeDtypeStruct((B * L, H * dh), BF16),
        compiler_params=_params(("arbitrary",) * 3),
        name="fox_prompt",
    )(z, z, z, c_col, c_row.reshape(B, H * nq, tq))


def _fox_sample_kernel(q_ref, kn_ref, vn_ref, kc_ref, vc_ref, cq_ref, ck_ref, o_ref, *, past, n_heads, scale):
    dh = FOX_HEAD_DIM
    n = q_ref.shape[0]
    r = lax.broadcasted_iota(jnp.int32, (n, n), 0)
    c = lax.broadcasted_iota(jnp.int32, (n, n), 1)
    causal = c <= r
    cq_all = cq_ref[...]
    outs = []
    for h in range(n_heads):
        cols = slice(h * dh, (h + 1) * dh)
        q = (q_ref[:, cols] * scale).astype(BF16)
        head_rows = pl.ds(h, past, stride=n_heads)
        c_all = ck_ref[0, h:h + 1, :]
        cq = cq_all[:, h:h + 1]
        s_old = _nt_dot(q, kc_ref[0, head_rows, :].astype(BF16)) + cq - c_all[:, :past]
        s_new = _nt_dot(q, kn_ref[:, cols].astype(BF16)) + cq - c_all[:, past:]
        s_new = jnp.where(causal, s_new, MASK_VALUE)
        m = jnp.maximum(jnp.max(s_old, axis=1, keepdims=True), jnp.max(s_new, axis=1, keepdims=True))
        p_old = jnp.exp(s_old - m)
        p_new = jnp.exp(s_new - m)
        l = jnp.sum(p_old, axis=1, keepdims=True) + jnp.sum(p_new, axis=1, keepdims=True)
        acc = (jnp.dot(p_old.astype(BF16), vc_ref[0, head_rows, :].astype(BF16), preferred_element_type=F32)
               + jnp.dot(p_new.astype(BF16), vn_ref[:, cols].astype(BF16), preferred_element_type=F32))
        outs.append((acc / l).astype(o_ref.dtype))
    o_ref[...] = jnp.concatenate(outs, axis=1)


def _fox_sample(z, row0, cache_k, cache_v, c_col, c_row, B, n, H, col_q, col_k, col_v, scale):
    dh = FOX_HEAD_DIM
    past = cache_k.shape[1] // H
    r0 = row0 // n
    width = H * dh
    assert (col_q * dh) % width == 0 and (col_k * dh) % width == 0 and (col_v * dh) % width == 0
    return pl.pallas_call(
        functools.partial(_fox_sample_kernel, past=past, n_heads=H, scale=scale),
        grid=(B,),
        in_specs=[pl.BlockSpec((n, width), lambda b: (r0 + b, col_q * dh // width)),
                  pl.BlockSpec((n, width), lambda b: (r0 + b, col_k * dh // width)),
                  pl.BlockSpec((n, width), lambda b: (r0 + b, col_v * dh // width)),
                  pl.BlockSpec((1, past * H, dh), lambda b: (b, 0, 0)),
                  pl.BlockSpec((1, past * H, dh), lambda b: (b, 0, 0)),
                  pl.BlockSpec((n, H), lambda b: (b, 0)),
                  pl.BlockSpec((1, H, past + n), lambda b: (b, 0, 0))],
        out_specs=pl.BlockSpec((n, width), lambda b: (b, 0)),
        out_shape=jax.ShapeDtypeStruct((B * n, width), BF16),
        compiler_params=_params(("arbitrary",)),
        name="fox_sample",
    )(z, z, z, cache_k, cache_v, c_col, c_row)


def _gather_rows_kernel(tok_ref, used_ref, h_hbm, o_ref, buf, sem, *, tm):
    i = pl.program_id(0)
    n = used_ref[0]

    def row_copy(blk, slot, r):
        t = tok_ref[blk * tm + r]
        return pltpu.make_async_copy(h_hbm.at[pl.ds(t, 1), :], buf.at[slot, pl.ds(r, 1), :], sem.at[slot])

    def issue(blk, slot):
        def body(r2, c):
            row_copy(blk, slot, 2 * r2).start(priority=0)
            row_copy(blk, slot, 2 * r2 + 1).start(priority=1)
            return c
        lax.fori_loop(0, tm // 2, body, 0, unroll=4)

    def drain(blk, slot):
        def body(r, c):
            row_copy(blk, slot, r).wait()
            return c
        lax.fori_loop(0, tm, body, 0, unroll=8)

    @pl.when(jnp.logical_and(i == 0, n > 0))
    def _():
        issue(0, 0)

    @pl.when(i + 1 < n)
    def _():
        issue(i + 1, (i + 1) % 2)

    @pl.when(i < n)
    def _():
        slot = i % 2
        drain(i, slot)
        words = buf[slot]
        lo = lax.bitcast_convert_type(words << 16, F32)
        hi = lax.bitcast_convert_type(words & jnp.uint32(0xFFFF0000), F32)
        o_ref[...] = jnp.concatenate([lo, hi], axis=1).astype(o_ref.dtype)

    @pl.when(i >= n)
    def _():
        o_ref[...] = jnp.zeros(o_ref.shape, o_ref.dtype)


def _gather_rows(h_packed, slot_tok, n_used, tm):
    h = h_packed
    n_slots = slot_tok.shape[0]
    D = 2 * h.shape[1]
    return pl.pallas_call(
        functools.partial(_gather_rows_kernel, tm=tm),
        grid_spec=pltpu.PrefetchScalarGridSpec(
            num_scalar_prefetch=2,
            grid=(n_slots // tm,),
            in_specs=[pl.BlockSpec(memory_space=pl.ANY)],
            out_specs=pl.BlockSpec((tm, D), lambda i, tok, used: (i, 0)),
            scratch_shapes=[pltpu.VMEM((2, tm, D // 2), h.dtype), pltpu.SemaphoreType.DMA((2,))]),
        out_shape=jax.ShapeDtypeStruct((n_slots, D), BF16),
        compiler_params=_params(("arbitrary",)),
        name="moe_dispatch_gather",
    )(slot_tok, n_used, h)


def _grouped_kernel(blk0_ref, nblk_ref, x_hbm, *refs, n_w, tm, tn):
    w_refs = refs[:n_w]
    o_hbm = refs[n_w]
    w_bf = refs[n_w + 1:2 * n_w + 1]
    xbuf, obuf, tbuf, primed_ref, in_sem, out_sem, tail_sem = refs[2 * n_w + 1:]
    col = pl.program_id(0)
    e = pl.program_id(1)
    n_col = pl.num_programs(0)
    n_exp = pl.num_programs(1)
    nblk = nblk_ref[e]
    blk0 = blk0_ref[e]
    npair = nblk // 2

    def block_copy(first, j, slot):
        return pltpu.make_async_copy(x_hbm.at[pl.ds((first + j) * tm, tm), :], xbuf.at[slot], in_sem.at[slot])

    def x_copy(j, slot):
        return block_copy(blk0, j, slot)

    def pair_base(p):
        return jnp.where(p == 0, PRIME_SLOT, (p % 2) * 2)

    def prime(first, count):
        block_copy(first, 0, PRIME_SLOT).start()

        @pl.when(count > 1)
        def _():
            block_copy(first, 1, PRIME_SLOT + 1).start()

    @pl.when(jnp.logical_and(col == 0, e == 0))
    def _():
        primed_ref[0] = 0

    has_next = jnp.logical_or(e + 1 < n_exp, col + 1 < n_col)
    e_next = jnp.where(e + 1 < n_exp, e + 1, 0)
    nblk_next = nblk_ref[e_next]
    prime_next = jnp.logical_and(has_next, nblk_next > 0)

    def request_next():
        @pl.when(prime_next)
        def _():
            prime(blk0_ref[e_next], nblk_next)

    def pair_out(p, slot):
        return pltpu.make_async_copy(obuf.at[slot],
                                     o_hbm.at[pl.ds((blk0 + 2 * p) * tm, 2 * tm), pl.ds(col * tn, tn)],
                                     out_sem.at[slot])

    def tail_out():
        return pltpu.make_async_copy(tbuf, o_hbm.at[pl.ds((blk0 + nblk - 1) * tm, tm), pl.ds(col * tn, tn)],
                                     tail_sem)

    def apply(x):
        if n_w == 2:
            a = jnp.dot(x, w_bf[0][...], preferred_element_type=F32)
            b = jnp.dot(x, w_bf[1][...], preferred_element_type=F32)
            return a * _sigmoid(a) * b
        return jnp.dot(x, w_bf[0][...], preferred_element_type=F32)

    @pl.when(nblk > 0)
    def _():
        @pl.when(primed_ref[0] == 0)
        def _():
            prime(blk0, nblk)

        for w_ref, dst in zip(w_refs, w_bf):
            dst[...] = w_ref[0].astype(BF16)

        def pair_body(p, carry):
            s0 = pair_base(p)
            s_next = ((p + 1) % 2) * 2

            @pl.when(2 * p + 2 < nblk)
            def _():
                x_copy(2 * p + 2, s_next).start()

            @pl.when(2 * p + 3 < nblk)
            def _():
                x_copy(2 * p + 3, s_next + 1).start()

            x_copy(2 * p, s0).wait()
            x_copy(2 * p + 1, s0 + 1).wait()
            y = apply(xbuf[pl.ds(s0, 2)].reshape(2 * tm, xbuf.shape[2]))
            oslot = p % 2

            @pl.when(p >= 2)
            def _():
                pair_out(p - 2, oslot).wait()

            obuf[oslot] = y.astype(obuf.dtype)
            pair_out(p, oslot).start()

            @pl.when(p == 0)
            def _():
                request_next()

            return carry

        lax.fori_loop(0, npair, pair_body, 0)

        @pl.when(nblk % 2 == 1)
        def _():
            slot = pair_base(npair)
            x_copy(nblk - 1, slot).wait()
            tbuf[...] = apply(xbuf[slot]).astype(tbuf.dtype)
            tail_out().start()

            @pl.when(npair == 0)
            def _():
                request_next()

        @pl.when(npair >= 2)
        def _():
            pair_out(npair - 2, npair % 2).wait()

        @pl.when(npair >= 1)
        def _():
            pair_out(npair - 1, (npair - 1) % 2).wait()

        @pl.when(nblk % 2 == 1)
        def _():
            tail_out().wait()

    @pl.when(nblk == 0)
    def _():
        request_next()

    primed_ref[0] = prime_next.astype(jnp.int32)

    @pl.when(e == n_exp - 1)
    def _():
        first = blk0 + nblk
        total = o_hbm.shape[0] // tm

        def zero_copy(j):
            return pltpu.make_async_copy(tbuf, o_hbm.at[pl.ds(j * tm, tm), pl.ds(col * tn, tn)], tail_sem)

        @pl.when(first < total)
        def _():
            tbuf[...] = jnp.zeros(tbuf.shape, tbuf.dtype)

        def start(j, c):
            zero_copy(j).start()
            return c

        def wait(j, c):
            zero_copy(j).wait()
            return c

        lax.fori_loop(first, total, start, 0)
        lax.fori_loop(first, total, wait, 0)


def _grouped_matmul(x, weights, blk0, nblk, tm, tn_pref, out_dtype, name):
    n_rows, K = x.shape
    E, _, N = weights[0].shape
    n_w = len(weights)
    tn = _pick(N, tn_pref, LANES)
    return pl.pallas_call(
        functools.partial(_grouped_kernel, n_w=n_w, tm=tm, tn=tn),
        grid_spec=pltpu.PrefetchScalarGridSpec(
            num_scalar_prefetch=2,
            grid=(N // tn, E),
            in_specs=[pl.BlockSpec(memory_space=pl.ANY)]
            + [pl.BlockSpec((1, K, tn), lambda c, e, b0, nb: (e, 0, c)) for _ in weights],
            out_specs=pl.BlockSpec(memory_space=pl.ANY),
            scratch_shapes=[pltpu.VMEM((K, tn), BF16) for _ in weights] + [
                pltpu.VMEM((PRIME_SLOT + 2, tm, K), x.dtype),
                pltpu.VMEM((2, 2 * tm, tn), out_dtype),
                pltpu.VMEM((tm, tn), out_dtype),
                pltpu.SMEM((1,), jnp.int32),
                pltpu.SemaphoreType.DMA((PRIME_SLOT + 2,)),
                pltpu.SemaphoreType.DMA((2,)),
                pltpu.SemaphoreType.DMA(())]),
        out_shape=jax.ShapeDtypeStruct((n_rows, N), out_dtype),
        compiler_params=_params(("arbitrary", "arbitrary")),
        name=name,
    )(blk0, nblk, x, *weights)


def _combine_kernel(dest_ref, x_ref, sh_ref, gate_ref, g_ref, rows_hbm, o_ref, buf, sem, *, tc, final_norm,
                    row0):
    i = pl.program_id(0)
    n = pl.num_programs(0)

    def row_copy(blk, slot, r, k):
        src = dest_ref[(row0 + blk * tc + r) * TOP_K + k]
        return pltpu.make_async_copy(rows_hbm.at[pl.ds(src, 1), :], buf.at[slot, k, pl.ds(r, 1), :],
                                     sem.at[slot])

    def issue(blk, slot):
        def body(r, c):
            for k in range(TOP_K):
                row_copy(blk, slot, r, k).start(priority=k % 2)
            return c
        lax.fori_loop(0, tc, body, 0)

    def drain(blk, slot):
        def body(r, c):
            for k in range(TOP_K):
                row_copy(blk, slot, r, k).wait()
            return c
        lax.fori_loop(0, tc, body, 0)

    @pl.when(i == 0)
    def _():
        issue(0, 0)

    @pl.when(i + 1 < n)
    def _():
        issue(i + 1, (i + 1) % 2)

    slot = i % 2
    drain(i, slot)
    y = x_ref[...] + sh_ref[...]
    gates = gate_ref[...]
    for k in range(TOP_K):
        y = y + gates[:, k:k + 1] * buf[slot, k]
    o_ref[...] = _rms(y, g_ref[...]) if final_norm else y


def _combine(x, shared, gates, dest, rows, g_final, final_norm, row0, n_rows, tc=32):
    D = x.shape[1]
    tc = _pick(math.gcd(row0, n_rows) if row0 else n_rows, tc)
    b0 = row0 // tc
    return pl.pallas_call(
        functools.partial(_combine_kernel, tc=tc, final_norm=final_norm, row0=row0),
        grid_spec=pltpu.PrefetchScalarGridSpec(
            num_scalar_prefetch=1,
            grid=(n_rows // tc,),
            in_specs=[pl.BlockSpec((tc, D), lambda i, d: (b0 + i, 0)),
                      pl.BlockSpec((tc, D), lambda i, d: (b0 + i, 0)),
                      pl.BlockSpec((tc, TOP_K), lambda i, d: (b0 + i, 0)),
                      pl.BlockSpec((1, D), lambda i, d: (0, 0)),
                      pl.BlockSpec(memory_space=pl.ANY)],
            out_specs=pl.BlockSpec((tc, D), lambda i, d: (i, 0)),
            scratch_shapes=[pltpu.VMEM((2, TOP_K, tc, D), F32), pltpu.SemaphoreType.DMA((2,))]),
        out_shape=jax.ShapeDtypeStruct((n_rows, D), F32),
        compiler_params=_params(("arbitrary",)),
        name="moe_combine_norm",
    )(dest, x, shared, gates, g_final.reshape(1, D), rows)


def _slot_assign_kernel(start_ref, idx_ref, rank_ref, dest_ref, *, n_experts):
    idx = idx_ref[...]
    base = jnp.zeros(idx.shape, jnp.int32)
    for e in range(n_experts):
        base = jnp.where(idx == e, start_ref[e], base)
    dest_ref[...] = base + rank_ref[...]


def _routing_tables(idx, rank, counts, n_experts, tm):
    T = idx.shape[0]
    n_assign = T * TOP_K
    n_blocks = (n_assign + n_experts * (tm - 1) + tm - 1) // tm
    counts = counts.reshape(n_experts).astype(jnp.int32)
    padded = (counts + tm - 1) // tm * tm
    pad_end = jnp.cumsum(padded)
    pad_start = (pad_end - padded).astype(jnp.int32)
    shape2d = (n_assign // LANES, LANES) if n_assign % LANES == 0 else (T, TOP_K)
    dest = pl.pallas_call(
        functools.partial(_slot_assign_kernel, n_experts=n_experts),
        grid_spec=pltpu.PrefetchScalarGridSpec(
            num_scalar_prefetch=1, grid=(1,),
            in_specs=[pl.BlockSpec(shape2d, lambda i, s: (0, 0)), pl.BlockSpec(shape2d, lambda i, s: (0, 0))],
            out_specs=pl.BlockSpec(shape2d, lambda i, s: (0, 0))),
        out_shape=jax.ShapeDtypeStruct(shape2d, jnp.int32),
        name="moe_slot_assign",
    )(pad_start, idx.reshape(shape2d), rank.reshape(shape2d)).reshape(n_assign)
    slot_tok = jnp.zeros((n_blocks * tm,), jnp.int32).at[dest].set(
        jnp.arange(n_assign, dtype=jnp.int32) // TOP_K)
    return dest, slot_tok, (pad_start // tm).astype(jnp.int32), (padded // tm).astype(jnp.int32)


def kernel(x_prompt, x_sample, cache_k, cache_v, cache_logf, state_ssm_re, state_ssm_im, norm_mix, w_in, b_forget, ssm_a_re, ssm_a_im, ssm_log_dt, ssm_b_re, ssm_b_im, ssm_c_re, ssm_c_im, ssm_d, w_glu, w_fox_out, w_out, norm_ffn, w_router, router_bias, w_exp_gate, w_exp_up, w_exp_down, w_sh_gate, w_sh_up, w_sh_down, norm_final):
    Bp, Lp, D = x_prompt.shape
    Bs, Ls, _ = x_sample.shape
    depth = w_in.shape[0]
    past = cache_k.shape[2]
    H = cache_k.shape[3]
    dh = cache_k.shape[4]
    assert dh == FOX_HEAD_DIM
    d_fox = H * dh
    d_ssm = ssm_d.shape[1]
    n_experts = w_router.shape[2]
    Tp = Bp * Lp
    Ts = Bs * Ls
    T = Tp + Ts
    scale = dh ** -0.5
    n_uqkv = d_ssm + 3 * d_fox
    col_q = d_ssm // dh
    col_k = col_q + H
    col_v = col_k + H
    moe_tm = 256

    x = jnp.concatenate([x_prompt.reshape(Tp, D), x_sample.reshape(Ts, D)], axis=0)
    outs = {name: [] for name in ("kp", "vp", "fp", "rp", "ip", "ks", "vs", "fs", "rs", "is")}
    for l in range(depth):
        wt_in = jnp.swapaxes(w_in[l], 0, 1)
        wt_f = wt_in[n_uqkv:n_uqkv + H]

        h, logf = _norm_forget(x, norm_mix[l], wt_f, b_forget[l])
        z = _matmul_wt(h, wt_in, 0, n_uqkv, F32, name="in_proj")
        sig_gates = _matmul_wt(h, wt_in, n_uqkv + H, 2 * D, BF16, sigmoid=True, name="gate_proj")

        lam_re, lam_im, bb_re, bb_im = _s5_discretise(ssm_a_re[l], ssm_a_im[l], ssm_log_dt[l],
                                                      ssm_b_re[l], ssm_b_im[l])
        s5_w = _s5_pack_weights(lam_re, lam_im, bb_re, bb_im, ssm_c_re[l], ssm_c_im[l])
        zero_state = jnp.zeros((Bp,) + state_ssm_re.shape[2:], F32)
        g_p, st_p = _s5_mixer(z, 0, Bp, 1, Lp, s5_w, ssm_d[l], _pack_state(zero_state, zero_state),
                              lc=_pick(Lp, 256))
        g_s, st_s = _s5_mixer(z, Tp, 1, Bs, Ls, s5_w, ssm_d[l],
                              _pack_state(state_ssm_re[l].astype(F32), state_ssm_im[l].astype(F32)), lc=Ls)
        g = jnp.concatenate([g_p.reshape(Tp, d_ssm), g_s.reshape(Ts, d_ssm)], axis=0)

        logf_p = logf[:Tp].reshape(Bp, Lp, H)
        logf_s = logf[Tp:].reshape(Bs, Ls, H)
        crow_p = _cumsum_last(jnp.swapaxes(logf_p, 1, 2))
        ccol_p = jnp.swapaxes(crow_p, 1, 2).reshape(Tp, H)
        attn_p = _fox_prompt(z, ccol_p, crow_p, Bp, Lp, H, col_q, col_k, col_v, scale)
        lf_all = jnp.concatenate([cache_logf[l].astype(F32), logf_s], axis=1)
        crow_s = _cumsum_last(jnp.swapaxes(lf_all, 1, 2))
        ccol_s = jnp.swapaxes(crow_s[:, :, past:], 1, 2).reshape(Ts, H)
        attn_s = _fox_sample(z, Tp, cache_k[l].reshape(Bs, past * H, dh), cache_v[l].reshape(Bs, past * H, dh),
                             ccol_s, crow_s, Bs, Ls, H, col_q, col_k, col_v, scale)
        attn = jnp.concatenate([attn_p, attn_s], axis=0)

        merged = _merge(g, attn, w_glu[l].astype(BF16), w_fox_out[l].astype(BF16), sig_gates)
        x = _matmul_residual(merged, w_out[l].astype(BF16), x)

        norm_out = norm_ffn[l]
        h2, h2p, idx, gates, rank, counts = _norm_router(x, norm_out, w_router[l], router_bias[l])
        dest, slot_tok, e_blk0, e_nblk = _routing_tables(idx, rank, counts, n_experts, moe_tm)
        xs = _gather_rows(h2p, slot_tok, (e_blk0[-1:] + e_nblk[-1:]), moe_tm)
        act = _grouped_matmul(xs, (w_exp_gate[l], w_exp_up[l]), e_blk0, e_nblk, moe_tm, 512, BF16, "expert_up")
        routed = _grouped_matmul(act, (w_exp_down[l],), e_blk0, e_nblk, moe_tm, 2048, F32, "expert_down")
        sh_tm = _pick(T, moe_tm, 16)
        sh_blk0 = jnp.zeros((1,), jnp.int32)
        sh_nblk = jnp.full((1,), T // sh_tm, jnp.int32)
        sh_act = _grouped_matmul(h2, (w_sh_gate[l][None], w_sh_up[l][None]), sh_blk0, sh_nblk, sh_tm, 512, BF16,
                                 "shared_up")
        shared = _grouped_matmul(sh_act, (w_sh_down[l][None],), sh_blk0, sh_nblk, sh_tm, 2048, F32,
                                 "shared_down")
        if l == depth - 1:
            y_p = _combine(x, shared, gates, dest, routed, norm_final, True, 0, Tp)
            y_s = _combine(x, shared, gates, dest, routed, norm_final, True, Tp, Ts)
        else:
            x = _combine(x, shared, gates, dest, routed, norm_final, False, 0, T)

        k_all = z[:, d_ssm + d_fox:d_ssm + 2 * d_fox]
        v_all = z[:, d_ssm + 2 * d_fox:d_ssm + 3 * d_fox]
        re_p, im_p = _unpack_state(st_p)
        re_s, im_s = _unpack_state(st_s)
        outs["kp"].append(k_all[:Tp].reshape(Bp, Lp, H, dh))
        outs["vp"].append(v_all[:Tp].reshape(Bp, Lp, H, dh))
        outs["fp"].append(logf_p)
        outs["rp"].append(re_p)
        outs["ip"].append(im_p)
        outs["ks"].append(k_all[Tp:].reshape(Bs, Ls, H, dh))
        outs["vs"].append(v_all[Tp:].reshape(Bs, Ls, H, dh))
        outs["fs"].append(logf_s)
        outs["rs"].append(re_s)
        outs["is"].append(im_s)

    return (y_p.reshape(Bp, Lp, D), y_s.reshape(Bs, Ls, D),
            jnp.stack(outs["kp"]), jnp.stack(outs["vp"]), jnp.stack(outs["fp"]),
            jnp.stack(outs["rp"]), jnp.stack(outs["ip"]),
            jnp.stack(outs["ks"]), jnp.stack(outs["vs"]), jnp.stack(outs["fs"]),
            jnp.stack(outs["rs"]), jnp.stack(outs["is"]))
```

```python
import functools
import math

import jax
import jax.numpy as jnp
from jax import lax
from jax.experimental import pallas as pl
from jax.experimental.pallas import tpu as pltpu

F32 = jnp.float32
BF16 = jnp.bfloat16
HIGHEST = lax.Precision.HIGHEST

SSM_GROUP = 16
FOX_HEAD_DIM = 128
TOP_K = 8
ROUTED_SCALE = 2.5
NORM_EPS = 1e-6
LAM_RE_MAX = -1e-4

V7X_VMEM_BYTES = 64 * 1024 * 1024
VMEM_LIMIT = V7X_VMEM_BYTES - 6 * 1024 * 1024
LANES = 128
SUBLANES = 8
SSM_BLOCK_GROUPS = 16
SSM_BLOCK_CH = SSM_BLOCK_GROUPS * SSM_GROUP
SSM_PAIRS = SSM_BLOCK_GROUPS // 2
MASK_VALUE = -1e30
PRIME_SLOT = 4


def _pick(n, pref, mult=SUBLANES):
    best = None
    d = mult
    while d <= min(n, pref):
        if n % d == 0:
            best = d
        d += mult
    return best if best is not None else n


def _params(sem, vmem=VMEM_LIMIT):
    return pltpu.CompilerParams(dimension_semantics=sem, vmem_limit_bytes=vmem)


def _sigmoid(x):
    return 1.0 / (1.0 + jnp.exp(-x))


def _sigmoid_tanh(x):
    return 0.5 * jnp.tanh(0.5 * x) + 0.5


def _rms(x, g):
    r = lax.rsqrt(jnp.mean(x * x, axis=-1, keepdims=True) + NORM_EPS)
    return x * r * g


def _norm_forget_kernel(x_ref, g_ref, w_ref, b_ref, h_ref, logf_ref):
    h = _rms(x_ref[...], g_ref[...])
    h_ref[...] = h.astype(h_ref.dtype)
    z = lax.dot_general(h, w_ref[...], (((1,), (1,)), ((), ())), precision=HIGHEST,
                        preferred_element_type=F32) + b_ref[...]
    logf_ref[...] = jnp.minimum(z, 0.0) - jnp.log1p(jnp.exp(-jnp.abs(z)))


def _norm_forget(x, g, wt_f, b_f):
    T, D = x.shape
    H = wt_f.shape[0]
    tm = _pick(T, 256)
    return pl.pallas_call(
        _norm_forget_kernel,
        grid=(T // tm,),
        in_specs=[pl.BlockSpec((tm, D), lambda i: (i, 0)),
                  pl.BlockSpec((1, D), lambda i: (0, 0)),
                  pl.BlockSpec((H, D), lambda i: (0, 0)),
                  pl.BlockSpec((1, H), lambda i: (0, 0))],
        out_specs=[pl.BlockSpec((tm, D), lambda i: (i, 0)),
                   pl.BlockSpec((tm, H), lambda i: (i, 0))],
        out_shape=[jax.ShapeDtypeStruct((T, D), BF16), jax.ShapeDtypeStruct((T, H), F32)],
        compiler_params=_params(("arbitrary",)),
        name="norm_forget",
    )(x, g.reshape(1, D), wt_f, b_f.reshape(1, H))


def _norm_router_kernel(x_ref, g_ref, w_ref, b_ref, h_ref, hp_ref, idx_ref, gate_ref, rank_ref, cnt_ref):
    @pl.when(pl.program_id(0) == 0)
    def _():
        cnt_ref[...] = jnp.zeros(cnt_ref.shape, F32)

    h = _rms(x_ref[...], g_ref[...])
    hb = h.astype(BF16)
    h_ref[...] = hb
    bits = lax.bitcast_convert_type(hb.astype(F32), jnp.uint32)
    half = bits.shape[1] // 2
    hp_ref[...] = (bits[:, half:] & jnp.uint32(0xFFFF0000)) | (bits[:, :half] >> 16)
    scores = _sigmoid(jnp.dot(h, w_ref[...], precision=HIGHEST, preferred_element_type=F32))
    tm, E = scores.shape
    lane = lax.broadcasted_iota(jnp.int32, (tm, E), 1).astype(F32)
    col = lax.broadcasted_iota(jnp.int32, (tm, TOP_K), 1)
    work = scores + b_ref[...]
    idx = jnp.zeros((tm, TOP_K), F32)
    sel = jnp.zeros((tm, TOP_K), F32)
    chosen = jnp.zeros((tm, E), F32)
    hits = []
    for k in range(TOP_K):
        m = jnp.max(work, axis=1, keepdims=True)
        am = jnp.min(jnp.where(work == m, lane, float(E)), axis=1, keepdims=True)
        hit = lane == am
        hits.append(hit)
        sk = jnp.sum(jnp.where(hit, scores, 0.0), axis=1, keepdims=True)
        idx = jnp.where(col == k, am, idx)
        sel = jnp.where(col == k, sk, sel)
        chosen = jnp.where(hit, 1.0, chosen)
        work = jnp.where(hit, -jnp.inf, work)
    idx_ref[...] = idx.astype(jnp.int32)
    gate_ref[...] = ROUTED_SCALE * sel / jnp.sum(sel, axis=1, keepdims=True)

    r = lax.broadcasted_iota(jnp.int32, (tm, tm), 0)
    c = lax.broadcasted_iota(jnp.int32, (tm, tm), 1)
    lower = jnp.where(c < r, 1.0, 0.0).astype(BF16)
    before = jnp.dot(lower, chosen.astype(BF16), preferred_element_type=F32) + cnt_ref[...]
    rank = jnp.zeros((tm, TOP_K), F32)
    for k in range(TOP_K):
        rank = jnp.where(col == k, jnp.sum(jnp.where(hits[k], before, 0.0), axis=1, keepdims=True), rank)
    rank_ref[...] = rank.astype(jnp.int32)
    cnt_ref[...] = cnt_ref[...] + jnp.sum(chosen, axis=0, keepdims=True)


def _norm_router(x, g, w_router, router_bias):
    T, D = x.shape
    E = w_router.shape[1]
    tm = _pick(T, 256)
    return pl.pallas_call(
        _norm_router_kernel,
        grid=(T // tm,),
        in_specs=[pl.BlockSpec((tm, D), lambda i: (i, 0)),
                  pl.BlockSpec((1, D), lambda i: (0, 0)),
                  pl.BlockSpec((D, E), lambda i: (0, 0)),
                  pl.BlockSpec((1, E), lambda i: (0, 0))],
        out_specs=[pl.BlockSpec((tm, D), lambda i: (i, 0)),
                   pl.BlockSpec((tm, D // 2), lambda i: (i, 0)),
                   pl.BlockSpec((tm, TOP_K), lambda i: (i, 0)),
                   pl.BlockSpec((tm, TOP_K), lambda i: (i, 0)),
                   pl.BlockSpec((tm, TOP_K), lambda i: (i, 0)),
                   pl.BlockSpec((1, E), lambda i: (0, 0))],
        out_shape=[jax.ShapeDtypeStruct((T, D), BF16), jax.ShapeDtypeStruct((T, D // 2), jnp.uint32),
                   jax.ShapeDtypeStruct((T, TOP_K), jnp.int32), jax.ShapeDtypeStruct((T, TOP_K), F32),
                   jax.ShapeDtypeStruct((T, TOP_K), jnp.int32), jax.ShapeDtypeStruct((1, E), F32)],
        compiler_params=_params(("arbitrary",)),
        name="norm_router",
    )(x, g.reshape(1, D), w_router, router_bias.reshape(1, E))


def _mm_wt_kernel(a_ref, wt_ref, o_ref, w_s, *, sigmoid):
    @pl.when(pl.program_id(1) == 0)
    def _():
        w_s[...] = wt_ref[...].T.astype(w_s.dtype)

    acc = jnp.dot(a_ref[...], w_s[...], preferred_element_type=F32)
    if sigmoid:
        acc = _sigmoid_tanh(acc)
    o_ref[...] = acc.astype(o_ref.dtype)


def _matmul_wt(a, wt, row0, n_rows_w, out_dtype, *, sigmoid=False, tm_pref=1088, tn_pref=512, name="matmul_wt"):
    M, K = a.shape
    tm = _pick(M, tm_pref, 16)
    tn = _pick(n_rows_w, tn_pref, LANES)
    if row0 % tn and row0 % SUBLANES:
        wt, row0 = wt[row0:row0 + n_rows_w], 0
    if row0 % tn == 0:
        w_spec = pl.BlockSpec((tn, K), lambda j, i: (row0 // tn + j, 0))
    else:
        w_spec = pl.BlockSpec((pl.Element(tn), pl.Element(K)),
                              lambda j, i: (pl.multiple_of(row0 + j * tn, SUBLANES), 0))
    return pl.pallas_call(
        functools.partial(_mm_wt_kernel, sigmoid=sigmoid),
        grid=(n_rows_w // tn, M // tm),
        in_specs=[pl.BlockSpec((tm, K), lambda j, i: (i, 0)), w_spec],
        out_specs=pl.BlockSpec((tm, tn), lambda j, i: (i, j)),
        out_shape=jax.ShapeDtypeStruct((M, n_rows_w), out_dtype),
        scratch_shapes=[pltpu.VMEM((K, tn), BF16)],
        compiler_params=_params(("arbitrary", "arbitrary")),
        name=name,
    )(a, wt)


def _mm_residual_kernel(a_ref, w_ref, r_ref, o_ref):
    o_ref[...] = r_ref[...] + jnp.dot(a_ref[...], w_ref[...], preferred_element_type=F32)


def _matmul_residual(a, w, res, *, tm_pref=1088, tn_pref=512):
    M, K = a.shape
    N = w.shape[1]
    tm = _pick(M, tm_pref, 16)
    tn = _pick(N, tn_pref, LANES)
    return pl.pallas_call(
        _mm_residual_kernel,
        grid=(N // tn, M // tm),
        in_specs=[pl.BlockSpec((tm, K), lambda j, i: (i, 0)),
                  pl.BlockSpec((K, tn), lambda j, i: (0, j)),
                  pl.BlockSpec((tm, tn), lambda j, i: (i, j))],
        out_specs=pl.BlockSpec((tm, tn), lambda j, i: (i, j)),
        out_shape=jax.ShapeDtypeStruct((M, N), F32),
        compiler_params=_params(("arbitrary", "arbitrary")),
        name="out_proj_residual",
    )(a, w, res)


def _merge_kernel(g_ref, at_ref, wv_ref, wg_ref, wf_ref, sa_ref, sb_ref, o_ref):
    g = g_ref[...]
    val = jnp.dot(g, wv_ref[...], preferred_element_type=F32)
    gate = jnp.dot(g, wg_ref[...], preferred_element_type=F32)
    y_b = jnp.dot(at_ref[...], wf_ref[...], preferred_element_type=F32)
    y_a = val * _sigmoid_tanh(gate)
    o_ref[...] = (sa_ref[...].astype(F32) * y_a + sb_ref[...].astype(F32) * y_b).astype(o_ref.dtype)


def _merge(g, attn, w_glu, w_fox_out, sig_gates, *, tm_pref=1088, tn_pref=512):
    T, K = g.shape
    D = w_fox_out.shape[1]
    tm = _pick(T, tm_pref, 16)
    tn = _pick(D, tn_pref, LANES)
    nj = D // tn
    return pl.pallas_call(
        _merge_kernel,
        grid=(nj, T // tm),
        in_specs=[pl.BlockSpec((tm, K), lambda j, i: (i, 0)),
                  pl.BlockSpec((tm, K), lambda j, i: (i, 0)),
                  pl.BlockSpec((K, tn), lambda j, i: (0, j)),
                  pl.BlockSpec((K, tn), lambda j, i: (0, j + nj)),
                  pl.BlockSpec((K, tn), lambda j, i: (0, j)),
                  pl.BlockSpec((tm, tn), lambda j, i: (i, j)),
                  pl.BlockSpec((tm, tn), lambda j, i: (i, j + nj))],
        out_specs=pl.BlockSpec((tm, tn), lambda j, i: (i, j)),
        out_shape=jax.ShapeDtypeStruct((T, D), BF16),
        compiler_params=_params(("arbitrary", "arbitrary")),
        name="merge_branches",
    )(g, attn, w_glu, w_glu, w_fox_out, sig_gates, sig_gates)


def _s5_disc_kernel(are_ref, aim_ref, ldt_ref, bre_ref, bim_ref, lre_ref, lim_ref, bbre_ref, bbim_ref):
    lam_re = jnp.minimum(are_ref[...], LAM_RE_MAX)
    lam_im = aim_ref[...]
    dt = jnp.exp(ldt_ref[...])
    mag = jnp.exp(lam_re * dt)
    lb_re = mag * jnp.cos(lam_im * dt)
    lb_im = mag * jnp.sin(lam_im * dt)
    lre_ref[...] = lb_re
    lim_ref[...] = lb_im
    n_re = lb_re - 1.0
    den = lam_re * lam_re + lam_im * lam_im
    co_re = (n_re * lam_re + lb_im * lam_im) / den
    co_im = (lb_im * lam_re - n_re * lam_im) / den
    G, P = co_re.shape
    PH = bre_ref.shape[1]
    hg = PH // P
    rows = lax.broadcasted_iota(jnp.int32, (P, PH), 0)
    cols = lax.broadcasted_iota(jnp.int32, (P, PH), 1)
    expand = jnp.where(cols // hg == rows, 1.0, 0.0).astype(F32)
    ce_re = jnp.dot(co_re, expand, precision=HIGHEST, preferred_element_type=F32)
    ce_im = jnp.dot(co_im, expand, precision=HIGHEST, preferred_element_type=F32)
    b_re = bre_ref[...]
    b_im = bim_ref[...]
    bbre_ref[...] = ce_re * b_re - ce_im * b_im
    bbim_ref[...] = ce_re * b_im + ce_im * b_re


def _s5_discretise(a_re, a_im, log_dt, b_re, b_im):
    G, P = a_re.shape
    hg = b_re.shape[2]
    outs = pl.pallas_call(
        _s5_disc_kernel,
        out_shape=[jax.ShapeDtypeStruct((G, P), F32), jax.ShapeDtypeStruct((G, P), F32),
                   jax.ShapeDtypeStruct((G, P * hg), F32), jax.ShapeDtypeStruct((G, P * hg), F32)],
        name="s5_discretise",
    )(a_re, a_im, log_dt.reshape(G, 1), b_re.reshape(G, P * hg), b_im.reshape(G, P * hg))
    lam_re, lam_im, bb_re, bb_im = outs
    return lam_re, lam_im, bb_re.reshape(G, P, hg), bb_im.reshape(G, P, hg)


def _s5_pack_weights(lam_re, lam_im, bb_re, bb_im, c_re, c_im):
    G, P, hg = bb_re.shape
    nblk = G // SSM_BLOCK_GROUPS
    half_ch = SSM_BLOCK_CH // 2
    pairs_per_half = SSM_PAIRS // 2

    def pair_view(x):
        return x.reshape((nblk, SSM_PAIRS, 2) + x.shape[1:])

    bre = pair_view(bb_re)
    bim = pair_view(bb_im)
    zero = jnp.zeros_like(bre[:, :, 0])

    def rows_of(which):
        re = [bre[:, :, 0], zero] if which == 0 else [zero, bre[:, :, 1]]
        im = [bim[:, :, 0], zero] if which == 0 else [zero, bim[:, :, 1]]
        blk = jnp.concatenate(re + im, axis=2)
        return jnp.swapaxes(blk, 2, 3)

    compact = jnp.concatenate([rows_of(0), rows_of(1)], axis=2)
    local = jnp.arange(SSM_PAIRS) % pairs_per_half
    place = jax.nn.one_hot(local[:, None] * 2 * hg + jnp.arange(2 * hg)[None, :], half_ch, dtype=F32)
    bw = jnp.einsum('prc,bprs->bpcs', place, compact)
    bw_hi = bw.astype(BF16)
    bw_lo = (bw - bw_hi.astype(F32)).astype(BF16)
    bw_hi2 = jnp.concatenate([bw_hi, bw_hi], axis=2)

    cre = pair_view(c_re)
    cim = pair_view(c_im)
    zc = jnp.zeros_like(cre[:, :, 0])

    def cols_of(which):
        re = [cre[:, :, 0], zc] if which == 0 else [zc, cre[:, :, 1]]
        im = [-cim[:, :, 0], zc] if which == 0 else [zc, -cim[:, :, 1]]
        return jnp.concatenate(re + im, axis=3)

    ccompact = jnp.concatenate([cols_of(0), cols_of(1)], axis=2)
    cplace = jax.nn.one_hot(jnp.arange(SSM_PAIRS)[:, None] * 2 * hg + jnp.arange(2 * hg)[None, :],
                            SSM_BLOCK_CH, dtype=F32)
    cw = jnp.einsum('prc,bprs->bpsc', cplace, ccompact).astype(BF16)

    lre = pair_view(lam_re)
    lim = pair_view(lam_im)
    a_re = jnp.concatenate([lre[:, :, 0], lre[:, :, 1]], axis=-1)
    a_im = jnp.concatenate([lim[:, :, 0], lim[:, :, 1]], axis=-1)
    return bw_hi2, bw_lo, cw, jnp.concatenate([a_re, a_im], axis=-1)


def _pack_state(s_re, s_im):
    B, G, P = s_re.shape
    nblk = G // SSM_BLOCK_GROUPS
    re = s_re.reshape(B, nblk, SSM_PAIRS, 2 * P)
    im = s_im.reshape(B, nblk, SSM_PAIRS, 2 * P)
    return jnp.moveaxis(jnp.concatenate([re, im], axis=-1), 1, 0)


def _unpack_state(st):
    nblk, B, pairs, w = st.shape
    st = jnp.moveaxis(st, 0, 1)
    P = w // 4
    re = st[..., :2 * P].reshape(B, nblk * pairs * 2, P)
    im = st[..., 2 * P:].reshape(B, nblk * pairs * 2, P)
    return re, im


def _s5_kernel(*refs, n_u, nseq, lc, rp, ilp):
    u_refs = refs[:n_u]
    bw2_ref, bwlo_ref, cw_ref, lam_ref, d_ref, h0_ref, y_ref, hout_ref, sre_ref, sim_ref, st_ref = refs[n_u:]
    chunk = pl.program_id(1)
    rows = nseq * lc
    half = LANES

    @pl.when(chunk == 0)
    def _():
        st_ref[...] = h0_ref[0]

    lam = lam_ref[0]
    sw = lam.shape[1] // 2
    a_re = lam[:, :sw]
    a_im = lam[:, sw:]

    u = u_refs[0][...] if n_u == 1 else jnp.concatenate([r[...] for r in u_refs], axis=0)
    for hf in range(2):
        uh = u[:, hf * half:(hf + 1) * half]
        hi = uh.astype(BF16)
        lo = (uh - hi.astype(F32)).astype(BF16)
        lhs = jnp.concatenate([hi, lo], axis=1)
        for q in range(SSM_PAIRS // 2):
            p = hf * (SSM_PAIRS // 2) + q
            bu = (jnp.dot(lhs, bw2_ref[0, p], preferred_element_type=F32)
                  + jnp.dot(hi, bwlo_ref[0, p], preferred_element_type=F32))
            sre_ref[pl.ds(p * rp, rows), :] = bu[:, :sw]
            sim_ref[pl.ds(p * rp, rows), :] = bu[:, sw:]

    def group_body(g, carry):
        seqs = [g * ilp + j for j in range(ilp)]

        def step(t, c):
            out = []
            for j, s in enumerate(seqs):
                x_re, x_im = c[2 * j], c[2 * j + 1]
                at = pl.ds(s * lc + t, SSM_PAIRS, stride=rp)
                n_re = a_re * x_re - a_im * x_im + sre_ref[at, :]
                n_im = a_re * x_im + a_im * x_re + sim_ref[at, :]
                sre_ref[at, :] = n_re
                sim_ref[at, :] = n_im
                out += [n_re, n_im]
            return tuple(out)

        init = []
        for s in seqs:
            x0 = st_ref[s]
            init += [x0[:, :sw], x0[:, sw:]]
        fin = lax.fori_loop(0, lc, step, tuple(init), unroll=4)
        for j, s in enumerate(seqs):
            st_ref[s] = jnp.concatenate([fin[2 * j], fin[2 * j + 1]], axis=1)
        return carry

    lax.fori_loop(0, nseq // ilp, group_body, 0)
    hout_ref[0] = st_ref[...]

    y = d_ref[...] * u
    for p in range(SSM_PAIRS):
        at = pl.ds(p * rp, rows)
        states = jnp.concatenate([sre_ref[at, :], sim_ref[at, :]], axis=1).astype(BF16)
        y = y + jnp.dot(states, cw_ref[0, p], preferred_element_type=F32)
    inner = math.sqrt(2.0 / math.pi) * (y + 0.044715 * (y * y * y))
    y = (0.5 * y * (1.0 + jnp.tanh(inner))).astype(y_ref.dtype)
    rows_u = rows // n_u
    for j in range(n_u):
        y_ref[j] = y[j * rows_u:(j + 1) * rows_u]


def _s5_mixer(z, row0, n_u, seq_per_u, seq_len, weights, d_skip, h0, *, lc):
    bw2, bwlo, cw, lam = weights
    nblk = bw2.shape[0]
    nseq = n_u * seq_per_u
    rows_u = seq_per_u * lc
    rows = n_u * rows_u
    nchunk = seq_len // lc
    assert nchunk == 1 or seq_per_u == 1
    rp = rows + SUBLANES
    r0 = row0 // rows_u
    sw4 = lam.shape[-1]
    ilp = 4 if nseq % 4 == 0 else 1
    kern = functools.partial(_s5_kernel, n_u=n_u, nseq=nseq, lc=lc, rp=rp, ilp=ilp)

    def u_spec(j):
        return pl.BlockSpec((rows_u, SSM_BLOCK_CH), lambda b, c: (r0 + j * nchunk + c, b))

    return pl.pallas_call(
        kern,
        grid=(nblk, nchunk),
        in_specs=[u_spec(j) for j in range(n_u)] + [
            pl.BlockSpec((1,) + bw2.shape[1:], lambda b, c: (b, 0, 0, 0)),
            pl.BlockSpec((1,) + bwlo.shape[1:], lambda b, c: (b, 0, 0, 0)),
            pl.BlockSpec((1,) + cw.shape[1:], lambda b, c: (b, 0, 0, 0)),
            pl.BlockSpec((1, SSM_PAIRS, sw4), lambda b, c: (b, 0, 0)),
            pl.BlockSpec((1, SSM_BLOCK_CH), lambda b, c: (0, b)),
            pl.BlockSpec((1, nseq, SSM_PAIRS, sw4), lambda b, c: (b, 0, 0, 0))],
        out_specs=[pl.BlockSpec((n_u, rows_u, SSM_BLOCK_CH), lambda b, c: (0, c, b)),
                   pl.BlockSpec((1, nseq, SSM_PAIRS, sw4), lambda b, c: (b, 0, 0, 0))],
        out_shape=[jax.ShapeDtypeStruct((n_u, nchunk * rows_u, nblk * SSM_BLOCK_CH), BF16),
                   jax.ShapeDtypeStruct((nblk, nseq, SSM_PAIRS, sw4), F32)],
        scratch_shapes=[pltpu.VMEM((SSM_PAIRS * rp, sw4 // 2), F32),
                        pltpu.VMEM((SSM_PAIRS * rp, sw4 // 2), F32),
                        pltpu.VMEM((nseq, SSM_PAIRS, sw4), F32)],
        compiler_params=_params(("arbitrary", "arbitrary")),
        name="s5_mixer",
    )(*([z] * n_u), bw2, bwlo, cw, lam, d_skip.reshape(1, -1), h0)


def _cumsum_kernel(x_ref, o_ref, *, chunk):
    H, L = x_ref.shape[1], x_ref.shape[2]
    carry = jnp.zeros((H, 1), F32)
    for s in range(0, L, chunk):
        w = min(chunk, L - s)
        r = lax.broadcasted_iota(jnp.int32, (w, w), 0)
        c = lax.broadcasted_iota(jnp.int32, (w, w), 1)
        tri = jnp.where(r <= c, 1.0, 0.0).astype(F32)
        part = jnp.dot(x_ref[0, :, s:s + w], tri, precision=HIGHEST, preferred_element_type=F32) + carry
        o_ref[0, :, s:s + w] = part
        carry = part[:, w - 1:w]


def _cumsum_last(x):
    B, H, L = x.shape
    return pl.pallas_call(
        functools.partial(_cumsum_kernel, chunk=256),
        grid=(B,),
        in_specs=[pl.BlockSpec((1, H, L), lambda b: (b, 0, 0))],
        out_specs=pl.BlockSpec((1, H, L), lambda b: (b, 0, 0)),
        out_shape=jax.ShapeDtypeStruct((B, H, L), F32),
        compiler_params=_params(("arbitrary",)),
        name="logf_cumsum",
    )(x)


def _pick_head(c_ref, h):
    blk = c_ref[...]
    lane = lax.broadcasted_iota(jnp.int32, blk.shape, 1)
    return jnp.sum(jnp.where(lane == h, blk, 0.0), axis=1, keepdims=True)


def _nt_dot(a, b):
    return lax.dot_general(a, b, (((1,), (1,)), ((), ())), preferred_element_type=F32)


def _fox_prompt_kernel(q_ref, k_ref, v_ref, cq_ref, ck_ref, o_ref, *, tq, nq, scale):
    h = pl.program_id(1)
    qi = pl.program_id(2)
    q = (q_ref[...] * scale).astype(BF16)
    cq = _pick_head(cq_ref, h)

    def block(ki, carry, diagonal):
        m_prev, l_prev, acc = carry
        at = pl.ds(pl.multiple_of(ki * tq, tq), tq)
        s = _nt_dot(q, k_ref[at, :].astype(BF16)) + cq - ck_ref[0, pl.ds(h * nq + ki, 1), :]
        if diagonal:
            r = lax.broadcasted_iota(jnp.int32, (tq, tq), 0)
            c = lax.broadcasted_iota(jnp.int32, (tq, tq), 1)
            s = jnp.where(c <= r, s, MASK_VALUE)
        m_new = jnp.maximum(m_prev, jnp.max(s, axis=1, keepdims=True))
        alpha = jnp.exp(m_prev - m_new)
        p = jnp.exp(s - m_new)
        l_new = alpha * l_prev + jnp.sum(p, axis=1, keepdims=True)
        acc = alpha * acc + jnp.dot(p.astype(BF16), v_ref[at, :].astype(BF16), preferred_element_type=F32)
        return m_new, l_new, acc

    init = (jnp.full((tq, 1), MASK_VALUE, F32), jnp.zeros((tq, 1), F32), jnp.zeros(o_ref.shape, F32))
    carry = lax.fori_loop(0, qi, lambda ki, c: block(ki, c, False), init)
    _, l, acc = block(qi, carry, True)
    o_ref[...] = (acc / l).astype(o_ref.dtype)


def _fox_prompt(z, c_col, c_row, B, L, H, col_q, col_k, col_v, scale):
    dh = FOX_HEAD_DIM
    tq = _pick(L, 512)
    nq = L // tq
    return pl.pallas_call(
        functools.partial(_fox_prompt_kernel, tq=tq, nq=nq, scale=scale),
        grid=(B, H, nq),
        in_specs=[pl.BlockSpec((tq, dh), lambda b, h, qi: (b * nq + qi, col_q + h)),
                  pl.BlockSpec((L, dh), lambda b, h, qi: (b, col_k + h)),
                  pl.BlockSpec((L, dh), lambda b, h, qi: (b, col_v + h)),
                  pl.BlockSpec((tq, H), lambda b, h, qi: (b * nq + qi, 0)),
                  pl.BlockSpec((1, H * nq, tq), lambda b, h, qi: (b, 0, 0))],
        out_specs=pl.BlockSpec((tq, dh), lambda b, h, qi: (b * nq + qi, h)),
        out_shape=jax.ShapeDtypeStruct((B * L, H * dh), BF16),
        compiler_params=_params(("arbitrary",) * 3),
        name="fox_prompt",
    )(z, z, z, c_col, c_row.reshape(B, H * nq, tq))


def _fox_sample_kernel(q_ref, kn_ref, vn_ref, kc_ref, vc_ref, cq_ref, ck_ref, o_ref, *, past, n_heads, scale):
    dh = FOX_HEAD_DIM
    n = q_ref.shape[0]
    r = lax.broadcasted_iota(jnp.int32, (n, n), 0)
    c = lax.broadcasted_iota(jnp.int32, (n, n), 1)
    causal = c <= r
    cq_all = cq_ref[...]
    outs = []
    for h in range(n_heads):
        cols = slice(h * dh, (h + 1) * dh)
        q = (q_ref[:, cols] * scale).astype(BF16)
        head_rows = pl.ds(h, past, stride=n_heads)
        c_all = ck_ref[0, h:h + 1, :]
        cq = cq_all[:, h:h + 1]
        s_old = _nt_dot(q, kc_ref[0, head_rows, :].astype(BF16)) + cq - c_all[:, :past]
        s_new = _nt_dot(q, kn_ref[:, cols].astype(BF16)) + cq - c_all[:, past:]
        s_new = jnp.where(causal, s_new, MASK_VALUE)
        m = jnp.maximum(jnp.max(s_old, axis=1, keepdims=True), jnp.max(s_new, axis=1, keepdims=True))
        p_old = jnp.exp(s_old - m)
        p_new = jnp.exp(s_new - m)
        l = jnp.sum(p_old, axis=1, keepdims=True) + jnp.sum(p_new, axis=1, keepdims=True)
        acc = (jnp.dot(p_old.astype(BF16), vc_ref[0, head_rows, :].astype(BF16), preferred_element_type=F32)
               + jnp.dot(p_new.astype(BF16), vn_ref[:, cols].astype(BF16), preferred_element_type=F32))
        outs.append((acc / l).astype(o_ref.dtype))
    o_ref[...] = jnp.concatenate(outs, axis=1)


def _fox_sample(z, row0, cache_k, cache_v, c_col, c_row, B, n, H, col_q, col_k, col_v, scale):
    dh = FOX_HEAD_DIM
    past = cache_k.shape[1] // H
    r0 = row0 // n
    width = H * dh
    assert (col_q * dh) % width == 0 and (col_k * dh) % width == 0 and (col_v * dh) % width == 0
    return pl.pallas_call(
        functools.partial(_fox_sample_kernel, past=past, n_heads=H, scale=scale),
        grid=(B,),
        in_specs=[pl.BlockSpec((n, width), lambda b: (r0 + b, col_q * dh // width)),
                  pl.BlockSpec((n, width), lambda b: (r0 + b, col_k * dh // width)),
                  pl.BlockSpec((n, width), lambda b: (r0 + b, col_v * dh // width)),
                  pl.BlockSpec((1, past * H, dh), lambda b: (b, 0, 0)),
                  pl.BlockSpec((1, past * H, dh), lambda b: (b, 0, 0)),
                  pl.BlockSpec((n, H), lambda b: (b, 0)),
                  pl.BlockSpec((1, H, past + n), lambda b: (b, 0, 0))],
        out_specs=pl.BlockSpec((n, width), lambda b: (b, 0)),
        out_shape=jax.ShapeDtypeStruct((B * n, width), BF16),
        compiler_params=_params(("arbitrary",)),
        name="fox_sample",
    )(z, z, z, cache_k, cache_v, c_col, c_row)


def _gather_rows_kernel(tok_ref, used_ref, h_hbm, o_ref, buf, sem, *, tm):
    i = pl.program_id(0)
    n = used_ref[0]

    def row_copy(blk, slot, r):
        t = tok_ref[blk * tm + r]
        return pltpu.make_async_copy(h_hbm.at[pl.ds(t, 1), :], buf.at[slot, pl.ds(r, 1), :], sem.at[slot])

    def issue(blk, slot):
        def body(r, c):
            row_copy(blk, slot, r).start()
            return c
        lax.fori_loop(0, tm, body, 0, unroll=8)

    def drain(blk, slot):
        del blk
        pltpu.make_async_copy(h_hbm.at[pl.ds(0, tm), :], buf.at[slot], sem.at[slot]).wait()

    @pl.when(jnp.logical_and(i == 0, n > 0))
    def _():
        issue(0, 0)

    @pl.when(i + 1 < n)
    def _():
        issue(i + 1, (i + 1) % 2)

    @pl.when(i < n)
    def _():
        slot = i % 2
        drain(i, slot)
        words = buf[slot]
        lo = lax.bitcast_convert_type(words << 16, F32)
        hi = lax.bitcast_convert_type(words & jnp.uint32(0xFFFF0000), F32)
        o_ref[...] = jnp.concatenate([lo, hi], axis=1).astype(o_ref.dtype)

    @pl.when(i >= n)
    def _():
        o_ref[...] = jnp.zeros(o_ref.shape, o_ref.dtype)


def _gather_rows(h_packed, slot_tok, n_used, tm):
    h = h_packed
    n_slots = slot_tok.shape[0]
    D = 2 * h.shape[1]
    return pl.pallas_call(
        functools.partial(_gather_rows_kernel, tm=tm),
        grid_spec=pltpu.PrefetchScalarGridSpec(
            num_scalar_prefetch=2,
            grid=(n_slots // tm,),
            in_specs=[pl.BlockSpec(memory_space=pl.ANY)],
            out_specs=pl.BlockSpec((tm, D), lambda i, tok, used: (i, 0)),
            scratch_shapes=[pltpu.VMEM((2, tm, D // 2), h.dtype), pltpu.SemaphoreType.DMA((2,))]),
        out_shape=jax.ShapeDtypeStruct((n_slots, D), BF16),
        compiler_params=_params(("arbitrary",)),
        name="moe_dispatch_gather",
    )(slot_tok, n_used, h)


def _grouped_kernel(blk0_ref, nblk_ref, x_hbm, *refs, n_w, tm, tn):
    w_refs = refs[:n_w]
    o_hbm = refs[n_w]
    w_bf = refs[n_w + 1:2 * n_w + 1]
    xbuf, obuf, tbuf, primed_ref, in_sem, out_sem, tail_sem = refs[2 * n_w + 1:]
    col = pl.program_id(0)
    e = pl.program_id(1)
    n_col = pl.num_programs(0)
    n_exp = pl.num_programs(1)
    nblk = nblk_ref[e]
    blk0 = blk0_ref[e]
    npair = nblk // 2

    def block_copy(first, j, slot):
        return pltpu.make_async_copy(x_hbm.at[pl.ds((first + j) * tm, tm), :], xbuf.at[slot], in_sem.at[slot])

    def x_copy(j, slot):
        return block_copy(blk0, j, slot)

    def pair_base(p):
        return jnp.where(p == 0, PRIME_SLOT, (p % 2) * 2)

    def prime(first, count):
        block_copy(first, 0, PRIME_SLOT).start()

        @pl.when(count > 1)
        def _():
            block_copy(first, 1, PRIME_SLOT + 1).start()

    @pl.when(jnp.logical_and(col == 0, e == 0))
    def _():
        primed_ref[0] = 0

    has_next = jnp.logical_or(e + 1 < n_exp, col + 1 < n_col)
    e_next = jnp.where(e + 1 < n_exp, e + 1, 0)
    nblk_next = nblk_ref[e_next]
    prime_next = jnp.logical_and(has_next, nblk_next > 0)

    def request_next():
        @pl.when(prime_next)
        def _():
            prime(blk0_ref[e_next], nblk_next)

    def pair_out(p, slot):
        return pltpu.make_async_copy(obuf.at[slot],
                                     o_hbm.at[pl.ds((blk0 + 2 * p) * tm, 2 * tm), pl.ds(col * tn, tn)],
                                     out_sem.at[slot])

    def tail_out():
        return pltpu.make_async_copy(tbuf, o_hbm.at[pl.ds((blk0 + nblk - 1) * tm, tm), pl.ds(col * tn, tn)],
                                     tail_sem)

    def apply(x):
        if n_w == 2:
            a = jnp.dot(x, w_bf[0][...], preferred_element_type=F32)
            b = jnp.dot(x, w_bf[1][...], preferred_element_type=F32)
            return a * _sigmoid_tanh(a) * b
        return jnp.dot(x, w_bf[0][...], preferred_element_type=F32)

    @pl.when(nblk > 0)
    def _():
        @pl.when(primed_ref[0] == 0)
        def _():
            prime(blk0, nblk)

        for w_ref, dst in zip(w_refs, w_bf):
            dst[...] = w_ref[0].astype(BF16)

        def pair_body(p, carry):
            s0 = pair_base(p)
            s_next = ((p + 1) % 2) * 2

            @pl.when(2 * p + 2 < nblk)
            def _():
                x_copy(2 * p + 2, s_next).start()

            @pl.when(2 * p + 3 < nblk)
            def _():
                x_copy(2 * p + 3, s_next + 1).start()

            x_copy(2 * p, s0).wait()
            x_copy(2 * p + 1, s0 + 1).wait()
            y = apply(xbuf[pl.ds(s0, 2)].reshape(2 * tm, xbuf.shape[2]))
            oslot = p % 2

            @pl.when(p >= 2)
            def _():
                pair_out(p - 2, oslot).wait()

            obuf[oslot] = y.astype(obuf.dtype)
            pair_out(p, oslot).start()

            @pl.when(p == 0)
            def _():
                request_next()

            return carry

        lax.fori_loop(0, npair, pair_body, 0)

        @pl.when(nblk % 2 == 1)
        def _():
            slot = pair_base(npair)
            x_copy(nblk - 1, slot).wait()
            tbuf[...] = apply(xbuf[slot]).astype(tbuf.dtype)
            tail_out().start()

            @pl.when(npair == 0)
            def _():
                request_next()

        @pl.when(npair >= 2)
        def _():
            pair_out(npair - 2, npair % 2).wait()

        @pl.when(npair >= 1)
        def _():
            pair_out(npair - 1, (npair - 1) % 2).wait()

        @pl.when(nblk % 2 == 1)
        def _():
            tail_out().wait()

    @pl.when(nblk == 0)
    def _():
        request_next()

    primed_ref[0] = prime_next.astype(jnp.int32)

    @pl.when(e == n_exp - 1)
    def _():
        first = blk0 + nblk
        total = o_hbm.shape[0] // tm

        def zero_copy(j):
            return pltpu.make_async_copy(tbuf, o_hbm.at[pl.ds(j * tm, tm), pl.ds(col * tn, tn)], tail_sem)

        @pl.when(first < total)
        def _():
            tbuf[...] = jnp.zeros(tbuf.shape, tbuf.dtype)

        def start(j, c):
            zero_copy(j).start()
            return c

        def wait(j, c):
            zero_copy(j).wait()
            return c

        lax.fori_loop(first, total, start, 0)
        lax.fori_loop(first, total, wait, 0)


def _grouped_matmul(x, weights, blk0, nblk, tm, tn_pref, out_dtype, name):
    n_rows, K = x.shape
    E, _, N = weights[0].shape
    n_w = len(weights)
    tn = _pick(N, tn_pref, LANES)
    return pl.pallas_call(
        functools.partial(_grouped_kernel, n_w=n_w, tm=tm, tn=tn),
        grid_spec=pltpu.PrefetchScalarGridSpec(
            num_scalar_prefetch=2,
            grid=(N // tn, E),
            in_specs=[pl.BlockSpec(memory_space=pl.ANY)]
            + [pl.BlockSpec((1, K, tn), lambda c, e, b0, nb: (e, 0, c)) for _ in weights],
            out_specs=pl.BlockSpec(memory_space=pl.ANY),
            scratch_shapes=[pltpu.VMEM((K, tn), BF16) for _ in weights] + [
                pltpu.VMEM((PRIME_SLOT + 2, tm, K), x.dtype),
                pltpu.VMEM((2, 2 * tm, tn), out_dtype),
                pltpu.VMEM((tm, tn), out_dtype),
                pltpu.SMEM((1,), jnp.int32),
                pltpu.SemaphoreType.DMA((PRIME_SLOT + 2,)),
                pltpu.SemaphoreType.DMA((2,)),
                pltpu.SemaphoreType.DMA(())]),
        out_shape=jax.ShapeDtypeStruct((n_rows, N), out_dtype),
        compiler_params=_params(("arbitrary", "arbitrary")),
        name=name,
    )(blk0, nblk, x, *weights)


def _combine_kernel(dest_ref, x_ref, sh_ref, gate_ref, g_ref, rows_hbm, o_ref, buf, sem, *, tc, final_norm,
                    row0):
    i = pl.program_id(0)
    n = pl.num_programs(0)

    def row_copy(blk, slot, r, k):
        src = dest_ref[(row0 + blk * tc + r) * TOP_K + k]
        return pltpu.make_async_copy(rows_hbm.at[pl.ds(src, 1), :], buf.at[slot, k, pl.ds(r, 1), :],
                                     sem.at[slot])

    def issue(blk, slot):
        def body(r, c):
            for k in range(TOP_K):
                row_copy(blk, slot, r, k).start()
            return c
        lax.fori_loop(0, tc, body, 0)

    def drain(blk, slot):
        del blk
        for k in range(TOP_K):
            pltpu.make_async_copy(rows_hbm.at[pl.ds(0, tc), :], buf.at[slot, k], sem.at[slot]).wait()

    @pl.when(i == 0)
    def _():
        issue(0, 0)

    @pl.when(i + 1 < n)
    def _():
        issue(i + 1, (i + 1) % 2)

    slot = i % 2
    drain(i, slot)
    y = x_ref[...] + sh_ref[...]
    gates = gate_ref[...]
    for k in range(TOP_K):
        y = y + gates[:, k:k + 1] * buf[slot, k]
    o_ref[...] = _rms(y, g_ref[...]) if final_norm else y


def _combine(x, shared, gates, dest, rows, g_final, final_norm, row0, n_rows, tc=32):
    D = x.shape[1]
    tc = _pick(math.gcd(row0, n_rows) if row0 else n_rows, tc)
    b0 = row0 // tc
    return pl.pallas_call(
        functools.partial(_combine_kernel, tc=tc, final_norm=final_norm, row0=row0),
        grid_spec=pltpu.PrefetchScalarGridSpec(
            num_scalar_prefetch=1,
            grid=(n_rows // tc,),
            in_specs=[pl.BlockSpec((tc, D), lambda i, d: (b0 + i, 0)),
                      pl.BlockSpec((tc, D), lambda i, d: (b0 + i, 0)),
                      pl.BlockSpec((tc, TOP_K), lambda i, d: (b0 + i, 0)),
                      pl.BlockSpec((1, D), lambda i, d: (0, 0)),
                      pl.BlockSpec(memory_space=pl.ANY)],
            out_specs=pl.BlockSpec((tc, D), lambda i, d: (i, 0)),
            scratch_shapes=[pltpu.VMEM((2, TOP_K, tc, D), F32), pltpu.SemaphoreType.DMA((2,))]),
        out_shape=jax.ShapeDtypeStruct((n_rows, D), F32),
        compiler_params=_params(("arbitrary",)),
        name="moe_combine_norm",
    )(dest, x, shared, gates, g_final.reshape(1, D), rows)


def _slot_assign_kernel(start_ref, idx_ref, rank_ref, dest_ref, *, n_experts):
    idx = idx_ref[...]
    base = jnp.zeros(idx.shape, jnp.int32)
    for e in range(n_experts):
        base = jnp.where(idx == e, start_ref[e], base)
    dest_ref[...] = base + rank_ref[...]


def _routing_tables(idx, rank, counts, n_experts, tm):
    T = idx.shape[0]
    n_assign = T * TOP_K
    n_blocks = (n_assign + n_experts * (tm - 1) + tm - 1) // tm
    counts = counts.reshape(n_experts).astype(jnp.int32)
    padded = (counts + tm - 1) // tm * tm
    pad_end = jnp.cumsum(padded)
    pad_start = (pad_end - padded).astype(jnp.int32)
    shape2d = (n_assign // LANES, LANES) if n_assign % LANES == 0 else (T, TOP_K)
    dest = pl.pallas_call(
        functools.partial(_slot_assign_kernel, n_experts=n_experts),
        grid_spec=pltpu.PrefetchScalarGridSpec(
            num_scalar_prefetch=1, grid=(1,),
            in_specs=[pl.BlockSpec(shape2d, lambda i, s: (0, 0)), pl.BlockSpec(shape2d, lambda i, s: (0, 0))],
            out_specs=pl.BlockSpec(shape2d, lambda i, s: (0, 0))),
        out_shape=jax.ShapeDtypeStruct(shape2d, jnp.int32),
        name="moe_slot_assign",
    )(pad_start, idx.reshape(shape2d), rank.reshape(shape2d)).reshape(n_assign)
    slot_tok = jnp.zeros((n_blocks * tm,), jnp.int32).at[dest].set(
        jnp.arange(n_assign, dtype=jnp.int32) // TOP_K)
    return dest, slot_tok, (pad_start // tm).astype(jnp.int32), (padded // tm).astype(jnp.int32)


def kernel(x_prompt, x_sample, cache_k, cache_v, cache_logf, state_ssm_re, state_ssm_im, norm_mix, w_in, b_forget, ssm_a_re, ssm_a_im, ssm_log_dt, ssm_b_re, ssm_b_im, ssm_c_re, ssm_c_im, ssm_d, w_glu, w_fox_out, w_out, norm_ffn, w_router, router_bias, w_exp_gate, w_exp_up, w_exp_down, w_sh_gate, w_sh_up, w_sh_down, norm_final):
    Bp, Lp, D = x_prompt.shape
    Bs, Ls, _ = x_sample.shape
    depth = w_in.shape[0]
    past = cache_k.shape[2]
    H = cache_k.shape[3]
    dh = cache_k.shape[4]
    assert dh == FOX_HEAD_DIM
    d_fox = H * dh
    d_ssm = ssm_d.shape[1]
    n_experts = w_router.shape[2]
    Tp = Bp * Lp
    Ts = Bs * Ls
    T = Tp + Ts
    scale = dh ** -0.5
    n_uqkv = d_ssm + 3 * d_fox
    col_q = d_ssm // dh
    col_k = col_q + H
    col_v = col_k + H
    moe_tm = 256

    x = jnp.concatenate([x_prompt.reshape(Tp, D), x_sample.reshape(Ts, D)], axis=0)
    outs = {name: [] for name in ("kp", "vp", "fp", "rp", "ip", "ks", "vs", "fs", "rs", "is")}
    for l in range(depth):
        wt_in = jnp.swapaxes(w_in[l], 0, 1)
        wt_f = wt_in[n_uqkv:n_uqkv + H]

        h, logf = _norm_forget(x, norm_mix[l], wt_f, b_forget[l])
        z = _matmul_wt(h, wt_in, 0, n_uqkv, F32, name="in_proj")
        sig_gates = _matmul_wt(h, wt_in, n_uqkv + H, 2 * D, BF16, sigmoid=True, name="gate_proj")

        lam_re, lam_im, bb_re, bb_im = _s5_discretise(ssm_a_re[l], ssm_a_im[l], ssm_log_dt[l],
                                                      ssm_b_re[l], ssm_b_im[l])
        s5_w = _s5_pack_weights(lam_re, lam_im, bb_re, bb_im, ssm_c_re[l], ssm_c_im[l])
        zero_state = jnp.zeros((Bp,) + state_ssm_re.shape[2:], F32)
        g_p, st_p = _s5_mixer(z, 0, Bp, 1, Lp, s5_w, ssm_d[l], _pack_state(zero_state, zero_state),
                              lc=_pick(Lp, 256))
        g_s, st_s = _s5_mixer(z, Tp, 1, Bs, Ls, s5_w, ssm_d[l],
                              _pack_state(state_ssm_re[l].astype(F32), state_ssm_im[l].astype(F32)), lc=Ls)
        g = jnp.concatenate([g_p.reshape(Tp, d_ssm), g_s.reshape(Ts, d_ssm)], axis=0)

        logf_p = logf[:Tp].reshape(Bp, Lp, H)
        logf_s = logf[Tp:].reshape(Bs, Ls, H)
        crow_p = _cumsum_last(jnp.swapaxes(logf_p, 1, 2))
        ccol_p = jnp.swapaxes(crow_p, 1, 2).reshape(Tp, H)
        attn_p = _fox_prompt(z, ccol_p, crow_p, Bp, Lp, H, col_q, col_k, col_v, scale)
        lf_all = jnp.concatenate([cache_logf[l].astype(F32), logf_s], axis=1)
        crow_s = _cumsum_last(jnp.swapaxes(lf_all, 1, 2))
        ccol_s = jnp.swapaxes(crow_s[:, :, past:], 1, 2).reshape(Ts, H)
        attn_s = _fox_sample(z, Tp, cache_k[l].reshape(Bs, past * H, dh), cache_v[l].reshape(Bs, past * H, dh),
                             ccol_s, crow_s, Bs, Ls, H, col_q, col_k, col_v, scale)
        attn = jnp.concatenate([attn_p, attn_s], axis=0)

        merged = _merge(g, attn, w_glu[l].astype(BF16), w_fox_out[l].astype(BF16), sig_gates)
        x = _matmul_residual(merged, w_out[l].astype(BF16), x)

        norm_out = norm_ffn[l]
        h2, h2p, idx, gates, rank, counts = _norm_router(x, norm_out, w_router[l], router_bias[l])
        dest, slot_tok, e_blk0, e_nblk = _routing_tables(idx, rank, counts, n_experts, moe_tm)
        xs = _gather_rows(h2p, slot_tok, (e_blk0[-1:] + e_nblk[-1:]), moe_tm)
        act = _grouped_matmul(xs, (w_exp_gate[l], w_exp_up[l]), e_blk0, e_nblk, moe_tm, 512, BF16, "expert_up")
        routed = _grouped_matmul(act, (w_exp_down[l],), e_blk0, e_nblk, moe_tm, 2048, F32, "expert_down")
        sh_tm = _pick(T, moe_tm, 16)
        sh_blk0 = jnp.zeros((1,), jnp.int32)
        sh_nblk = jnp.full((1,), T // sh_tm, jnp.int32)
        sh_act = _grouped_matmul(h2, (w_sh_gate[l][None], w_sh_up[l][None]), sh_blk0, sh_nblk, sh_tm, 512, BF16,
                                 "shared_up")
        shared = _grouped_matmul(sh_act, (w_sh_down[l][None],), sh_blk0, sh_nblk, sh_tm, 2048, F32,
                                 "shared_down")
        if l == depth - 1:
            y_p = _combine(x, shared, gates, dest, routed, norm_final, True, 0, Tp)
            y_s = _combine(x, shared, gates, dest, routed, norm_final, True, Tp, Ts)
        else:
            x = _combine(x, shared, gates, dest, routed, norm_final, False, 0, T)

        k_all = z[:, d_ssm + d_fox:d_ssm + 2 * d_fox]
        v_all = z[:, d_ssm + 2 * d_fox:d_ssm + 3 * d_fox]
        re_p, im_p = _unpack_state(st_p)
        re_s, im_s = _unpack_state(st_s)
        outs["kp"].append(k_all[:Tp].reshape(Bp, Lp, H, dh))
        outs["vp"].append(v_all[:Tp].reshape(Bp, Lp, H, dh))
        outs["fp"].append(logf_p)
        outs["rp"].append(re_p)
        outs["ip"].append(im_p)
        outs["ks"].append(k_all[Tp:].reshape(Bs, Ls, H, dh))
        outs["vs"].append(v_all[Tp:].reshape(Bs, Ls, H, dh))
        outs["fs"].append(logf_s)
        outs["rs"].append(re_s)
        outs["is"].append(im_s)

    return (y_p.reshape(Bp, Lp, D), y_s.reshape(Bs, Ls, D),
            jnp.stack(outs["kp"]), jnp.stack(outs["vp"]), jnp.stack(outs["fp"]),
            jnp.stack(outs["rp"]), jnp.stack(outs["ip"]),
            jnp.stack(outs["ks"]), jnp.stack(outs["vs"]), jnp.stack(outs["fs"]),
            jnp.stack(outs["rs"]), jnp.stack(outs["is"]))
```

```python
import functools
import math

import jax
import jax.numpy as jnp
from jax import lax
from jax.experimental import pallas as pl
from jax.experimental.pallas import tpu as pltpu

F32 = jnp.float32
BF16 = jnp.bfloat16
HIGHEST = lax.Precision.HIGHEST

SSM_GROUP = 16
FOX_HEAD_DIM = 128
TOP_K = 8
ROUTED_SCALE = 2.5
NORM_EPS = 1e-6
LAM_RE_MAX = -1e-4

V7X_VMEM_BYTES = 64 * 1024 * 1024
VMEM_LIMIT = V7X_VMEM_BYTES - 6 * 1024 * 1024
LANES = 128
SUBLANES = 8
SSM_BLOCK_GROUPS = 16
SSM_BLOCK_CH = SSM_BLOCK_GROUPS * SSM_GROUP
SSM_PAIRS = SSM_BLOCK_GROUPS // 2
MASK_VALUE = -1e30
LOG2_E = 1.4426950408889634
PRIME_SLOT = 4


def _pick(n, pref, mult=SUBLANES):
    best = None
    d = mult
    while d <= min(n, pref):
        if n % d == 0:
            best = d
        d += mult
    return best if best is not None else n


def _params(sem, vmem=VMEM_LIMIT):
    return pltpu.CompilerParams(dimension_semantics=sem, vmem_limit_bytes=vmem)


def _sigmoid(x):
    return 1.0 / (1.0 + jnp.exp(-x))


def _sigmoid_tanh(x):
    return 0.5 * jnp.tanh(0.5 * x) + 0.5


def _rms(x, g):
    r = lax.rsqrt(jnp.mean(x * x, axis=-1, keepdims=True) + NORM_EPS)
    return x * r * g


def _dot_split(a, b, dims):
    def split(x):
        hi = x.astype(BF16)
        return hi, (x - hi.astype(F32)).astype(BF16)

    def dot(x, y):
        return lax.dot_general(x, y, dims, preferred_element_type=F32)

    a_hi, a_lo = split(a)
    b_hi, b_lo = split(b)
    return dot(a_hi, b_hi) + dot(a_lo, b_hi) + dot(a_hi, b_lo)


def _norm_forget_kernel(x_ref, g_ref, w_ref, b_ref, h_ref, logf_ref):
    h = _rms(x_ref[...], g_ref[...])
    h_ref[...] = h.astype(h_ref.dtype)
    z = _dot_split(h, w_ref[...], (((1,), (1,)), ((), ()))) + b_ref[...]
    logf_ref[...] = jnp.minimum(z, 0.0) - jnp.log1p(jnp.exp(-jnp.abs(z)))


def _norm_forget(x, g, wt_f, b_f):
    T, D = x.shape
    H = wt_f.shape[0]
    tm = _pick(T, 256)
    return pl.pallas_call(
        _norm_forget_kernel,
        grid=(T // tm,),
        in_specs=[pl.BlockSpec((tm, D), lambda i: (i, 0)),
                  pl.BlockSpec((1, D), lambda i: (0, 0)),
                  pl.BlockSpec((H, D), lambda i: (0, 0)),
                  pl.BlockSpec((1, H), lambda i: (0, 0))],
        out_specs=[pl.BlockSpec((tm, D), lambda i: (i, 0)),
                   pl.BlockSpec((tm, H), lambda i: (i, 0))],
        out_shape=[jax.ShapeDtypeStruct((T, D), BF16), jax.ShapeDtypeStruct((T, H), F32)],
        compiler_params=_params(("arbitrary",)),
        name="norm_forget",
    )(x, g.reshape(1, D), wt_f, b_f.reshape(1, H))


def _norm_router_kernel(x_ref, g_ref, w_ref, b_ref, h_ref, hp_ref, idx_ref, gate_ref, rank_ref, cnt_ref):
    @pl.when(pl.program_id(0) == 0)
    def _():
        cnt_ref[...] = jnp.zeros(cnt_ref.shape, F32)

    h = _rms(x_ref[...], g_ref[...])
    hb = h.astype(BF16)
    h_ref[...] = hb
    bits = lax.bitcast_convert_type(hb.astype(F32), jnp.uint32)
    half = bits.shape[1] // 2
    hp_ref[...] = (bits[:, half:] & jnp.uint32(0xFFFF0000)) | (bits[:, :half] >> 16)
    scores = _sigmoid(_dot_split(h, w_ref[...], (((1,), (0,)), ((), ()))))
    tm, E = scores.shape
    lane = lax.broadcasted_iota(jnp.int32, (tm, E), 1).astype(F32)
    col = lax.broadcasted_iota(jnp.int32, (tm, TOP_K), 1)
    work = scores + b_ref[...]
    idx = jnp.zeros((tm, TOP_K), F32)
    sel = jnp.zeros((tm, TOP_K), F32)
    chosen = jnp.zeros((tm, E), F32)
    hits = []
    for k in range(TOP_K):
        m = jnp.max(work, axis=1, keepdims=True)
        am = jnp.min(jnp.where(work == m, lane, float(E)), axis=1, keepdims=True)
        hit = lane == am
        hits.append(hit)
        sk = jnp.sum(jnp.where(hit, scores, 0.0), axis=1, keepdims=True)
        idx = jnp.where(col == k, am, idx)
        sel = jnp.where(col == k, sk, sel)
        chosen = jnp.where(hit, 1.0, chosen)
        work = jnp.where(hit, -jnp.inf, work)
    idx_ref[...] = idx.astype(jnp.int32)
    gate_ref[...] = ROUTED_SCALE * sel / jnp.sum(sel, axis=1, keepdims=True)

    r = lax.broadcasted_iota(jnp.int32, (tm, tm), 0)
    c = lax.broadcasted_iota(jnp.int32, (tm, tm), 1)
    lower = jnp.where(c < r, 1.0, 0.0).astype(BF16)
    before = jnp.dot(lower, chosen.astype(BF16), preferred_element_type=F32) + cnt_ref[...]
    rank = jnp.zeros((tm, TOP_K), F32)
    for k in range(TOP_K):
        rank = jnp.where(col == k, jnp.sum(jnp.where(hits[k], before, 0.0), axis=1, keepdims=True), rank)
    rank_ref[...] = rank.astype(jnp.int32)
    cnt_ref[...] = cnt_ref[...] + jnp.sum(chosen, axis=0, keepdims=True)


def _norm_router(x, g, w_router, router_bias):
    T, D = x.shape
    E = w_router.shape[1]
    tm = _pick(T, 256)
    return pl.pallas_call(
        _norm_router_kernel,
        grid=(T // tm,),
        in_specs=[pl.BlockSpec((tm, D), lambda i: (i, 0)),
                  pl.BlockSpec((1, D), lambda i: (0, 0)),
                  pl.BlockSpec((D, E), lambda i: (0, 0)),
                  pl.BlockSpec((1, E), lambda i: (0, 0))],
        out_specs=[pl.BlockSpec((tm, D), lambda i: (i, 0)),
                   pl.BlockSpec((tm, D // 2), lambda i: (i, 0)),
                   pl.BlockSpec((tm, TOP_K), lambda i: (i, 0)),
                   pl.BlockSpec((tm, TOP_K), lambda i: (i, 0)),
                   pl.BlockSpec((tm, TOP_K), lambda i: (i, 0)),
                   pl.BlockSpec((1, E), lambda i: (0, 0))],
        out_shape=[jax.ShapeDtypeStruct((T, D), BF16), jax.ShapeDtypeStruct((T, D // 2), jnp.uint32),
                   jax.ShapeDtypeStruct((T, TOP_K), jnp.int32), jax.ShapeDtypeStruct((T, TOP_K), F32),
                   jax.ShapeDtypeStruct((T, TOP_K), jnp.int32), jax.ShapeDtypeStruct((1, E), F32)],
        compiler_params=_params(("arbitrary",)),
        name="norm_router",
    )(x, g.reshape(1, D), w_router, router_bias.reshape(1, E))


def _mm_wt_kernel(a_ref, wt_ref, o_ref, w_s, *, sigmoid):
    @pl.when(pl.program_id(1) == 0)
    def _():
        w_s[...] = wt_ref[...].T.astype(w_s.dtype)

    acc = jnp.dot(a_ref[...], w_s[...], preferred_element_type=F32)
    if sigmoid:
        acc = _sigmoid_tanh(acc)
    o_ref[...] = acc.astype(o_ref.dtype)


def _matmul_wt(a, wt, row0, n_rows_w, out_dtype, *, sigmoid=False, tm_pref=1088, tn_pref=512, name="matmul_wt"):
    M, K = a.shape
    tm = _pick(M, tm_pref, 16)
    tn = _pick(n_rows_w, tn_pref, LANES)
    if row0 % tn and row0 % SUBLANES:
        wt, row0 = wt[row0:row0 + n_rows_w], 0
    if row0 % tn == 0:
        w_spec = pl.BlockSpec((tn, K), lambda j, i: (row0 // tn + j, 0))
    else:
        w_spec = pl.BlockSpec((pl.Element(tn), pl.Element(K)),
                              lambda j, i: (pl.multiple_of(row0 + j * tn, SUBLANES), 0))
    return pl.pallas_call(
        functools.partial(_mm_wt_kernel, sigmoid=sigmoid),
        grid=(n_rows_w // tn, M // tm),
        in_specs=[pl.BlockSpec((tm, K), lambda j, i: (i, 0)), w_spec],
        out_specs=pl.BlockSpec((tm, tn), lambda j, i: (i, j)),
        out_shape=jax.ShapeDtypeStruct((M, n_rows_w), out_dtype),
        scratch_shapes=[pltpu.VMEM((K, tn), BF16)],
        compiler_params=_params(("arbitrary", "arbitrary")),
        name=name,
    )(a, wt)


def _mm_residual_kernel(a_ref, w_ref, r_ref, o_ref):
    o_ref[...] = r_ref[...] + jnp.dot(a_ref[...], w_ref[...], preferred_element_type=F32)


def _matmul_residual(a, w, res, *, tm_pref=512, tn_pref=1024):
    M, K = a.shape
    N = w.shape[1]
    tm = _pick(M, tm_pref, 16)
    tn = _pick(N, tn_pref, LANES)
    return pl.pallas_call(
        _mm_residual_kernel,
        grid=(N // tn, M // tm),
        in_specs=[pl.BlockSpec((tm, K), lambda j, i: (i, 0)),
                  pl.BlockSpec((K, tn), lambda j, i: (0, j)),
                  pl.BlockSpec((tm, tn), lambda j, i: (i, j))],
        out_specs=pl.BlockSpec((tm, tn), lambda j, i: (i, j)),
        out_shape=jax.ShapeDtypeStruct((M, N), F32),
        compiler_params=_params(("arbitrary", "arbitrary")),
        name="out_proj_residual",
    )(a, w, res)


def _merge_kernel(g_ref, at_ref, wv_ref, wg_ref, wf_ref, sa_ref, sb_ref, o_ref):
    g = g_ref[...]
    val = jnp.dot(g, wv_ref[...], preferred_element_type=F32)
    gate = jnp.dot(g, wg_ref[...], preferred_element_type=F32)
    y_b = jnp.dot(at_ref[...], wf_ref[...], preferred_element_type=F32)
    y_a = val * _sigmoid_tanh(gate)
    o_ref[...] = (sa_ref[...].astype(F32) * y_a + sb_ref[...].astype(F32) * y_b).astype(o_ref.dtype)


def _merge(g, attn, w_glu, w_fox_out, sig_gates, *, tm_pref=1088, tn_pref=512):
    T, K = g.shape
    D = w_fox_out.shape[1]
    tm = _pick(T, tm_pref, 16)
    tn = _pick(D, tn_pref, LANES)
    nj = D // tn
    return pl.pallas_call(
        _merge_kernel,
        grid=(nj, T // tm),
        in_specs=[pl.BlockSpec((tm, K), lambda j, i: (i, 0)),
                  pl.BlockSpec((tm, K), lambda j, i: (i, 0)),
                  pl.BlockSpec((K, tn), lambda j, i: (0, j)),
                  pl.BlockSpec((K, tn), lambda j, i: (0, j + nj)),
                  pl.BlockSpec((K, tn), lambda j, i: (0, j)),
                  pl.BlockSpec((tm, tn), lambda j, i: (i, j)),
                  pl.BlockSpec((tm, tn), lambda j, i: (i, j + nj))],
        out_specs=pl.BlockSpec((tm, tn), lambda j, i: (i, j)),
        out_shape=jax.ShapeDtypeStruct((T, D), BF16),
        compiler_params=_params(("arbitrary", "arbitrary")),
        name="merge_branches",
    )(g, attn, w_glu, w_glu, w_fox_out, sig_gates, sig_gates)


def _s5_disc_kernel(are_ref, aim_ref, ldt_ref, bre_ref, bim_ref, lre_ref, lim_ref, bbre_ref, bbim_ref):
    lam_re = jnp.minimum(are_ref[...], LAM_RE_MAX)
    lam_im = aim_ref[...]
    dt = jnp.exp(ldt_ref[...])
    mag = jnp.exp(lam_re * dt)
    lb_re = mag * jnp.cos(lam_im * dt)
    lb_im = mag * jnp.sin(lam_im * dt)
    lre_ref[...] = lb_re
    lim_ref[...] = lb_im
    n_re = lb_re - 1.0
    den = lam_re * lam_re + lam_im * lam_im
    co_re = (n_re * lam_re + lb_im * lam_im) / den
    co_im = (lb_im * lam_re - n_re * lam_im) / den
    G, P = co_re.shape
    PH = bre_ref.shape[1]
    hg = PH // P
    rows = lax.broadcasted_iota(jnp.int32, (P, PH), 0)
    cols = lax.broadcasted_iota(jnp.int32, (P, PH), 1)
    expand = jnp.where(cols // hg == rows, 1.0, 0.0).astype(F32)
    ce_re = jnp.dot(co_re, expand, precision=HIGHEST, preferred_element_type=F32)
    ce_im = jnp.dot(co_im, expand, precision=HIGHEST, preferred_element_type=F32)
    b_re = bre_ref[...]
    b_im = bim_ref[...]
    bbre_ref[...] = ce_re * b_re - ce_im * b_im
    bbim_ref[...] = ce_re * b_im + ce_im * b_re


def _s5_discretise(a_re, a_im, log_dt, b_re, b_im):
    G, P = a_re.shape
    hg = b_re.shape[2]
    outs = pl.pallas_call(
        _s5_disc_kernel,
        out_shape=[jax.ShapeDtypeStruct((G, P), F32), jax.ShapeDtypeStruct((G, P), F32),
                   jax.ShapeDtypeStruct((G, P * hg), F32), jax.ShapeDtypeStruct((G, P * hg), F32)],
        name="s5_discretise",
    )(a_re, a_im, log_dt.reshape(G, 1), b_re.reshape(G, P * hg), b_im.reshape(G, P * hg))
    lam_re, lam_im, bb_re, bb_im = outs
    return lam_re, lam_im, bb_re.reshape(G, P, hg), bb_im.reshape(G, P, hg)


def _s5_pack_weights(lam_re, lam_im, bb_re, bb_im, c_re, c_im):
    G, P, hg = bb_re.shape
    nblk = G // SSM_BLOCK_GROUPS
    half_ch = SSM_BLOCK_CH // 2
    pairs_per_half = SSM_PAIRS // 2

    def pair_view(x):
        return x.reshape((nblk, SSM_PAIRS, 2) + x.shape[1:])

    bre = pair_view(bb_re)
    bim = pair_view(bb_im)
    zero = jnp.zeros_like(bre[:, :, 0])

    def rows_of(which):
        re = [bre[:, :, 0], zero] if which == 0 else [zero, bre[:, :, 1]]
        im = [bim[:, :, 0], zero] if which == 0 else [zero, bim[:, :, 1]]
        blk = jnp.concatenate(re + im, axis=2)
        return jnp.swapaxes(blk, 2, 3)

    compact = jnp.concatenate([rows_of(0), rows_of(1)], axis=2)
    local = jnp.arange(SSM_PAIRS) % pairs_per_half
    place = jax.nn.one_hot(local[:, None] * 2 * hg + jnp.arange(2 * hg)[None, :], half_ch, dtype=F32)
    bw = jnp.einsum('prc,bprs->bpcs', place, compact)
    bw_hi = bw.astype(BF16)
    bw_lo = (bw - bw_hi.astype(F32)).astype(BF16)
    bw_hi2 = jnp.concatenate([bw_hi, bw_hi], axis=2)

    cre = pair_view(c_re)
    cim = pair_view(c_im)
    zc = jnp.zeros_like(cre[:, :, 0])

    def cols_of(which):
        re = [cre[:, :, 0], zc] if which == 0 else [zc, cre[:, :, 1]]
        im = [-cim[:, :, 0], zc] if which == 0 else [zc, -cim[:, :, 1]]
        return jnp.concatenate(re + im, axis=3)

    ccompact = jnp.concatenate([cols_of(0), cols_of(1)], axis=2)
    cplace = jax.nn.one_hot(jnp.arange(SSM_PAIRS)[:, None] * 2 * hg + jnp.arange(2 * hg)[None, :],
                            SSM_BLOCK_CH, dtype=F32)
    cw = jnp.einsum('prc,bprs->bpsc', cplace, ccompact).astype(BF16)

    lre = pair_view(lam_re)
    lim = pair_view(lam_im)
    a_re = jnp.concatenate([lre[:, :, 0], lre[:, :, 1]], axis=-1)
    a_im = jnp.concatenate([lim[:, :, 0], lim[:, :, 1]], axis=-1)
    return bw_hi2, bw_lo, cw, jnp.concatenate([a_re, a_im], axis=-1)


def _pack_state(s_re, s_im):
    B, G, P = s_re.shape
    nblk = G // SSM_BLOCK_GROUPS
    re = s_re.reshape(B, nblk, SSM_PAIRS, 2 * P)
    im = s_im.reshape(B, nblk, SSM_PAIRS, 2 * P)
    return jnp.moveaxis(jnp.concatenate([re, im], axis=-1), 1, 0)


def _unpack_state(st):
    nblk, B, pairs, w = st.shape
    st = jnp.moveaxis(st, 0, 1)
    P = w // 4
    re = st[..., :2 * P].reshape(B, nblk * pairs * 2, P)
    im = st[..., 2 * P:].reshape(B, nblk * pairs * 2, P)
    return re, im


def _s5_kernel(*refs, n_u, nseq, lc, rp, ilp):
    u_refs = refs[:n_u]
    bw2_ref, bwlo_ref, cw_ref, lam_ref, d_ref, h0_ref, y_ref, hout_ref, sre_ref, sim_ref, st_ref = refs[n_u:]
    chunk = pl.program_id(1)
    rows = nseq * lc
    half = LANES

    @pl.when(chunk == 0)
    def _():
        st_ref[...] = h0_ref[0]

    lam = lam_ref[0]
    sw = lam.shape[1] // 2
    a_re = lam[:, :sw]
    a_im = lam[:, sw:]

    u = u_refs[0][...] if n_u == 1 else jnp.concatenate([r[...] for r in u_refs], axis=0)
    for hf in range(2):
        uh = u[:, hf * half:(hf + 1) * half]
        hi = uh.astype(BF16)
        lo = (uh - hi.astype(F32)).astype(BF16)
        lhs = jnp.concatenate([hi, lo], axis=1)
        for q in range(SSM_PAIRS // 2):
            p = hf * (SSM_PAIRS // 2) + q
            bu = (jnp.dot(lhs, bw2_ref[0, p], preferred_element_type=F32)
                  + jnp.dot(hi, bwlo_ref[0, p], preferred_element_type=F32))
            sre_ref[pl.ds(p * rp, rows), :] = bu[:, :sw]
            sim_ref[pl.ds(p * rp, rows), :] = bu[:, sw:]

    def group_body(g, carry):
        seqs = [g * ilp + j for j in range(ilp)]

        def step(t, c):
            out = []
            for j, s in enumerate(seqs):
                x_re, x_im = c[2 * j], c[2 * j + 1]
                at = pl.ds(s * lc + t, SSM_PAIRS, stride=rp)
                n_re = a_re * x_re - a_im * x_im + sre_ref[at, :]
                n_im = a_re * x_im + a_im * x_re + sim_ref[at, :]
                sre_ref[at, :] = n_re
                sim_ref[at, :] = n_im
                out += [n_re, n_im]
            return tuple(out)

        init = []
        for s in seqs:
            x0 = st_ref[s]
            init += [x0[:, :sw], x0[:, sw:]]
        fin = lax.fori_loop(0, lc, step, tuple(init), unroll=4)
        for j, s in enumerate(seqs):
            st_ref[s] = jnp.concatenate([fin[2 * j], fin[2 * j + 1]], axis=1)
        return carry

    lax.fori_loop(0, nseq // ilp, group_body, 0)
    hout_ref[0] = st_ref[...]

    y = d_ref[...] * u
    for p in range(SSM_PAIRS):
        at = pl.ds(p * rp, rows)
        states = jnp.concatenate([sre_ref[at, :], sim_ref[at, :]], axis=1).astype(BF16)
        y = y + jnp.dot(states, cw_ref[0, p], preferred_element_type=F32)
    inner = math.sqrt(2.0 / math.pi) * (y + 0.044715 * (y * y * y))
    y = (0.5 * y * (1.0 + jnp.tanh(inner))).astype(y_ref.dtype)
    rows_u = rows // n_u
    for j in range(n_u):
        y_ref[j] = y[j * rows_u:(j + 1) * rows_u]


def _s5_mixer(z, row0, n_u, seq_per_u, seq_len, weights, d_skip, h0, *, lc):
    bw2, bwlo, cw, lam = weights
    nblk = bw2.shape[0]
    nseq = n_u * seq_per_u
    rows_u = seq_per_u * lc
    rows = n_u * rows_u
    nchunk = seq_len // lc
    assert nchunk == 1 or seq_per_u == 1
    rp = rows + SUBLANES
    r0 = row0 // rows_u
    sw4 = lam.shape[-1]
    ilp = 4 if nseq % 4 == 0 else 1
    kern = functools.partial(_s5_kernel, n_u=n_u, nseq=nseq, lc=lc, rp=rp, ilp=ilp)

    def u_spec(j):
        return pl.BlockSpec((rows_u, SSM_BLOCK_CH), lambda b, c: (r0 + j * nchunk + c, b))

    return pl.pallas_call(
        kern,
        grid=(nblk, nchunk),
        in_specs=[u_spec(j) for j in range(n_u)] + [
            pl.BlockSpec((1,) + bw2.shape[1:], lambda b, c: (b, 0, 0, 0)),
            pl.BlockSpec((1,) + bwlo.shape[1:], lambda b, c: (b, 0, 0, 0)),
            pl.BlockSpec((1,) + cw.shape[1:], lambda b, c: (b, 0, 0, 0)),
            pl.BlockSpec((1, SSM_PAIRS, sw4), lambda b, c: (b, 0, 0)),
            pl.BlockSpec((1, SSM_BLOCK_CH), lambda b, c: (0, b)),
            pl.BlockSpec((1, nseq, SSM_PAIRS, sw4), lambda b, c: (b, 0, 0, 0))],
        out_specs=[pl.BlockSpec((n_u, rows_u, SSM_BLOCK_CH), lambda b, c: (0, c, b)),
                   pl.BlockSpec((1, nseq, SSM_PAIRS, sw4), lambda b, c: (b, 0, 0, 0))],
        out_shape=[jax.ShapeDtypeStruct((n_u, nchunk * rows_u, nblk * SSM_BLOCK_CH), BF16),
                   jax.ShapeDtypeStruct((nblk, nseq, SSM_PAIRS, sw4), F32)],
        scratch_shapes=[pltpu.VMEM((SSM_PAIRS * rp, sw4 // 2), F32),
                        pltpu.VMEM((SSM_PAIRS * rp, sw4 // 2), F32),
                        pltpu.VMEM((nseq, SSM_PAIRS, sw4), F32)],
        compiler_params=_params(("arbitrary", "arbitrary")),
        name="s5_mixer",
    )(*([z] * n_u), bw2, bwlo, cw, lam, d_skip.reshape(1, -1), h0)


def _cumsum_kernel(x_ref, o_ref, *, chunk):
    H, L = x_ref.shape[1], x_ref.shape[2]
    carry = jnp.zeros((H, 1), F32)
    for s in range(0, L, chunk):
        w = min(chunk, L - s)
        r = lax.broadcasted_iota(jnp.int32, (w, w), 0)
        c = lax.broadcasted_iota(jnp.int32, (w, w), 1)
        tri = jnp.where(r <= c, 1.0, 0.0).astype(F32)
        part = jnp.dot(x_ref[0, :, s:s + w], tri, precision=HIGHEST, preferred_element_type=F32) + carry
        o_ref[0, :, s:s + w] = part
        carry = part[:, w - 1:w]


def _cumsum_last(x):
    B, H, L = x.shape
    return pl.pallas_call(
        functools.partial(_cumsum_kernel, chunk=256),
        grid=(B,),
        in_specs=[pl.BlockSpec((1, H, L), lambda b: (b, 0, 0))],
        out_specs=pl.BlockSpec((1, H, L), lambda b: (b, 0, 0)),
        out_shape=jax.ShapeDtypeStruct((B, H, L), F32),
        compiler_params=_params(("arbitrary",)),
        name="logf_cumsum",
    )(x)


def _pick_head(c_ref, h):
    blk = c_ref[...]
    lane = lax.broadcasted_iota(jnp.int32, blk.shape, 1)
    return jnp.sum(jnp.where(lane == h, blk, 0.0), axis=1, keepdims=True)


def _nt_dot(a, b):
    return lax.dot_general(a, b, (((1,), (1,)), ((), ())), preferred_element_type=F32)


def _fox_prompt_kernel(q_ref, k_ref, v_ref, cq_ref, ck_ref, o_ref, *, tq, nq, scale):
    h = pl.program_id(1)
    qi = pl.program_id(2)
    q = (q_ref[...] * (scale * LOG2_E)).astype(BF16)
    cq = _pick_head(cq_ref, h) * LOG2_E

    def block(ki, carry, diagonal):
        m_prev, l_prev, acc = carry
        at = pl.ds(pl.multiple_of(ki * tq, tq), tq)
        s = _nt_dot(q, k_ref[at, :].astype(BF16)) - ck_ref[0, pl.ds(h * nq + ki, 1), :] * LOG2_E
        if diagonal:
            r = lax.broadcasted_iota(jnp.int32, (tq, tq), 0)
            c = lax.broadcasted_iota(jnp.int32, (tq, tq), 1)
            s = jnp.where(c <= r, s, MASK_VALUE)
        m_new = jnp.maximum(m_prev, jnp.max(s, axis=1, keepdims=True) + cq)
        alpha = jnp.exp2(m_prev - m_new)
        p = jnp.exp2(s + (cq - m_new))
        l_new = alpha * l_prev + jnp.sum(p, axis=1, keepdims=True)
        acc = alpha * acc + jnp.dot(p.astype(BF16), v_ref[at, :].astype(BF16), preferred_element_type=F32)
        return m_new, l_new, acc

    init = (jnp.full((tq, 1), MASK_VALUE, F32), jnp.zeros((tq, 1), F32), jnp.zeros(o_ref.shape, F32))
    carry = lax.fori_loop(0, qi, lambda ki, c: block(ki, c, False), init)
    _, l, acc = block(qi, carry, True)
    o_ref[...] = (acc / l).astype(o_ref.dtype)


def _fox_prompt(z, c_col, c_row, B, L, H, col_q, col_k, col_v, scale):
    dh = FOX_HEAD_DIM
    tq = _pick(L, 512)
    nq = L // tq
    return pl.pallas_call(
        functools.partial(_fox_prompt_kernel, tq=tq, nq=nq, scale=scale),
        grid=(B, H, nq),
        in_specs=[pl.BlockSpec((tq, dh), lambda b, h, qi: (b * nq + qi, col_q + h)),
                  pl.BlockSpec((L, dh), lambda b, h, qi: (b, col_k + h)),
                  pl.BlockSpec((L, dh), lambda b, h, qi: (b, col_v + h)),
                  pl.BlockSpec((tq, H), lambda b, h, qi: (b * nq + qi, 0)),
                  pl.BlockSpec((1, H * nq, tq), lambda b, h, qi: (b, 0, 0))],
        out_specs=pl.BlockSpec((tq, dh), lambda b, h, qi: (b * nq + qi, h)),
        out_shape=jax.ShapeDtypeStruct((B * L, H * dh), BF16),
        compiler_params=_params(("arbitrary",) * 3),
        name="fox_prompt",
    )(z, z, z, c_col, c_row.reshape(B, H * nq, tq))


def _fox_sample_kernel(q_ref, kn_ref, vn_ref, kc_ref, vc_ref, cq_ref, ck_ref, o_ref, *, past, n_heads, scale):
    dh = FOX_HEAD_DIM
    n = q_ref.shape[0]
    r = lax.broadcasted_iota(jnp.int32, (n, n), 0)
    c = lax.broadcasted_iota(jnp.int32, (n, n), 1)
    causal = c <= r
    cq_all = cq_ref[...]
    outs = []
    for h in range(n_heads):
        cols = slice(h * dh, (h + 1) * dh)
        q = (q_ref[:, cols] * scale).astype(BF16)
        head_rows = pl.ds(h, past, stride=n_heads)
        c_all = ck_ref[0, h:h + 1, :]
        cq = cq_all[:, h:h + 1]
        s_old = _nt_dot(q, kc_ref[0, head_rows, :].astype(BF16)) + cq - c_all[:, :past]
        s_new = _nt_dot(q, kn_ref[:, cols].astype(BF16)) + cq - c_all[:, past:]
        s_new = jnp.where(causal, s_new, MASK_VALUE)
        m = jnp.maximum(jnp.max(s_old, axis=1, keepdims=True), jnp.max(s_new, axis=1, keepdims=True))
        p_old = jnp.exp(s_old - m)
        p_new = jnp.exp(s_new - m)
        l = jnp.sum(p_old, axis=1, keepdims=True) + jnp.sum(p_new, axis=1, keepdims=True)
        acc = (jnp.dot(p_old.astype(BF16), vc_ref[0, head_rows, :].astype(BF16), preferred_element_type=F32)
               + jnp.dot(p_new.astype(BF16), vn_ref[:, cols].astype(BF16), preferred_element_type=F32))
        outs.append((acc / l).astype(o_ref.dtype))
    o_ref[...] = jnp.concatenate(outs, axis=1)


def _fox_sample(z, row0, cache_k, cache_v, c_col, c_row, B, n, H, col_q, col_k, col_v, scale):
    dh = FOX_HEAD_DIM
    past = cache_k.shape[1] // H
    r0 = row0 // n
    width = H * dh
    assert (col_q * dh) % width == 0 and (col_k * dh) % width == 0 and (col_v * dh) % width == 0
    return pl.pallas_call(
        functools.partial(_fox_sample_kernel, past=past, n_heads=H, scale=scale),
        grid=(B,),
        in_specs=[pl.BlockSpec((n, width), lambda b: (r0 + b, col_q * dh // width)),
                  pl.BlockSpec((n, width), lambda b: (r0 + b, col_k * dh // width)),
                  pl.BlockSpec((n, width), lambda b: (r0 + b, col_v * dh // width)),
                  pl.BlockSpec((1, past * H, dh), lambda b: (b, 0, 0)),
                  pl.BlockSpec((1, past * H, dh), lambda b: (b, 0, 0)),
                  pl.BlockSpec((n, H), lambda b: (b, 0)),
                  pl.BlockSpec((1, H, past + n), lambda b: (b, 0, 0))],
        out_specs=pl.BlockSpec((n, width), lambda b: (b, 0)),
        out_shape=jax.ShapeDtypeStruct((B * n, width), BF16),
        compiler_params=_params(("arbitrary",)),
        name="fox_sample",
    )(z, z, z, cache_k, cache_v, c_col, c_row)


def _gather_rows_kernel(tok_ref, used_ref, h_hbm, o_ref, buf, sem, *, tm):
    i = pl.program_id(0)
    n = used_ref[0]

    def row_copy(blk, slot, r):
        t = tok_ref[blk * tm + r]
        return pltpu.make_async_copy(h_hbm.at[pl.ds(t, 1), :], buf.at[slot, pl.ds(r, 1), :], sem.at[slot])

    def issue(blk, slot):
        def body(r, c):
            row_copy(blk, slot, r).start()
            return c
        lax.fori_loop(0, tm, body, 0, unroll=8)

    def drain(blk, slot):
        del blk
        pltpu.make_async_copy(h_hbm.at[pl.ds(0, tm), :], buf.at[slot], sem.at[slot]).wait()

    @pl.when(jnp.logical_and(i == 0, n > 0))
    def _():
        issue(0, 0)

    @pl.when(i + 1 < n)
    def _():
        issue(i + 1, (i + 1) % 2)

    @pl.when(i < n)
    def _():
        slot = i % 2
        drain(i, slot)
        words = buf[slot]
        lo = lax.bitcast_convert_type(words << 16, F32)
        hi = lax.bitcast_convert_type(words & jnp.uint32(0xFFFF0000), F32)
        o_ref[...] = jnp.concatenate([lo, hi], axis=1).astype(o_ref.dtype)

    @pl.when(i >= n)
    def _():
        o_ref[...] = jnp.zeros(o_ref.shape, o_ref.dtype)


def _gather_rows(h_packed, slot_tok, n_used, tm):
    h = h_packed
    n_slots = slot_tok.shape[0]
    D = 2 * h.shape[1]
    return pl.pallas_call(
        functools.partial(_gather_rows_kernel, tm=tm),
        grid_spec=pltpu.PrefetchScalarGridSpec(
            num_scalar_prefetch=2,
            grid=(n_slots // tm,),
            in_specs=[pl.BlockSpec(memory_space=pl.ANY)],
            out_specs=pl.BlockSpec((tm, D), lambda i, tok, used: (i, 0)),
            scratch_shapes=[pltpu.VMEM((2, tm, D // 2), h.dtype), pltpu.SemaphoreType.DMA((2,))]),
        out_shape=jax.ShapeDtypeStruct((n_slots, D), BF16),
        compiler_params=_params(("arbitrary",)),
        name="moe_dispatch_gather",
    )(slot_tok, n_used, h)


def _grouped_kernel(blk0_ref, nblk_ref, x_hbm, *refs, n_w, tm, tn):
    w_refs = refs[:n_w]
    o_hbm = refs[n_w]
    w_bf = refs[n_w + 1:2 * n_w + 1]
    xbuf, obuf, tbuf, primed_ref, in_sem, out_sem, tail_sem = refs[2 * n_w + 1:]
    col = pl.program_id(0)
    e = pl.program_id(1)
    n_col = pl.num_programs(0)
    n_exp = pl.num_programs(1)
    nblk = nblk_ref[e]
    blk0 = blk0_ref[e]
    npair = nblk // 2

    def block_copy(first, j, slot):
        return pltpu.make_async_copy(x_hbm.at[pl.ds((first + j) * tm, tm), :], xbuf.at[slot], in_sem.at[slot])

    def x_copy(j, slot):
        return block_copy(blk0, j, slot)

    def pair_base(p):
        return jnp.where(p == 0, PRIME_SLOT, (p % 2) * 2)

    def prime(first, count):
        block_copy(first, 0, PRIME_SLOT).start()

        @pl.when(count > 1)
        def _():
            block_copy(first, 1, PRIME_SLOT + 1).start()

    @pl.when(jnp.logical_and(col == 0, e == 0))
    def _():
        primed_ref[0] = 0

    has_next = jnp.logical_or(e + 1 < n_exp, col + 1 < n_col)
    e_next = jnp.where(e + 1 < n_exp, e + 1, 0)
    nblk_next = nblk_ref[e_next]
    prime_next = jnp.logical_and(has_next, nblk_next > 0)

    def request_next():
        @pl.when(prime_next)
        def _():
            prime(blk0_ref[e_next], nblk_next)

    def pair_out(p, slot):
        return pltpu.make_async_copy(obuf.at[slot],
                                     o_hbm.at[pl.ds((blk0 + 2 * p) * tm, 2 * tm), pl.ds(col * tn, tn)],
                                     out_sem.at[slot])

    def tail_out():
        return pltpu.make_async_copy(tbuf, o_hbm.at[pl.ds((blk0 + nblk - 1) * tm, tm), pl.ds(col * tn, tn)],
                                     tail_sem)

    def apply(x):
        if n_w == 2:
            a = jnp.dot(x, w_bf[0][...], preferred_element_type=F32)
            b = jnp.dot(x, w_bf[1][...], preferred_element_type=F32)
            return a * _sigmoid_tanh(a) * b
        return jnp.dot(x, w_bf[0][...], preferred_element_type=F32)

    @pl.when(nblk > 0)
    def _():
        @pl.when(primed_ref[0] == 0)
        def _():
            prime(blk0, nblk)

        for w_ref, dst in zip(w_refs, w_bf):
            dst[...] = w_ref[0].astype(BF16)

        def pair_body(p, carry):
            s0 = pair_base(p)
            s_next = ((p + 1) % 2) * 2

            @pl.when(2 * p + 2 < nblk)
            def _():
                x_copy(2 * p + 2, s_next).start(priority=1)

            @pl.when(2 * p + 3 < nblk)
            def _():
                x_copy(2 * p + 3, s_next + 1).start(priority=1)

            x_copy(2 * p, s0).wait()
            x_copy(2 * p + 1, s0 + 1).wait()
            y = apply(xbuf[pl.ds(s0, 2)].reshape(2 * tm, xbuf.shape[2]))
            oslot = p % 2

            @pl.when(p >= 2)
            def _():
                pair_out(p - 2, oslot).wait()

            obuf[oslot] = y.astype(obuf.dtype)
            pair_out(p, oslot).start()

            @pl.when(p == 0)
            def _():
                request_next()

            return carry

        lax.fori_loop(0, npair, pair_body, 0)

        @pl.when(nblk % 2 == 1)
        def _():
            slot = pair_base(npair)
            x_copy(nblk - 1, slot).wait()
            tbuf[...] = apply(xbuf[slot]).astype(tbuf.dtype)
            tail_out().start()

            @pl.when(npair == 0)
            def _():
                request_next()

        @pl.when(npair >= 2)
        def _():
            pair_out(npair - 2, npair % 2).wait()

        @pl.when(npair >= 1)
        def _():
            pair_out(npair - 1, (npair - 1) % 2).wait()

        @pl.when(nblk % 2 == 1)
        def _():
            tail_out().wait()

    @pl.when(nblk == 0)
    def _():
        request_next()

    primed_ref[0] = prime_next.astype(jnp.int32)

    @pl.when(e == n_exp - 1)
    def _():
        first = blk0 + nblk
        total = o_hbm.shape[0] // tm

        def zero_copy(j):
            return pltpu.make_async_copy(tbuf, o_hbm.at[pl.ds(j * tm, tm), pl.ds(col * tn, tn)], tail_sem)

        @pl.when(first < total)
        def _():
            tbuf[...] = jnp.zeros(tbuf.shape, tbuf.dtype)

        def start(j, c):
            zero_copy(j).start()
            return c

        def wait(j, c):
            zero_copy(j).wait()
            return c

        lax.fori_loop(first, total, start, 0)
        lax.fori_loop(first, total, wait, 0)


def _grouped_matmul(x, weights, blk0, nblk, tm, tn_pref, out_dtype, name):
    n_rows, K = x.shape
    E, _, N = weights[0].shape
    n_w = len(weights)
    tn = _pick(N, tn_pref, LANES)
    return pl.pallas_call(
        functools.partial(_grouped_kernel, n_w=n_w, tm=tm, tn=tn),
        grid_spec=pltpu.PrefetchScalarGridSpec(
            num_scalar_prefetch=2,
            grid=(N // tn, E),
            in_specs=[pl.BlockSpec(memory_space=pl.ANY)]
            + [pl.BlockSpec((1, K, tn), lambda c, e, b0, nb: (e, 0, c)) for _ in weights],
            out_specs=pl.BlockSpec(memory_space=pl.ANY),
            scratch_shapes=[pltpu.VMEM((K, tn), BF16) for _ in weights] + [
                pltpu.VMEM((PRIME_SLOT + 2, tm, K), x.dtype),
                pltpu.VMEM((2, 2 * tm, tn), out_dtype),
                pltpu.VMEM((tm, tn), out_dtype),
                pltpu.SMEM((1,), jnp.int32),
                pltpu.SemaphoreType.DMA((PRIME_SLOT + 2,)),
                pltpu.SemaphoreType.DMA((2,)),
                pltpu.SemaphoreType.DMA(())]),
        out_shape=jax.ShapeDtypeStruct((n_rows, N), out_dtype),
        compiler_params=_params(("arbitrary", "arbitrary")),
        name=name,
    )(blk0, nblk, x, *weights)


def _combine_kernel(dest_ref, x_ref, sh_ref, gate_ref, g_ref, rows_hbm, o_ref, buf, sem, *, tc, final_norm,
                    row0):
    i = pl.program_id(0)
    n = pl.num_programs(0)

    def row_copy(blk, slot, r, k):
        src = dest_ref[(row0 + blk * tc + r) * TOP_K + k]
        return pltpu.make_async_copy(rows_hbm.at[pl.ds(src, 1), :], buf.at[slot, k, pl.ds(r, 1), :],
                                     sem.at[slot])

    def issue(blk, slot):
        def body(r, c):
            for k in range(TOP_K):
                row_copy(blk, slot, r, k).start()
            return c
        lax.fori_loop(0, tc, body, 0)

    def drain(blk, slot):
        del blk
        for k in range(TOP_K):
            pltpu.make_async_copy(rows_hbm.at[pl.ds(0, tc), :], buf.at[slot, k], sem.at[slot]).wait()

    @pl.when(i == 0)
    def _():
        issue(0, 0)

    @pl.when(i + 1 < n)
    def _():
        issue(i + 1, (i + 1) % 2)

    slot = i % 2
    drain(i, slot)
    y = x_ref[...] + sh_ref[...]
    gates = gate_ref[...]
    for k in range(TOP_K):
        y = y + gates[:, k:k + 1] * buf[slot, k]
    o_ref[...] = _rms(y, g_ref[...]) if final_norm else y


def _combine(x, shared, gates, dest, rows, g_final, final_norm, row0, n_rows, tc=32):
    D = x.shape[1]
    tc = _pick(math.gcd(row0, n_rows) if row0 else n_rows, tc)
    b0 = row0 // tc
    return pl.pallas_call(
        functools.partial(_combine_kernel, tc=tc, final_norm=final_norm, row0=row0),
        grid_spec=pltpu.PrefetchScalarGridSpec(
            num_scalar_prefetch=1,
            grid=(n_rows // tc,),
            in_specs=[pl.BlockSpec((tc, D), lambda i, d: (b0 + i, 0)),
                      pl.BlockSpec((tc, D), lambda i, d: (b0 + i, 0)),
                      pl.BlockSpec((tc, TOP_K), lambda i, d: (b0 + i, 0)),
                      pl.BlockSpec((1, D), lambda i, d: (0, 0)),
                      pl.BlockSpec(memory_space=pl.ANY)],
            out_specs=pl.BlockSpec((tc, D), lambda i, d: (i, 0)),
            scratch_shapes=[pltpu.VMEM((2, TOP_K, tc, D), F32), pltpu.SemaphoreType.DMA((2,))]),
        out_shape=jax.ShapeDtypeStruct((n_rows, D), F32),
        compiler_params=_params(("arbitrary",)),
        name="moe_combine_norm",
    )(dest, x, shared, gates, g_final.reshape(1, D), rows)


def _slot_assign_kernel(start_ref, idx_ref, rank_ref, dest_ref, *, n_experts):
    idx = idx_ref[...]
    base = jnp.zeros(idx.shape, jnp.int32)
    for e in range(n_experts):
        base = jnp.where(idx == e, start_ref[e], base)
    dest_ref[...] = base + rank_ref[...]


def _routing_tables(idx, rank, counts, n_experts, tm):
    T = idx.shape[0]
    n_assign = T * TOP_K
    n_blocks = (n_assign + n_experts * (tm - 1) + tm - 1) // tm
    counts = counts.reshape(n_experts).astype(jnp.int32)
    padded = (counts + tm - 1) // tm * tm
    pad_end = jnp.cumsum(padded)
    pad_start = (pad_end - padded).astype(jnp.int32)
    shape2d = (n_assign // LANES, LANES) if n_assign % LANES == 0 else (T, TOP_K)
    dest = pl.pallas_call(
        functools.partial(_slot_assign_kernel, n_experts=n_experts),
        grid_spec=pltpu.PrefetchScalarGridSpec(
            num_scalar_prefetch=1, grid=(1,),
            in_specs=[pl.BlockSpec(shape2d, lambda i, s: (0, 0)), pl.BlockSpec(shape2d, lambda i, s: (0, 0))],
            out_specs=pl.BlockSpec(shape2d, lambda i, s: (0, 0))),
        out_shape=jax.ShapeDtypeStruct(shape2d, jnp.int32),
        name="moe_slot_assign",
    )(pad_start, idx.reshape(shape2d), rank.reshape(shape2d)).reshape(n_assign)
    slot_tok = jnp.zeros((n_blocks * tm,), jnp.int32).at[dest].set(
        jnp.arange(n_assign, dtype=jnp.int32) // TOP_K)
    return dest, slot_tok, (pad_start // tm).astype(jnp.int32), (padded // tm).astype(jnp.int32)


def kernel(x_prompt, x_sample, cache_k, cache_v, cache_logf, state_ssm_re, state_ssm_im, norm_mix, w_in, b_forget, ssm_a_re, ssm_a_im, ssm_log_dt, ssm_b_re, ssm_b_im, ssm_c_re, ssm_c_im, ssm_d, w_glu, w_fox_out, w_out, norm_ffn, w_router, router_bias, w_exp_gate, w_exp_up, w_exp_down, w_sh_gate, w_sh_up, w_sh_down, norm_final):
    Bp, Lp, D = x_prompt.shape
    Bs, Ls, _ = x_sample.shape
    depth = w_in.shape[0]
    past = cache_k.shape[2]
    H = cache_k.shape[3]
    dh = cache_k.shape[4]
    assert dh == FOX_HEAD_DIM
    d_fox = H * dh
    d_ssm = ssm_d.shape[1]
    n_experts = w_router.shape[2]
    Tp = Bp * Lp
    Ts = Bs * Ls
    T = Tp + Ts
    scale = dh ** -0.5
    n_uqkv = d_ssm + 3 * d_fox
    col_q = d_ssm // dh
    col_k = col_q + H
    col_v = col_k + H
    moe_tm = 256

    x = jnp.concatenate([x_prompt.reshape(Tp, D), x_sample.reshape(Ts, D)], axis=0)
    outs = {name: [] for name in ("kp", "vp", "fp", "rp", "ip", "ks", "vs", "fs", "rs", "is")}
    for l in range(depth):
        wt_in = jnp.swapaxes(w_in[l], 0, 1)
        wt_f = wt_in[n_uqkv:n_uqkv + H]

        h, logf = _norm_forget(x, norm_mix[l], wt_f, b_forget[l])
        z = _matmul_wt(h, wt_in, 0, n_uqkv, F32, name="in_proj")
        sig_gates = _matmul_wt(h, wt_in, n_uqkv + H, 2 * D, BF16, sigmoid=True, name="gate_proj")

        lam_re, lam_im, bb_re, bb_im = _s5_discretise(ssm_a_re[l], ssm_a_im[l], ssm_log_dt[l],
                                                      ssm_b_re[l], ssm_b_im[l])
        s5_w = _s5_pack_weights(lam_re, lam_im, bb_re, bb_im, ssm_c_re[l], ssm_c_im[l])
        zero_state = jnp.zeros((Bp,) + state_ssm_re.shape[2:], F32)
        g_p, st_p = _s5_mixer(z, 0, Bp, 1, Lp, s5_w, ssm_d[l], _pack_state(zero_state, zero_state),
                              lc=_pick(Lp, 256))
        g_s, st_s = _s5_mixer(z, Tp, 1, Bs, Ls, s5_w, ssm_d[l],
                              _pack_state(state_ssm_re[l].astype(F32), state_ssm_im[l].astype(F32)), lc=Ls)
        g = jnp.concatenate([g_p.reshape(Tp, d_ssm), g_s.reshape(Ts, d_ssm)], axis=0)

        logf_p = logf[:Tp].reshape(Bp, Lp, H)
        logf_s = logf[Tp:].reshape(Bs, Ls, H)
        crow_p = _cumsum_last(jnp.swapaxes(logf_p, 1, 2))
        ccol_p = jnp.swapaxes(crow_p, 1, 2).reshape(Tp, H)
        attn_p = _fox_prompt(z, ccol_p, crow_p, Bp, Lp, H, col_q, col_k, col_v, scale)
        lf_all = jnp.concatenate([cache_logf[l].astype(F32), logf_s], axis=1)
        crow_s = _cumsum_last(jnp.swapaxes(lf_all, 1, 2))
        ccol_s = jnp.swapaxes(crow_s[:, :, past:], 1, 2).reshape(Ts, H)
        attn_s = _fox_sample(z, Tp, cache_k[l].reshape(Bs, past * H, dh), cache_v[l].reshape(Bs, past * H, dh),
                             ccol_s, crow_s, Bs, Ls, H, col_q, col_k, col_v, scale)
        attn = jnp.concatenate([attn_p, attn_s], axis=0)

        merged = _merge(g, attn, w_glu[l].astype(BF16), w_fox_out[l].astype(BF16), sig_gates)
        x = _matmul_residual(merged, w_out[l].astype(BF16), x)

        norm_out = norm_ffn[l]
        h2, h2p, idx, gates, rank, counts = _norm_router(x, norm_out, w_router[l], router_bias[l])
        dest, slot_tok, e_blk0, e_nblk = _routing_tables(idx, rank, counts, n_experts, moe_tm)
        xs = _gather_rows(h2p, slot_tok, (e_blk0[-1:] + e_nblk[-1:]), moe_tm)
        act = _grouped_matmul(xs, (w_exp_gate[l], w_exp_up[l]), e_blk0, e_nblk, moe_tm, 512, BF16, "expert_up")
        routed = _grouped_matmul(act, (w_exp_down[l],), e_blk0, e_nblk, moe_tm, 2048, F32, "expert_down")
        sh_tm = _pick(T, moe_tm, 16)
        sh_blk0 = jnp.zeros((1,), jnp.int32)
        sh_nblk = jnp.full((1,), T // sh_tm, jnp.int32)
        sh_act = _grouped_matmul(h2, (w_sh_gate[l][None], w_sh_up[l][None]), sh_blk0, sh_nblk, sh_tm, 512, BF16,
                                 "shared_up")
        shared = _grouped_matmul(sh_act, (w_sh_down[l][None],), sh_blk0, sh_nblk, sh_tm, 2048, F32,
                                 "shared_down")
        if l == depth - 1:
            y_p = _combine(x, shared, gates, dest, routed, norm_final, True, 0, Tp)
            y_s = _combine(x, shared, gates, dest, routed, norm_final, True, Tp, Ts)
        else:
            x = _combine(x, shared, gates, dest, routed, norm_final, False, 0, T)

        k_all = z[:, d_ssm + d_fox:d_ssm + 2 * d_fox]
        v_all = z[:, d_ssm + 2 * d_fox:d_ssm + 3 * d_fox]
        re_p, im_p = _unpack_state(st_p)
        re_s, im_s = _unpack_state(st_s)
        outs["kp"].append(k_all[:Tp].reshape(Bp, Lp, H, dh))
        outs["vp"].append(v_all[:Tp].reshape(Bp, Lp, H, dh))
        outs["fp"].append(logf_p)
        outs["rp"].append(re_p)
        outs["ip"].append(im_p)
        outs["ks"].append(k_all[Tp:].reshape(Bs, Ls, H, dh))
        outs["vs"].append(v_all[Tp:].reshape(Bs, Ls, H, dh))
        outs["fs"].append(logf_s)
        outs["rs"].append(re_s)
        outs["is"].append(im_s)

    return (y_p.reshape(Bp, Lp, D), y_s.reshape(Bs, Ls, D),
            jnp.stack(outs["kp"]), jnp.stack(outs["vp"]), jnp.stack(outs["fp"]),
            jnp.stack(outs["rp"]), jnp.stack(outs["ip"]),
            jnp.stack(outs["ks"]), jnp.stack(outs["vs"]), jnp.stack(outs["fs"]),
            jnp.stack(outs["rs"]), jnp.stack(outs["is"]))
```

```python
import functools
import math

import jax
import jax.numpy as jnp
from jax import lax
from jax.experimental import pallas as pl
from jax.experimental.pallas import tpu as pltpu

F32 = jnp.float32
BF16 = jnp.bfloat16
HIGHEST = lax.Precision.HIGHEST

SSM_GROUP = 16
FOX_HEAD_DIM = 128
TOP_K = 8
ROUTED_SCALE = 2.5
NORM_EPS = 1e-6
LAM_RE_MAX = -1e-4

V7X_VMEM_BYTES = 64 * 1024 * 1024
VMEM_LIMIT = V7X_VMEM_BYTES - 6 * 1024 * 1024
LANES = 128
SUBLANES = 8
SSM_BLOCK_GROUPS = 16
SSM_BLOCK_CH = SSM_BLOCK_GROUPS * SSM_GROUP
SSM_PAIRS = SSM_BLOCK_GROUPS // 2
MASK_VALUE = -1e30
LOG2_E = 1.4426950408889634
PRIME_SLOT = 4


def _pick(n, pref, mult=SUBLANES):
    best = None
    d = mult
    while d <= min(n, pref):
        if n % d == 0:
            best = d
        d += mult
    return best if best is not None else n


def _params(sem, vmem=VMEM_LIMIT):
    return pltpu.CompilerParams(dimension_semantics=sem, vmem_limit_bytes=vmem)


def _sigmoid(x):
    return 1.0 / (1.0 + jnp.exp(-x))


def _sigmoid_tanh(x):
    return 0.5 * jnp.tanh(0.5 * x) + 0.5


def _rms(x, g):
    r = lax.rsqrt(jnp.mean(x * x, axis=-1, keepdims=True) + NORM_EPS)
    return x * r * g


def _dot_split(a, b, dims):
    def split(x):
        hi = x.astype(BF16)
        return hi, (x - hi.astype(F32)).astype(BF16)

    def dot(x, y):
        return lax.dot_general(x, y, dims, preferred_element_type=F32)

    a_hi, a_lo = split(a)
    b_hi, b_lo = split(b)
    return dot(a_hi, b_hi) + dot(a_lo, b_hi) + dot(a_hi, b_lo)


def _stacked_rows(parts, tm, width, row_axis, col_axis=None):
    def col(idx):
        return 0 if col_axis is None else idx[col_axis]

    if len(parts) == 1:
        return [pl.BlockSpec((tm, width), lambda *idx: (idx[row_axis], col(idx)))], lambda refs, i: refs[0][...]
    n_first = parts[0].shape[0] // tm
    specs = [pl.BlockSpec((tm, width), lambda *idx: (jnp.minimum(idx[row_axis], n_first - 1), col(idx))),
             pl.BlockSpec((tm, width), lambda *idx: (jnp.maximum(idx[row_axis] - n_first, 0), col(idx)))]
    return specs, lambda refs, i: jnp.where(i < n_first, refs[0][...], refs[1][...])


def _norm_forget_kernel(*refs, n_x, read_x):
    x_refs = refs[:n_x]
    g_ref, w_ref, b_ref, h_ref, logf_ref = refs[n_x:]
    h = _rms(read_x(x_refs, pl.program_id(0)), g_ref[...])
    h_ref[...] = h.astype(h_ref.dtype)
    z = _dot_split(h, w_ref[...], (((1,), (1,)), ((), ()))) + b_ref[...]
    logf_ref[...] = jnp.minimum(z, 0.0) - jnp.log1p(jnp.exp(-jnp.abs(z)))


def _norm_forget(x_parts, g, wt_f, b_f):
    D = x_parts[0].shape[1]
    T = sum(p.shape[0] for p in x_parts)
    H = wt_f.shape[0]
    tm = _pick(math.gcd(*[p.shape[0] for p in x_parts]), 256)
    x_specs, read_x = _stacked_rows(x_parts, tm, D, 0)
    return pl.pallas_call(
        functools.partial(_norm_forget_kernel, n_x=len(x_parts), read_x=read_x),
        grid=(T // tm,),
        in_specs=x_specs + [pl.BlockSpec((1, D), lambda i: (0, 0)),
                            pl.BlockSpec((H, D), lambda i: (0, 0)),
                            pl.BlockSpec((1, H), lambda i: (0, 0))],
        out_specs=[pl.BlockSpec((tm, D), lambda i: (i, 0)),
                   pl.BlockSpec((tm, H), lambda i: (i, 0))],
        out_shape=[jax.ShapeDtypeStruct((T, D), BF16), jax.ShapeDtypeStruct((T, H), F32)],
        compiler_params=_params(("arbitrary",)),
        name="norm_forget",
    )(*x_parts, g.reshape(1, D), wt_f, b_f.reshape(1, H))


def _norm_router_kernel(x_ref, g_ref, w_ref, b_ref, h_ref, hp_ref, idx_ref, gate_ref, rank_ref, cnt_ref):
    @pl.when(pl.program_id(0) == 0)
    def _():
        cnt_ref[...] = jnp.zeros(cnt_ref.shape, F32)

    h = _rms(x_ref[...], g_ref[...])
    hb = h.astype(BF16)
    h_ref[...] = hb
    bits = lax.bitcast_convert_type(hb.astype(F32), jnp.uint32)
    half = bits.shape[1] // 2
    hp_ref[...] = (bits[:, half:] & jnp.uint32(0xFFFF0000)) | (bits[:, :half] >> 16)
    scores = _sigmoid(_dot_split(h, w_ref[...], (((1,), (0,)), ((), ()))))
    tm, E = scores.shape
    lane = lax.broadcasted_iota(jnp.int32, (tm, E), 1).astype(F32)
    col = lax.broadcasted_iota(jnp.int32, (tm, TOP_K), 1)
    work = scores + b_ref[...]
    idx = jnp.zeros((tm, TOP_K), F32)
    sel = jnp.zeros((tm, TOP_K), F32)
    chosen = jnp.zeros((tm, E), F32)
    hits = []
    for k in range(TOP_K):
        m = jnp.max(work, axis=1, keepdims=True)
        am = jnp.min(jnp.where(work == m, lane, float(E)), axis=1, keepdims=True)
        hit = lane == am
        hits.append(hit)
        sk = jnp.sum(jnp.where(hit, scores, 0.0), axis=1, keepdims=True)
        idx = jnp.where(col == k, am, idx)
        sel = jnp.where(col == k, sk, sel)
        chosen = jnp.where(hit, 1.0, chosen)
        work = jnp.where(hit, -jnp.inf, work)
    idx_ref[...] = idx.astype(jnp.int32)
    gate_ref[...] = ROUTED_SCALE * sel / jnp.sum(sel, axis=1, keepdims=True)

    r = lax.broadcasted_iota(jnp.int32, (tm, tm), 0)
    c = lax.broadcasted_iota(jnp.int32, (tm, tm), 1)
    lower = jnp.where(c < r, 1.0, 0.0).astype(BF16)
    before = jnp.dot(lower, chosen.astype(BF16), preferred_element_type=F32) + cnt_ref[...]
    rank = jnp.zeros((tm, TOP_K), F32)
    for k in range(TOP_K):
        rank = jnp.where(col == k, jnp.sum(jnp.where(hits[k], before, 0.0), axis=1, keepdims=True), rank)
    rank_ref[...] = rank.astype(jnp.int32)
    cnt_ref[...] = cnt_ref[...] + jnp.sum(chosen, axis=0, keepdims=True)


def _norm_router(x, g, w_router, router_bias):
    T, D = x.shape
    E = w_router.shape[1]
    tm = _pick(T, 256)
    return pl.pallas_call(
        _norm_router_kernel,
        grid=(T // tm,),
        in_specs=[pl.BlockSpec((tm, D), lambda i: (i, 0)),
                  pl.BlockSpec((1, D), lambda i: (0, 0)),
                  pl.BlockSpec((D, E), lambda i: (0, 0)),
                  pl.BlockSpec((1, E), lambda i: (0, 0))],
        out_specs=[pl.BlockSpec((tm, D), lambda i: (i, 0)),
                   pl.BlockSpec((tm, D // 2), lambda i: (i, 0)),
                   pl.BlockSpec((tm, TOP_K), lambda i: (i, 0)),
                   pl.BlockSpec((tm, TOP_K), lambda i: (i, 0)),
                   pl.BlockSpec((tm, TOP_K), lambda i: (i, 0)),
                   pl.BlockSpec((1, E), lambda i: (0, 0))],
        out_shape=[jax.ShapeDtypeStruct((T, D), BF16), jax.ShapeDtypeStruct((T, D // 2), jnp.uint32),
                   jax.ShapeDtypeStruct((T, TOP_K), jnp.int32), jax.ShapeDtypeStruct((T, TOP_K), F32),
                   jax.ShapeDtypeStruct((T, TOP_K), jnp.int32), jax.ShapeDtypeStruct((1, E), F32)],
        compiler_params=_params(("arbitrary",)),
        name="norm_router",
    )(x, g.reshape(1, D), w_router, router_bias.reshape(1, E))


def _mm_wt_kernel(a_ref, wt_ref, o_ref, w_s, *, sigmoid):
    @pl.when(pl.program_id(1) == 0)
    def _():
        w_s[...] = wt_ref[...].T.astype(w_s.dtype)

    acc = jnp.dot(a_ref[...], w_s[...], preferred_element_type=F32)
    if sigmoid:
        acc = _sigmoid_tanh(acc)
    o_ref[...] = acc.astype(o_ref.dtype)


def _matmul_wt(a, wt, row0, n_rows_w, out_dtype, *, sigmoid=False, tm_pref=1088, tn_pref=512, name="matmul_wt"):
    M, K = a.shape
    tm = _pick(M, tm_pref, 16)
    tn = _pick(n_rows_w, tn_pref, LANES)
    if row0 % tn and row0 % SUBLANES:
        wt, row0 = wt[row0:row0 + n_rows_w], 0
    if row0 % tn == 0:
        w_spec = pl.BlockSpec((tn, K), lambda j, i: (row0 // tn + j, 0))
    else:
        w_spec = pl.BlockSpec((pl.Element(tn), pl.Element(K)),
                              lambda j, i: (pl.multiple_of(row0 + j * tn, SUBLANES), 0))
    return pl.pallas_call(
        functools.partial(_mm_wt_kernel, sigmoid=sigmoid),
        grid=(n_rows_w // tn, M // tm),
        in_specs=[pl.BlockSpec((tm, K), lambda j, i: (i, 0)), w_spec],
        out_specs=pl.BlockSpec((tm, tn), lambda j, i: (i, j)),
        out_shape=jax.ShapeDtypeStruct((M, n_rows_w), out_dtype),
        scratch_shapes=[pltpu.VMEM((K, tn), BF16)],
        compiler_params=_params(("arbitrary", "arbitrary")),
        name=name,
    )(a, wt)


def _mm_residual_kernel(a_ref, w_ref, *refs, read_res):
    o_ref = refs[-1]
    res = read_res(refs[:-1], pl.program_id(1))
    o_ref[...] = res + jnp.dot(a_ref[...], w_ref[...], preferred_element_type=F32)


def _matmul_residual(a, w, res_parts, *, tm_pref=512, tn_pref=1024):
    M, K = a.shape
    N = w.shape[1]
    tm = _pick(math.gcd(*[p.shape[0] for p in res_parts]), tm_pref, 16)
    tn = _pick(N, tn_pref, LANES)
    res_specs, read_res = _stacked_rows(res_parts, tm, tn, 1, 0)
    return pl.pallas_call(
        functools.partial(_mm_residual_kernel, read_res=read_res),
        grid=(N // tn, M // tm),
        in_specs=[pl.BlockSpec((tm, K), lambda j, i: (i, 0)),
                  pl.BlockSpec((K, tn), lambda j, i: (0, j))] + res_specs,
        out_specs=pl.BlockSpec((tm, tn), lambda j, i: (i, j)),
        out_shape=jax.ShapeDtypeStruct((M, N), F32),
        compiler_params=_params(("arbitrary", "arbitrary")),
        name="out_proj_residual",
    )(a, w, *res_parts)


def _merge_kernel(g_ref, at_ref, wv_ref, wg_ref, wf_ref, sa_ref, sb_ref, o_ref):
    g = g_ref[...]
    val = jnp.dot(g, wv_ref[...], preferred_element_type=F32)
    gate = jnp.dot(g, wg_ref[...], preferred_element_type=F32)
    y_b = jnp.dot(at_ref[...], wf_ref[...], preferred_element_type=F32)
    y_a = val * _sigmoid_tanh(gate)
    o_ref[...] = (sa_ref[...].astype(F32) * y_a + sb_ref[...].astype(F32) * y_b).astype(o_ref.dtype)


def _merge(g, attn, w_glu, w_fox_out, sig_gates, *, tm_pref=1088, tn_pref=512):
    T, K = g.shape
    D = w_fox_out.shape[1]
    tm = _pick(T, tm_pref, 16)
    tn = _pick(D, tn_pref, LANES)
    nj = D // tn
    return pl.pallas_call(
        _merge_kernel,
        grid=(nj, T // tm),
        in_specs=[pl.BlockSpec((tm, K), lambda j, i: (i, 0)),
                  pl.BlockSpec((tm, K), lambda j, i: (i, 0)),
                  pl.BlockSpec((K, tn), lambda j, i: (0, j)),
                  pl.BlockSpec((K, tn), lambda j, i: (0, j + nj)),
                  pl.BlockSpec((K, tn), lambda j, i: (0, j)),
                  pl.BlockSpec((tm, tn), lambda j, i: (i, j)),
                  pl.BlockSpec((tm, tn), lambda j, i: (i, j + nj))],
        out_specs=pl.BlockSpec((tm, tn), lambda j, i: (i, j)),
        out_shape=jax.ShapeDtypeStruct((T, D), BF16),
        compiler_params=_params(("arbitrary", "arbitrary")),
        name="merge_branches",
    )(g, attn, w_glu, w_glu, w_fox_out, sig_gates, sig_gates)


def _s5_disc_kernel(are_ref, aim_ref, ldt_ref, bre_ref, bim_ref, lre_ref, lim_ref, bbre_ref, bbim_ref):
    lam_re = jnp.minimum(are_ref[...], LAM_RE_MAX)
    lam_im = aim_ref[...]
    dt = jnp.exp(ldt_ref[...])
    mag = jnp.exp(lam_re * dt)
    lb_re = mag * jnp.cos(lam_im * dt)
    lb_im = mag * jnp.sin(lam_im * dt)
    lre_ref[...] = lb_re
    lim_ref[...] = lb_im
    n_re = lb_re - 1.0
    den = lam_re * lam_re + lam_im * lam_im
    co_re = (n_re * lam_re + lb_im * lam_im) / den
    co_im = (lb_im * lam_re - n_re * lam_im) / den
    G, P = co_re.shape
    PH = bre_ref.shape[1]
    hg = PH // P
    rows = lax.broadcasted_iota(jnp.int32, (P, PH), 0)
    cols = lax.broadcasted_iota(jnp.int32, (P, PH), 1)
    expand = jnp.where(cols // hg == rows, 1.0, 0.0).astype(F32)
    ce_re = jnp.dot(co_re, expand, precision=HIGHEST, preferred_element_type=F32)
    ce_im = jnp.dot(co_im, expand, precision=HIGHEST, preferred_element_type=F32)
    b_re = bre_ref[...]
    b_im = bim_ref[...]
    bbre_ref[...] = ce_re * b_re - ce_im * b_im
    bbim_ref[...] = ce_re * b_im + ce_im * b_re


def _s5_discretise(a_re, a_im, log_dt, b_re, b_im):
    G, P = a_re.shape
    hg = b_re.shape[2]
    outs = pl.pallas_call(
        _s5_disc_kernel,
        out_shape=[jax.ShapeDtypeStruct((G, P), F32), jax.ShapeDtypeStruct((G, P), F32),
                   jax.ShapeDtypeStruct((G, P * hg), F32), jax.ShapeDtypeStruct((G, P * hg), F32)],
        name="s5_discretise",
    )(a_re, a_im, log_dt.reshape(G, 1), b_re.reshape(G, P * hg), b_im.reshape(G, P * hg))
    lam_re, lam_im, bb_re, bb_im = outs
    return lam_re, lam_im, bb_re.reshape(G, P, hg), bb_im.reshape(G, P, hg)


def _s5_pack_weights(lam_re, lam_im, bb_re, bb_im, c_re, c_im):
    G, P, hg = bb_re.shape
    nblk = G // SSM_BLOCK_GROUPS
    half_ch = SSM_BLOCK_CH // 2
    pairs_per_half = SSM_PAIRS // 2

    def pair_view(x):
        return x.reshape((nblk, SSM_PAIRS, 2) + x.shape[1:])

    bre = pair_view(bb_re)
    bim = pair_view(bb_im)
    zero = jnp.zeros_like(bre[:, :, 0])

    def rows_of(which):
        re = [bre[:, :, 0], zero] if which == 0 else [zero, bre[:, :, 1]]
        im = [bim[:, :, 0], zero] if which == 0 else [zero, bim[:, :, 1]]
        blk = jnp.concatenate(re + im, axis=2)
        return jnp.swapaxes(blk, 2, 3)

    compact = jnp.concatenate([rows_of(0), rows_of(1)], axis=2)
    local = jnp.arange(SSM_PAIRS) % pairs_per_half
    place = jax.nn.one_hot(local[:, None] * 2 * hg + jnp.arange(2 * hg)[None, :], half_ch, dtype=F32)
    bw = jnp.einsum('prc,bprs->bpcs', place, compact)
    bw_hi = bw.astype(BF16)
    bw_lo = (bw - bw_hi.astype(F32)).astype(BF16)
    bw_hi2 = jnp.concatenate([bw_hi, bw_hi], axis=2)

    cre = pair_view(c_re)
    cim = pair_view(c_im)
    zc = jnp.zeros_like(cre[:, :, 0])

    def cols_of(which):
        re = [cre[:, :, 0], zc] if which == 0 else [zc, cre[:, :, 1]]
        im = [-cim[:, :, 0], zc] if which == 0 else [zc, -cim[:, :, 1]]
        return jnp.concatenate(re + im, axis=3)

    ccompact = jnp.concatenate([cols_of(0), cols_of(1)], axis=2)
    cplace = jax.nn.one_hot(jnp.arange(SSM_PAIRS)[:, None] * 2 * hg + jnp.arange(2 * hg)[None, :],
                            SSM_BLOCK_CH, dtype=F32)
    cw = jnp.einsum('prc,bprs->bpsc', cplace, ccompact).astype(BF16)

    lre = pair_view(lam_re)
    lim = pair_view(lam_im)
    a_re = jnp.concatenate([lre[:, :, 0], lre[:, :, 1]], axis=-1)
    a_im = jnp.concatenate([lim[:, :, 0], lim[:, :, 1]], axis=-1)
    return bw_hi2, bw_lo, cw, jnp.concatenate([a_re, a_im], axis=-1)


def _pack_state(s_re, s_im):
    B, G, P = s_re.shape
    nblk = G // SSM_BLOCK_GROUPS
    re = s_re.reshape(B, nblk, SSM_PAIRS, 2 * P)
    im = s_im.reshape(B, nblk, SSM_PAIRS, 2 * P)
    return jnp.moveaxis(jnp.concatenate([re, im], axis=-1), 1, 0)


def _unpack_state(st):
    nblk, B, pairs, w = st.shape
    st = jnp.moveaxis(st, 0, 1)
    P = w // 4
    re = st[..., :2 * P].reshape(B, nblk * pairs * 2, P)
    im = st[..., 2 * P:].reshape(B, nblk * pairs * 2, P)
    return re, im


def _s5_kernel(*refs, n_u, nseq, lc, rp, ilp):
    u_refs = refs[:n_u]
    bw2_ref, bwlo_ref, cw_ref, lam_ref, d_ref, h0_ref, y_ref, hout_ref, sre_ref, sim_ref, st_ref = refs[n_u:]
    chunk = pl.program_id(1)
    rows = nseq * lc
    half = LANES

    @pl.when(chunk == 0)
    def _():
        st_ref[...] = h0_ref[0]

    lam = lam_ref[0]
    sw = lam.shape[1] // 2
    a_re = lam[:, :sw]
    a_im = lam[:, sw:]

    u = u_refs[0][...] if n_u == 1 else jnp.concatenate([r[...] for r in u_refs], axis=0)
    for hf in range(2):
        uh = u[:, hf * half:(hf + 1) * half]
        hi = uh.astype(BF16)
        lo = (uh - hi.astype(F32)).astype(BF16)
        lhs = jnp.concatenate([hi, lo], axis=1)
        for q in range(SSM_PAIRS // 2):
            p = hf * (SSM_PAIRS // 2) + q
            bu = (jnp.dot(lhs, bw2_ref[0, p], preferred_element_type=F32)
                  + jnp.dot(hi, bwlo_ref[0, p], preferred_element_type=F32))
            sre_ref[pl.ds(p * rp, rows), :] = bu[:, :sw]
            sim_ref[pl.ds(p * rp, rows), :] = bu[:, sw:]

    def group_body(g, carry):
        seqs = [g * ilp + j for j in range(ilp)]

        def step(t, c):
            out = []
            for j, s in enumerate(seqs):
                x_re, x_im = c[2 * j], c[2 * j + 1]
                at = pl.ds(s * lc + t, SSM_PAIRS, stride=rp)
                n_re = a_re * x_re - a_im * x_im + sre_ref[at, :]
                n_im = a_re * x_im + a_im * x_re + sim_ref[at, :]
                sre_ref[at, :] = n_re
                sim_ref[at, :] = n_im
                out += [n_re, n_im]
            return tuple(out)

        init = []
        for s in seqs:
            x0 = st_ref[s]
            init += [x0[:, :sw], x0[:, sw:]]
        fin = lax.fori_loop(0, lc, step, tuple(init), unroll=4)
        for j, s in enumerate(seqs):
            st_ref[s] = jnp.concatenate([fin[2 * j], fin[2 * j + 1]], axis=1)
        return carry

    lax.fori_loop(0, nseq // ilp, group_body, 0)
    hout_ref[0] = st_ref[...]

    y = d_ref[...] * u
    for p in range(SSM_PAIRS):
        at = pl.ds(p * rp, rows)
        states = jnp.concatenate([sre_ref[at, :], sim_ref[at, :]], axis=1).astype(BF16)
        y = y + jnp.dot(states, cw_ref[0, p], preferred_element_type=F32)
    inner = math.sqrt(2.0 / math.pi) * (y + 0.044715 * (y * y * y))
    y = (0.5 * y * (1.0 + jnp.tanh(inner))).astype(y_ref.dtype)
    rows_u = rows // n_u
    for j in range(n_u):
        y_ref[j] = y[j * rows_u:(j + 1) * rows_u]


def _s5_mixer(z, row0, n_u, seq_per_u, seq_len, weights, d_skip, h0, *, lc):
    bw2, bwlo, cw, lam = weights
    nblk = bw2.shape[0]
    nseq = n_u * seq_per_u
    rows_u = seq_per_u * lc
    rows = n_u * rows_u
    nchunk = seq_len // lc
    assert nchunk == 1 or seq_per_u == 1
    rp = rows + SUBLANES
    r0 = row0 // rows_u
    sw4 = lam.shape[-1]
    ilp = 4 if nseq % 4 == 0 else 1
    kern = functools.partial(_s5_kernel, n_u=n_u, nseq=nseq, lc=lc, rp=rp, ilp=ilp)

    def u_spec(j):
        return pl.BlockSpec((rows_u, SSM_BLOCK_CH), lambda b, c: (r0 + j * nchunk + c, b))

    return pl.pallas_call(
        kern,
        grid=(nblk, nchunk),
        in_specs=[u_spec(j) for j in range(n_u)] + [
            pl.BlockSpec((1,) + bw2.shape[1:], lambda b, c: (b, 0, 0, 0)),
            pl.BlockSpec((1,) + bwlo.shape[1:], lambda b, c: (b, 0, 0, 0)),
            pl.BlockSpec((1,) + cw.shape[1:], lambda b, c: (b, 0, 0, 0)),
            pl.BlockSpec((1, SSM_PAIRS, sw4), lambda b, c: (b, 0, 0)),
            pl.BlockSpec((1, SSM_BLOCK_CH), lambda b, c: (0, b)),
            pl.BlockSpec((1, nseq, SSM_PAIRS, sw4), lambda b, c: (b, 0, 0, 0))],
        out_specs=[pl.BlockSpec((n_u, rows_u, SSM_BLOCK_CH), lambda b, c: (0, c, b)),
                   pl.BlockSpec((1, nseq, SSM_PAIRS, sw4), lambda b, c: (b, 0, 0, 0))],
        out_shape=[jax.ShapeDtypeStruct((n_u, nchunk * rows_u, nblk * SSM_BLOCK_CH), BF16),
                   jax.ShapeDtypeStruct((nblk, nseq, SSM_PAIRS, sw4), F32)],
        scratch_shapes=[pltpu.VMEM((SSM_PAIRS * rp, sw4 // 2), F32),
                        pltpu.VMEM((SSM_PAIRS * rp, sw4 // 2), F32),
                        pltpu.VMEM((nseq, SSM_PAIRS, sw4), F32)],
        compiler_params=_params(("arbitrary", "arbitrary")),
        name="s5_mixer",
    )(*([z] * n_u), bw2, bwlo, cw, lam, d_skip.reshape(1, -1), h0)


def _cumsum_kernel(x_ref, o_ref, *, chunk):
    H, L = x_ref.shape[1], x_ref.shape[2]
    carry = jnp.zeros((H, 1), F32)
    for s in range(0, L, chunk):
        w = min(chunk, L - s)
        r = lax.broadcasted_iota(jnp.int32, (w, w), 0)
        c = lax.broadcasted_iota(jnp.int32, (w, w), 1)
        tri = jnp.where(r <= c, 1.0, 0.0).astype(F32)
        part = jnp.dot(x_ref[0, :, s:s + w], tri, precision=HIGHEST, preferred_element_type=F32) + carry
        o_ref[0, :, s:s + w] = part
        carry = part[:, w - 1:w]


def _cumsum_last(x):
    B, H, L = x.shape
    return pl.pallas_call(
        functools.partial(_cumsum_kernel, chunk=256),
        grid=(B,),
        in_specs=[pl.BlockSpec((1, H, L), lambda b: (b, 0, 0))],
        out_specs=pl.BlockSpec((1, H, L), lambda b: (b, 0, 0)),
        out_shape=jax.ShapeDtypeStruct((B, H, L), F32),
        compiler_params=_params(("arbitrary",)),
        name="logf_cumsum",
    )(x)


def _pick_head(c_ref, h):
    blk = c_ref[...]
    lane = lax.broadcasted_iota(jnp.int32, blk.shape, 1)
    return jnp.sum(jnp.where(lane == h, blk, 0.0), axis=1, keepdims=True)


def _nt_dot(a, b):
    return lax.dot_general(a, b, (((1,), (1,)), ((), ())), preferred_element_type=F32)


def _fox_prompt_kernel(q_ref, k_ref, v_ref, cq_ref, ck_ref, o_ref, *, tq, nq, scale):
    h = pl.program_id(1)
    qi = pl.program_id(2)
    q = (q_ref[...] * (scale * LOG2_E)).astype(BF16)
    cq = _pick_head(cq_ref, h) * LOG2_E

    def block(ki, carry, diagonal):
        m_prev, l_prev, acc = carry
        at = pl.ds(pl.multiple_of(ki * tq, tq), tq)
        s = _nt_dot(q, k_ref[at, :].astype(BF16)) - ck_ref[0, pl.ds(h * nq + ki, 1), :] * LOG2_E
        if diagonal:
            r = lax.broadcasted_iota(jnp.int32, (tq, tq), 0)
            c = lax.broadcasted_iota(jnp.int32, (tq, tq), 1)
            s = jnp.where(c <= r, s, MASK_VALUE)
        m_new = jnp.maximum(m_prev, jnp.max(s, axis=1, keepdims=True) + cq)
        alpha = jnp.exp2(m_prev - m_new)
        p = jnp.exp2(s + (cq - m_new))
        l_new = alpha * l_prev + jnp.sum(p, axis=1, keepdims=True)
        acc = alpha * acc + jnp.dot(p.astype(BF16), v_ref[at, :].astype(BF16), preferred_element_type=F32)
        return m_new, l_new, acc

    init = (jnp.full((tq, 1), MASK_VALUE, F32), jnp.zeros((tq, 1), F32), jnp.zeros(o_ref.shape, F32))
    carry = lax.fori_loop(0, qi, lambda ki, c: block(ki, c, False), init)
    _, l, acc = block(qi, carry, True)
    o_ref[...] = (acc / l).astype(o_ref.dtype)


def _fox_prompt(z, c_col, c_row, B, L, H, col_q, col_k, col_v, scale):
    dh = FOX_HEAD_DIM
    tq = _pick(L, 512)
    nq = L // tq
    return pl.pallas_call(
        functools.partial(_fox_prompt_kernel, tq=tq, nq=nq, scale=scale),
        grid=(B, H, nq),
        in_specs=[pl.BlockSpec((tq, dh), lambda b, h, qi: (b * nq + qi, col_q + h)),
                  pl.BlockSpec((L, dh), lambda b, h, qi: (b, col_k + h)),
                  pl.BlockSpec((L, dh), lambda b, h, qi: (b, col_v + h)),
                  pl.BlockSpec((tq, H), lambda b, h, qi: (b * nq + qi, 0)),
                  pl.BlockSpec((1, H * nq, tq), lambda b, h, qi: (b, 0, 0))],
        out_specs=pl.BlockSpec((tq, dh), lambda b, h, qi: (b * nq + qi, h)),
        out_shape=jax.ShapeDtypeStruct((B * L, H * dh), BF16),
        compiler_params=_params(("arbitrary",) * 3),
        name="fox_prompt",
    )(z, z, z, c_col, c_row.reshape(B, H * nq, tq))


def _fox_sample_kernel(q_ref, kn_ref, vn_ref, kc_ref, vc_ref, cq_ref, ck_ref, o_ref, *, past, n_heads, scale):
    dh = FOX_HEAD_DIM
    n = q_ref.shape[0]
    r = lax.broadcasted_iota(jnp.int32, (n, n), 0)
    c = lax.broadcasted_iota(jnp.int32, (n, n), 1)
    causal = c <= r
    cq_all = cq_ref[...]
    outs = []
    for h in range(n_heads):
        cols = slice(h * dh, (h + 1) * dh)
        q = (q_ref[:, cols] * scale).astype(BF16)
        head_rows = pl.ds(h, past, stride=n_heads)
        c_all = ck_ref[0, h:h + 1, :]
        cq = cq_all[:, h:h + 1]
        s_old = _nt_dot(q, kc_ref[0, head_rows, :].astype(BF16)) + cq - c_all[:, :past]
        s_new = _nt_dot(q, kn_ref[:, cols].astype(BF16)) + cq - c_all[:, past:]
        s_new = jnp.where(causal, s_new, MASK_VALUE)
        m = jnp.maximum(jnp.max(s_old, axis=1, keepdims=True), jnp.max(s_new, axis=1, keepdims=True))
        p_old = jnp.exp(s_old - m)
        p_new = jnp.exp(s_new - m)
        l = jnp.sum(p_old, axis=1, keepdims=True) + jnp.sum(p_new, axis=1, keepdims=True)
        acc = (jnp.dot(p_old.astype(BF16), vc_ref[0, head_rows, :].astype(BF16), preferred_element_type=F32)
               + jnp.dot(p_new.astype(BF16), vn_ref[:, cols].astype(BF16), preferred_element_type=F32))
        outs.append((acc / l).astype(o_ref.dtype))
    o_ref[...] = jnp.concatenate(outs, axis=1)


def _fox_sample(z, row0, cache_k, cache_v, c_col, c_row, B, n, H, col_q, col_k, col_v, scale):
    dh = FOX_HEAD_DIM
    past = cache_k.shape[1] // H
    r0 = row0 // n
    width = H * dh
    assert (col_q * dh) % width == 0 and (col_k * dh) % width == 0 and (col_v * dh) % width == 0
    return pl.pallas_call(
        functools.partial(_fox_sample_kernel, past=past, n_heads=H, scale=scale),
        grid=(B,),
        in_specs=[pl.BlockSpec((n, width), lambda b: (r0 + b, col_q * dh // width)),
                  pl.BlockSpec((n, width), lambda b: (r0 + b, col_k * dh // width)),
                  pl.BlockSpec((n, width), lambda b: (r0 + b, col_v * dh // width)),
                  pl.BlockSpec((1, past * H, dh), lambda b: (b, 0, 0)),
                  pl.BlockSpec((1, past * H, dh), lambda b: (b, 0, 0)),
                  pl.BlockSpec((n, H), lambda b: (b, 0)),
                  pl.BlockSpec((1, H, past + n), lambda b: (b, 0, 0))],
        out_specs=pl.BlockSpec((n, width), lambda b: (b, 0)),
        out_shape=jax.ShapeDtypeStruct((B * n, width), BF16),
        compiler_params=_params(("arbitrary",)),
        name="fox_sample",
    )(z, z, z, cache_k, cache_v, c_col, c_row)


def _gather_rows_kernel(tok_ref, used_ref, h_hbm, o_ref, buf, sem, *, tm):
    i = pl.program_id(0)
    n = used_ref[0]

    def row_copy(blk, slot, r):
        t = tok_ref[blk * tm + r]
        return pltpu.make_async_copy(h_hbm.at[pl.ds(t, 1), :], buf.at[slot, pl.ds(r, 1), :], sem.at[slot])

    def issue(blk, slot):
        def body(r, c):
            row_copy(blk, slot, r).start()
            return c
        lax.fori_loop(0, tm, body, 0, unroll=8)

    def drain(blk, slot):
        del blk
        pltpu.make_async_copy(h_hbm.at[pl.ds(0, tm), :], buf.at[slot], sem.at[slot]).wait()

    @pl.when(jnp.logical_and(i == 0, n > 0))
    def _():
        issue(0, 0)

    @pl.when(i + 1 < n)
    def _():
        issue(i + 1, (i + 1) % 2)

    @pl.when(i < n)
    def _():
        slot = i % 2
        drain(i, slot)
        words = buf[slot]
        lo = lax.bitcast_convert_type(words << 16, F32)
        hi = lax.bitcast_convert_type(words & jnp.uint32(0xFFFF0000), F32)
        o_ref[...] = jnp.concatenate([lo, hi], axis=1).astype(o_ref.dtype)

    @pl.when(i >= n)
    def _():
        o_ref[...] = jnp.zeros(o_ref.shape, o_ref.dtype)


def _gather_rows(h_packed, slot_tok, n_used, tm):
    h = h_packed
    n_slots = slot_tok.shape[0]
    D = 2 * h.shape[1]
    return pl.pallas_call(
        functools.partial(_gather_rows_kernel, tm=tm),
        grid_spec=pltpu.PrefetchScalarGridSpec(
            num_scalar_prefetch=2,
            grid=(n_slots // tm,),
            in_specs=[pl.BlockSpec(memory_space=pl.ANY)],
            out_specs=pl.BlockSpec((tm, D), lambda i, tok, used: (i, 0)),
            scratch_shapes=[pltpu.VMEM((2, tm, D // 2), h.dtype), pltpu.SemaphoreType.DMA((2,))]),
        out_shape=jax.ShapeDtypeStruct((n_slots, D), BF16),
        compiler_params=_params(("arbitrary",)),
        name="moe_dispatch_gather",
    )(slot_tok, n_used, h)


def _grouped_kernel(blk0_ref, nblk_ref, x_hbm, *refs, n_w, tm, tn):
    w_refs = refs[:n_w]
    o_hbm = refs[n_w]
    w_bf = refs[n_w + 1:2 * n_w + 1]
    xbuf, obuf, tbuf, primed_ref, pending_ref, in_sem, out_sem, tail_sem = refs[2 * n_w + 1:]
    col = pl.program_id(0)
    e = pl.program_id(1)
    n_col = pl.num_programs(0)
    n_exp = pl.num_programs(1)
    nblk = nblk_ref[e]
    blk0 = blk0_ref[e]
    npair = nblk // 2

    def block_copy(first, j, slot):
        return pltpu.make_async_copy(x_hbm.at[pl.ds((first + j) * tm, tm), :], xbuf.at[slot], in_sem.at[slot])

    def x_copy(j, slot):
        return block_copy(blk0, j, slot)

    def pair_base(p):
        return jnp.where(p == 0, PRIME_SLOT, (p % 2) * 2)

    def prime(first, count):
        block_copy(first, 0, PRIME_SLOT).start()

        @pl.when(count > 1)
        def _():
            block_copy(first, 1, PRIME_SLOT + 1).start()

    @pl.when(jnp.logical_and(col == 0, e == 0))
    def _():
        primed_ref[0] = 0
        for k in range(3):
            pending_ref[k] = 0

    def settle_pair(slot):
        @pl.when(pending_ref[slot] == 1)
        def _():
            pltpu.make_async_copy(obuf.at[slot], o_hbm.at[pl.ds(0, 2 * tm), pl.ds(0, tn)], out_sem.at[slot]).wait()
            pending_ref[slot] = 0

    def settle_tail():
        @pl.when(pending_ref[2] == 1)
        def _():
            pltpu.make_async_copy(tbuf, o_hbm.at[pl.ds(0, tm), pl.ds(0, tn)], tail_sem).wait()
            pending_ref[2] = 0

    has_next = jnp.logical_or(e + 1 < n_exp, col + 1 < n_col)
    e_next = jnp.where(e + 1 < n_exp, e + 1, 0)
    nblk_next = nblk_ref[e_next]
    prime_next = jnp.logical_and(has_next, nblk_next > 0)

    def request_next():
        @pl.when(prime_next)
        def _():
            prime(blk0_ref[e_next], nblk_next)

    def pair_out(p, slot):
        return pltpu.make_async_copy(obuf.at[slot],
                                     o_hbm.at[pl.ds((blk0 + 2 * p) * tm, 2 * tm), pl.ds(col * tn, tn)],
                                     out_sem.at[slot])

    def tail_out():
        return pltpu.make_async_copy(tbuf, o_hbm.at[pl.ds((blk0 + nblk - 1) * tm, tm), pl.ds(col * tn, tn)],
                                     tail_sem)

    def apply(x):
        if n_w == 2:
            a = jnp.dot(x, w_bf[0][...], preferred_element_type=F32)
            b = jnp.dot(x, w_bf[1][...], preferred_element_type=F32)
            return a * _sigmoid_tanh(a) * b
        return jnp.dot(x, w_bf[0][...], preferred_element_type=F32)

    @pl.when(nblk > 0)
    def _():
        @pl.when(primed_ref[0] == 0)
        def _():
            prime(blk0, nblk)

        for w_ref, dst in zip(w_refs, w_bf):
            dst[...] = w_ref[0].astype(BF16)

        def pair_body(p, carry):
            s0 = pair_base(p)
            s_next = ((p + 1) % 2) * 2

            @pl.when(2 * p + 2 < nblk)
            def _():
                x_copy(2 * p + 2, s_next).start()

            @pl.when(2 * p + 3 < nblk)
            def _():
                x_copy(2 * p + 3, s_next + 1).start()

            x_copy(2 * p, s0).wait()
            x_copy(2 * p + 1, s0 + 1).wait()
            y = apply(xbuf[pl.ds(s0, 2)].reshape(2 * tm, xbuf.shape[2]))
            oslot = p % 2

            @pl.when(p >= 2)
            def _():
                pair_out(p - 2, oslot).wait()

            @pl.when(p < 2)
            def _():
                settle_pair(oslot)

            obuf[oslot] = y.astype(obuf.dtype)
            pair_out(p, oslot).start()

            @pl.when(p == 0)
            def _():
                request_next()

            return carry

        lax.fori_loop(0, npair, pair_body, 0)

        @pl.when(npair >= 2)
        def _():
            pending_ref[npair % 2] = 1

        @pl.when(npair >= 1)
        def _():
            pending_ref[(npair - 1) % 2] = 1

        @pl.when(nblk % 2 == 1)
        def _():
            slot = pair_base(npair)
            x_copy(nblk - 1, slot).wait()
            y = apply(xbuf[slot])
            settle_tail()
            tbuf[...] = y.astype(tbuf.dtype)
            tail_out().start()
            pending_ref[2] = 1

            @pl.when(npair == 0)
            def _():
                request_next()

    @pl.when(nblk == 0)
    def _():
        request_next()

    primed_ref[0] = prime_next.astype(jnp.int32)

    @pl.when(jnp.logical_not(has_next))
    def _():
        settle_pair(0)
        settle_pair(1)

    @pl.when(e == n_exp - 1)
    def _():
        first = blk0 + nblk
        total = o_hbm.shape[0] // tm
        settle_tail()

        def zero_copy(j):
            return pltpu.make_async_copy(tbuf, o_hbm.at[pl.ds(j * tm, tm), pl.ds(col * tn, tn)], tail_sem)

        @pl.when(first < total)
        def _():
            tbuf[...] = jnp.zeros(tbuf.shape, tbuf.dtype)

        def start(j, c):
            zero_copy(j).start()
            return c

        def wait(j, c):
            zero_copy(j).wait()
            return c

        lax.fori_loop(first, total, start, 0)
        lax.fori_loop(first, total, wait, 0)


def _grouped_matmul(x, weights, blk0, nblk, tm, tn_pref, out_dtype, name):
    n_rows, K = x.shape
    E, _, N = weights[0].shape
    n_w = len(weights)
    tn = _pick(N, tn_pref, LANES)
    return pl.pallas_call(
        functools.partial(_grouped_kernel, n_w=n_w, tm=tm, tn=tn),
        grid_spec=pltpu.PrefetchScalarGridSpec(
            num_scalar_prefetch=2,
            grid=(N // tn, E),
            in_specs=[pl.BlockSpec(memory_space=pl.ANY)]
            + [pl.BlockSpec((1, K, tn), lambda c, e, b0, nb: (e, 0, c)) for _ in weights],
            out_specs=pl.BlockSpec(memory_space=pl.ANY),
            scratch_shapes=[pltpu.VMEM((K, tn), BF16) for _ in weights] + [
                pltpu.VMEM((PRIME_SLOT + 2, tm, K), x.dtype),
                pltpu.VMEM((2, 2 * tm, tn), out_dtype),
                pltpu.VMEM((tm, tn), out_dtype),
                pltpu.SMEM((1,), jnp.int32),
                pltpu.SMEM((3,), jnp.int32),
                pltpu.SemaphoreType.DMA((PRIME_SLOT + 2,)),
                pltpu.SemaphoreType.DMA((2,)),
                pltpu.SemaphoreType.DMA(())]),
        out_shape=jax.ShapeDtypeStruct((n_rows, N), out_dtype),
        compiler_params=_params(("arbitrary", "arbitrary")),
        name=name,
    )(blk0, nblk, x, *weights)


def _combine_kernel(dest_ref, x_ref, sh_ref, gate_ref, g_ref, rows_hbm, o_ref, buf, sem, *, tc, final_norm,
                    row0):
    i = pl.program_id(0)
    n = pl.num_programs(0)

    def row_copy(blk, slot, r, k):
        src = dest_ref[(row0 + blk * tc + r) * TOP_K + k]
        return pltpu.make_async_copy(rows_hbm.at[pl.ds(src, 1), :], buf.at[slot, k, pl.ds(r, 1), :],
                                     sem.at[slot])

    def issue(blk, slot):
        def body(r, c):
            for k in range(TOP_K):
                row_copy(blk, slot, r, k).start()
            return c
        lax.fori_loop(0, tc, body, 0)

    def drain(blk, slot):
        del blk
        for k in range(TOP_K):
            pltpu.make_async_copy(rows_hbm.at[pl.ds(0, tc), :], buf.at[slot, k], sem.at[slot]).wait()

    @pl.when(i == 0)
    def _():
        issue(0, 0)

    @pl.when(i + 1 < n)
    def _():
        issue(i + 1, (i + 1) % 2)

    slot = i % 2
    drain(i, slot)
    y = x_ref[...] + sh_ref[...]
    gates = gate_ref[...]
    for k in range(TOP_K):
        y = y + gates[:, k:k + 1] * buf[slot, k]
    o_ref[...] = _rms(y, g_ref[...]) if final_norm else y


def _combine(x, shared, gates, dest, rows, g_final, final_norm, row0, n_rows, tc=32):
    D = x.shape[1]
    tc = _pick(math.gcd(row0, n_rows) if row0 else n_rows, tc)
    b0 = row0 // tc
    return pl.pallas_call(
        functools.partial(_combine_kernel, tc=tc, final_norm=final_norm, row0=row0),
        grid_spec=pltpu.PrefetchScalarGridSpec(
            num_scalar_prefetch=1,
            grid=(n_rows // tc,),
            in_specs=[pl.BlockSpec((tc, D), lambda i, d: (b0 + i, 0)),
                      pl.BlockSpec((tc, D), lambda i, d: (b0 + i, 0)),
                      pl.BlockSpec((tc, TOP_K), lambda i, d: (b0 + i, 0)),
                      pl.BlockSpec((1, D), lambda i, d: (0, 0)),
                      pl.BlockSpec(memory_space=pl.ANY)],
            out_specs=pl.BlockSpec((tc, D), lambda i, d: (i, 0)),
            scratch_shapes=[pltpu.VMEM((2, TOP_K, tc, D), F32), pltpu.SemaphoreType.DMA((2,))]),
        out_shape=jax.ShapeDtypeStruct((n_rows, D), F32),
        compiler_params=_params(("arbitrary",)),
        name="moe_combine_norm",
    )(dest, x, shared, gates, g_final.reshape(1, D), rows)


def _slot_assign_kernel(start_ref, idx_ref, rank_ref, dest_ref, *, n_experts):
    idx = idx_ref[...]
    base = jnp.zeros(idx.shape, jnp.int32)
    for e in range(n_experts):
        base = jnp.where(idx == e, start_ref[e], base)
    dest_ref[...] = base + rank_ref[...]


def _routing_tables(idx, rank, counts, n_experts, tm):
    T = idx.shape[0]
    n_assign = T * TOP_K
    n_blocks = (n_assign + n_experts * (tm - 1) + tm - 1) // tm
    counts = counts.reshape(n_experts).astype(jnp.int32)
    padded = (counts + tm - 1) // tm * tm
    pad_end = jnp.cumsum(padded)
    pad_start = (pad_end - padded).astype(jnp.int32)
    shape2d = (n_assign // LANES, LANES) if n_assign % LANES == 0 else (T, TOP_K)
    dest = pl.pallas_call(
        functools.partial(_slot_assign_kernel, n_experts=n_experts),
        grid_spec=pltpu.PrefetchScalarGridSpec(
            num_scalar_prefetch=1, grid=(1,),
            in_specs=[pl.BlockSpec(shape2d, lambda i, s: (0, 0)), pl.BlockSpec(shape2d, lambda i, s: (0, 0))],
            out_specs=pl.BlockSpec(shape2d, lambda i, s: (0, 0))),
        out_shape=jax.ShapeDtypeStruct(shape2d, jnp.int32),
        name="moe_slot_assign",
    )(pad_start, idx.reshape(shape2d), rank.reshape(shape2d)).reshape(n_assign)
    slot_tok = jnp.zeros((n_blocks * tm,), jnp.int32).at[dest].set(
        jnp.arange(n_assign, dtype=jnp.int32) // TOP_K)
    return dest, slot_tok, (pad_start // tm).astype(jnp.int32), (padded // tm).astype(jnp.int32)


def kernel(x_prompt, x_sample, cache_k, cache_v, cache_logf, state_ssm_re, state_ssm_im, norm_mix, w_in, b_forget, ssm_a_re, ssm_a_im, ssm_log_dt, ssm_b_re, ssm_b_im, ssm_c_re, ssm_c_im, ssm_d, w_glu, w_fox_out, w_out, norm_ffn, w_router, router_bias, w_exp_gate, w_exp_up, w_exp_down, w_sh_gate, w_sh_up, w_sh_down, norm_final):
    Bp, Lp, D = x_prompt.shape
    Bs, Ls, _ = x_sample.shape
    depth = w_in.shape[0]
    past = cache_k.shape[2]
    H = cache_k.shape[3]
    dh = cache_k.shape[4]
    assert dh == FOX_HEAD_DIM
    d_fox = H * dh
    d_ssm = ssm_d.shape[1]
    n_experts = w_router.shape[2]
    Tp = Bp * Lp
    Ts = Bs * Ls
    T = Tp + Ts
    scale = dh ** -0.5
    n_uqkv = d_ssm + 3 * d_fox
    col_q = d_ssm // dh
    col_k = col_q + H
    col_v = col_k + H
    moe_tm = 256

    x_parts = (x_prompt.reshape(Tp, D), x_sample.reshape(Ts, D))
    outs = {name: [] for name in ("kp", "vp", "fp", "rp", "ip", "ks", "vs", "fs", "rs", "is")}
    for l in range(depth):
        wt_in = jnp.swapaxes(w_in[l], 0, 1)
        wt_f = wt_in[n_uqkv:n_uqkv + H]

        h, logf = _norm_forget(x_parts, norm_mix[l], wt_f, b_forget[l])
        z = _matmul_wt(h, wt_in, 0, n_uqkv, F32, name="in_proj")
        sig_gates = _matmul_wt(h, wt_in, n_uqkv + H, 2 * D, BF16, sigmoid=True, name="gate_proj")

        lam_re, lam_im, bb_re, bb_im = _s5_discretise(ssm_a_re[l], ssm_a_im[l], ssm_log_dt[l],
                                                      ssm_b_re[l], ssm_b_im[l])
        s5_w = _s5_pack_weights(lam_re, lam_im, bb_re, bb_im, ssm_c_re[l], ssm_c_im[l])
        zero_state = jnp.zeros((Bp,) + state_ssm_re.shape[2:], F32)
        g_p, st_p = _s5_mixer(z, 0, Bp, 1, Lp, s5_w, ssm_d[l], _pack_state(zero_state, zero_state),
                              lc=_pick(Lp, 256))
        g_s, st_s = _s5_mixer(z, Tp, 1, Bs, Ls, s5_w, ssm_d[l],
                              _pack_state(state_ssm_re[l].astype(F32), state_ssm_im[l].astype(F32)), lc=Ls)
        g = jnp.concatenate([g_p.reshape(Tp, d_ssm), g_s.reshape(Ts, d_ssm)], axis=0)

        logf_p = logf[:Tp].reshape(Bp, Lp, H)
        logf_s = logf[Tp:].reshape(Bs, Ls, H)
        crow_p = _cumsum_last(jnp.swapaxes(logf_p, 1, 2))
        ccol_p = jnp.swapaxes(crow_p, 1, 2).reshape(Tp, H)
        attn_p = _fox_prompt(z, ccol_p, crow_p, Bp, Lp, H, col_q, col_k, col_v, scale)
        lf_all = jnp.concatenate([cache_logf[l].astype(F32), logf_s], axis=1)
        crow_s = _cumsum_last(jnp.swapaxes(lf_all, 1, 2))
        ccol_s = jnp.swapaxes(crow_s[:, :, past:], 1, 2).reshape(Ts, H)
        attn_s = _fox_sample(z, Tp, cache_k[l].reshape(Bs, past * H, dh), cache_v[l].reshape(Bs, past * H, dh),
                             ccol_s, crow_s, Bs, Ls, H, col_q, col_k, col_v, scale)
        attn = jnp.concatenate([attn_p, attn_s], axis=0)

        merged = _merge(g, attn, w_glu[l].astype(BF16), w_fox_out[l].astype(BF16), sig_gates)
        x = _matmul_residual(merged, w_out[l].astype(BF16), x_parts)

        norm_out = norm_ffn[l]
        h2, h2p, idx, gates, rank, counts = _norm_router(x, norm_out, w_router[l], router_bias[l])
        dest, slot_tok, e_blk0, e_nblk = _routing_tables(idx, rank, counts, n_experts, moe_tm)
        xs = _gather_rows(h2p, slot_tok, (e_blk0[-1:] + e_nblk[-1:]), moe_tm)
        act = _grouped_matmul(xs, (w_exp_gate[l], w_exp_up[l]), e_blk0, e_nblk, moe_tm, 512, BF16, "expert_up")
        routed = _grouped_matmul(act, (w_exp_down[l],), e_blk0, e_nblk, moe_tm, 2048, F32, "expert_down")
        sh_tm = _pick(T, moe_tm, 16)
        sh_blk0 = jnp.zeros((1,), jnp.int32)
        sh_nblk = jnp.full((1,), T // sh_tm, jnp.int32)
        sh_act = _grouped_matmul(h2, (w_sh_gate[l][None], w_sh_up[l][None]), sh_blk0, sh_nblk, sh_tm, 512, BF16,
                                 "shared_up")
        shared = _grouped_matmul(sh_act, (w_sh_down[l][None],), sh_blk0, sh_nblk, sh_tm, 2048, F32,
                                 "shared_down")
        if l == depth - 1:
            y_p = _combine(x, shared, gates, dest, routed, norm_final, True, 0, Tp)
            y_s = _combine(x, shared, gates, dest, routed, norm_final, True, Tp, Ts)
        else:
            x_parts = (_combine(x, shared, gates, dest, routed, norm_final, False, 0, T),)

        k_all = z[:, d_ssm + d_fox:d_ssm + 2 * d_fox]
        v_all = z[:, d_ssm + 2 * d_fox:d_ssm + 3 * d_fox]
        re_p, im_p = _unpack_state(st_p)
        re_s, im_s = _unpack_state(st_s)
        outs["kp"].append(k_all[:Tp].reshape(Bp, Lp, H, dh))
        outs["vp"].append(v_all[:Tp].reshape(Bp, Lp, H, dh))
        outs["fp"].append(logf_p)
        outs["rp"].append(re_p)
        outs["ip"].append(im_p)
        outs["ks"].append(k_all[Tp:].reshape(Bs, Ls, H, dh))
        outs["vs"].append(v_all[Tp:].reshape(Bs, Ls, H, dh))
        outs["fs"].append(logf_s)
        outs["rs"].append(re_s)
        outs["is"].append(im_s)

    return (y_p.reshape(Bp, Lp, D), y_s.reshape(Bs, Ls, D),
            jnp.stack(outs["kp"]), jnp.stack(outs["vp"]), jnp.stack(outs["fp"]),
            jnp.stack(outs["rp"]), jnp.stack(outs["ip"]),
            jnp.stack(outs["ks"]), jnp.stack(outs["vs"]), jnp.stack(outs["fs"]),
            jnp.stack(outs["rs"]), jnp.stack(outs["is"]))
```

```python
import functools
import math

import jax
import jax.numpy as jnp
from jax import lax
from jax.experimental import pallas as pl
from jax.experimental.pallas import tpu as pltpu

F32 = jnp.float32
BF16 = jnp.bfloat16
HIGHEST = lax.Precision.HIGHEST

SSM_GROUP = 16
FOX_HEAD_DIM = 128
TOP_K = 8
ROUTED_SCALE = 2.5
NORM_EPS = 1e-6
LAM_RE_MAX = -1e-4

V7X_VMEM_BYTES = 64 * 1024 * 1024
VMEM_LIMIT = V7X_VMEM_BYTES - 6 * 1024 * 1024
LANES = 128
SUBLANES = 8
SSM_BLOCK_GROUPS = 16
SSM_BLOCK_CH = SSM_BLOCK_GROUPS * SSM_GROUP
SSM_PAIRS = SSM_BLOCK_GROUPS // 2
MASK_VALUE = -1e30
LOG2_E = 1.4426950408889634
PRIME_SLOT = 4


def _pick(n, pref, mult=SUBLANES):
    best = None
    d = mult
    while d <= min(n, pref):
        if n % d == 0:
            best = d
        d += mult
    return best if best is not None else n


def _params(sem, vmem=VMEM_LIMIT):
    return pltpu.CompilerParams(dimension_semantics=sem, vmem_limit_bytes=vmem)


def _sigmoid(x):
    return 1.0 / (1.0 + jnp.exp(-x))


def _sigmoid_tanh(x):
    return 0.5 * jnp.tanh(0.5 * x) + 0.5


def _rms(x, g):
    r = lax.rsqrt(jnp.mean(x * x, axis=-1, keepdims=True) + NORM_EPS)
    return x * r * g


def _dot_split(a, b, dims):
    def split(x):
        hi = x.astype(BF16)
        return hi, (x - hi.astype(F32)).astype(BF16)

    def dot(x, y):
        return lax.dot_general(x, y, dims, preferred_element_type=F32)

    a_hi, a_lo = split(a)
    b_hi, b_lo = split(b)
    return dot(a_hi, b_hi) + dot(a_lo, b_hi) + dot(a_hi, b_lo)


def _stacked_rows(parts, tm, width, row_axis, col_axis=None):
    def col(idx):
        return 0 if col_axis is None else idx[col_axis]

    if len(parts) == 1:
        return [pl.BlockSpec((tm, width), lambda *idx: (idx[row_axis], col(idx)))], lambda refs, i: refs[0][...]
    n_first = parts[0].shape[0] // tm
    specs = [pl.BlockSpec((tm, width), lambda *idx: (jnp.minimum(idx[row_axis], n_first - 1), col(idx))),
             pl.BlockSpec((tm, width), lambda *idx: (jnp.maximum(idx[row_axis] - n_first, 0), col(idx)))]
    return specs, lambda refs, i: jnp.where(i < n_first, refs[0][...], refs[1][...])


def _norm_forget_kernel(*refs, n_x, read_x):
    x_refs = refs[:n_x]
    g_ref, w_ref, b_ref, h_ref, logf_ref = refs[n_x:]
    h = _rms(read_x(x_refs, pl.program_id(0)), g_ref[...])
    h_ref[...] = h.astype(h_ref.dtype)
    z = _dot_split(h, w_ref[...], (((1,), (1,)), ((), ()))) + b_ref[...]
    logf_ref[...] = jnp.minimum(z, 0.0) - jnp.log1p(jnp.exp(-jnp.abs(z)))


def _norm_forget(x_parts, g, wt_f, b_f):
    D = x_parts[0].shape[1]
    T = sum(p.shape[0] for p in x_parts)
    H = wt_f.shape[0]
    tm = _pick(math.gcd(*[p.shape[0] for p in x_parts]), 256)
    x_specs, read_x = _stacked_rows(x_parts, tm, D, 0)
    return pl.pallas_call(
        functools.partial(_norm_forget_kernel, n_x=len(x_parts), read_x=read_x),
        grid=(T // tm,),
        in_specs=x_specs + [pl.BlockSpec((1, D), lambda i: (0, 0)),
                            pl.BlockSpec((H, D), lambda i: (0, 0)),
                            pl.BlockSpec((1, H), lambda i: (0, 0))],
        out_specs=[pl.BlockSpec((tm, D), lambda i: (i, 0)),
                   pl.BlockSpec((tm, H), lambda i: (i, 0))],
        out_shape=[jax.ShapeDtypeStruct((T, D), BF16), jax.ShapeDtypeStruct((T, H), F32)],
        compiler_params=_params(("arbitrary",)),
        name="norm_forget",
    )(*x_parts, g.reshape(1, D), wt_f, b_f.reshape(1, H))


def _norm_router_kernel(x_ref, g_ref, w_ref, b_ref, h_ref, hp_ref, idx_ref, gate_ref, rank_ref, cnt_ref):
    @pl.when(pl.program_id(0) == 0)
    def _():
        cnt_ref[...] = jnp.zeros(cnt_ref.shape, F32)

    h = _rms(x_ref[...], g_ref[...])
    hb = h.astype(BF16)
    h_ref[...] = hb
    bits = lax.bitcast_convert_type(hb.astype(F32), jnp.uint32)
    half = bits.shape[1] // 2
    hp_ref[...] = (bits[:, half:] & jnp.uint32(0xFFFF0000)) | (bits[:, :half] >> 16)
    scores = _sigmoid(_dot_split(h, w_ref[...], (((1,), (0,)), ((), ()))))
    tm, E = scores.shape
    lane = lax.broadcasted_iota(jnp.int32, (tm, E), 1).astype(F32)
    col = lax.broadcasted_iota(jnp.int32, (tm, TOP_K), 1)
    work = scores + b_ref[...]
    idx = jnp.zeros((tm, TOP_K), F32)
    sel = jnp.zeros((tm, TOP_K), F32)
    chosen = jnp.zeros((tm, E), F32)
    hits = []
    for k in range(TOP_K):
        m = jnp.max(work, axis=1, keepdims=True)
        am = jnp.min(jnp.where(work == m, lane, float(E)), axis=1, keepdims=True)
        hit = lane == am
        hits.append(hit)
        sk = jnp.sum(jnp.where(hit, scores, 0.0), axis=1, keepdims=True)
        idx = jnp.where(col == k, am, idx)
        sel = jnp.where(col == k, sk, sel)
        chosen = jnp.where(hit, 1.0, chosen)
        work = jnp.where(hit, -jnp.inf, work)
    idx_ref[...] = idx.astype(jnp.int32)
    gate_ref[...] = ROUTED_SCALE * sel / jnp.sum(sel, axis=1, keepdims=True)

    r = lax.broadcasted_iota(jnp.int32, (tm, tm), 0)
    c = lax.broadcasted_iota(jnp.int32, (tm, tm), 1)
    lower = jnp.where(c < r, 1.0, 0.0).astype(BF16)
    before = jnp.dot(lower, chosen.astype(BF16), preferred_element_type=F32) + cnt_ref[...]
    rank = jnp.zeros((tm, TOP_K), F32)
    for k in range(TOP_K):
        rank = jnp.where(col == k, jnp.sum(jnp.where(hits[k], before, 0.0), axis=1, keepdims=True), rank)
    rank_ref[...] = rank.astype(jnp.int32)
    cnt_ref[...] = cnt_ref[...] + jnp.sum(chosen, axis=0, keepdims=True)


def _norm_router(x, g, w_router, router_bias):
    T, D = x.shape
    E = w_router.shape[1]
    tm = _pick(T, 256)
    return pl.pallas_call(
        _norm_router_kernel,
        grid=(T // tm,),
        in_specs=[pl.BlockSpec((tm, D), lambda i: (i, 0)),
                  pl.BlockSpec((1, D), lambda i: (0, 0)),
                  pl.BlockSpec((D, E), lambda i: (0, 0)),
                  pl.BlockSpec((1, E), lambda i: (0, 0))],
        out_specs=[pl.BlockSpec((tm, D), lambda i: (i, 0)),
                   pl.BlockSpec((tm, D // 2), lambda i: (i, 0)),
                   pl.BlockSpec((tm, TOP_K), lambda i: (i, 0)),
                   pl.BlockSpec((tm, TOP_K), lambda i: (i, 0)),
                   pl.BlockSpec((tm, TOP_K), lambda i: (i, 0)),
                   pl.BlockSpec((1, E), lambda i: (0, 0))],
        out_shape=[jax.ShapeDtypeStruct((T, D), BF16), jax.ShapeDtypeStruct((T, D // 2), jnp.uint32),
                   jax.ShapeDtypeStruct((T, TOP_K), jnp.int32), jax.ShapeDtypeStruct((T, TOP_K), F32),
                   jax.ShapeDtypeStruct((T, TOP_K), jnp.int32), jax.ShapeDtypeStruct((1, E), F32)],
        compiler_params=_params(("arbitrary",)),
        name="norm_router",
    )(x, g.reshape(1, D), w_router, router_bias.reshape(1, E))


def _mm_wt_kernel(a_ref, wt_ref, o_ref, w_s, *, sigmoid):
    @pl.when(pl.program_id(1) == 0)
    def _():
        w_s[...] = wt_ref[...].T.astype(w_s.dtype)

    acc = jnp.dot(a_ref[...], w_s[...], preferred_element_type=F32)
    if sigmoid:
        acc = _sigmoid_tanh(acc)
    o_ref[...] = acc.astype(o_ref.dtype)


def _matmul_wt(a, wt, row0, n_rows_w, out_dtype, *, sigmoid=False, tm_pref=1088, tn_pref=512, name="matmul_wt"):
    M, K = a.shape
    tm = _pick(M, tm_pref, 16)
    tn = _pick(n_rows_w, tn_pref, LANES)
    if row0 % tn and row0 % SUBLANES:
        wt, row0 = wt[row0:row0 + n_rows_w], 0
    if row0 % tn == 0:
        w_spec = pl.BlockSpec((tn, K), lambda j, i: (row0 // tn + j, 0))
    else:
        w_spec = pl.BlockSpec((pl.Element(tn), pl.Element(K)),
                              lambda j, i: (pl.multiple_of(row0 + j * tn, SUBLANES), 0))
    return pl.pallas_call(
        functools.partial(_mm_wt_kernel, sigmoid=sigmoid),
        grid=(n_rows_w // tn, M // tm),
        in_specs=[pl.BlockSpec((tm, K), lambda j, i: (i, 0)), w_spec],
        out_specs=pl.BlockSpec((tm, tn), lambda j, i: (i, j)),
        out_shape=jax.ShapeDtypeStruct((M, n_rows_w), out_dtype),
        scratch_shapes=[pltpu.VMEM((K, tn), BF16)],
        compiler_params=_params(("arbitrary", "arbitrary")),
        name=name,
    )(a, wt)


def _mm_residual_kernel(a_ref, w_ref, *refs, read_res):
    o_ref = refs[-1]
    res = read_res(refs[:-1], pl.program_id(1))
    o_ref[...] = res + jnp.dot(a_ref[...], w_ref[...], preferred_element_type=F32)


def _matmul_residual(a, w, res_parts, *, tm_pref=512, tn_pref=1024):
    M, K = a.shape
    N = w.shape[1]
    tm = _pick(math.gcd(*[p.shape[0] for p in res_parts]), tm_pref, 16)
    tn = _pick(N, tn_pref, LANES)
    res_specs, read_res = _stacked_rows(res_parts, tm, tn, 1, 0)
    return pl.pallas_call(
        functools.partial(_mm_residual_kernel, read_res=read_res),
        grid=(N // tn, M // tm),
        in_specs=[pl.BlockSpec((tm, K), lambda j, i: (i, 0)),
                  pl.BlockSpec((K, tn), lambda j, i: (0, j))] + res_specs,
        out_specs=pl.BlockSpec((tm, tn), lambda j, i: (i, j)),
        out_shape=jax.ShapeDtypeStruct((M, N), F32),
        compiler_params=_params(("arbitrary", "arbitrary")),
        name="out_proj_residual",
    )(a, w, *res_parts)


def _merge_kernel(*refs, n_g, n_a, read_g, read_a):
    g_refs = refs[:n_g]
    at_refs = refs[n_g:n_g + n_a]
    wv_ref, wg_ref, wf_ref, sa_ref, sb_ref, o_ref, wv_s, wg_s, wf_s = refs[n_g + n_a:]
    i = pl.program_id(1)

    @pl.when(i == 0)
    def _():
        wv_s[...] = wv_ref[...].astype(BF16)
        wg_s[...] = wg_ref[...].astype(BF16)
        wf_s[...] = wf_ref[...].astype(BF16)

    g = read_g(g_refs, i)
    val = jnp.dot(g, wv_s[...], preferred_element_type=F32)
    gate = jnp.dot(g, wg_s[...], preferred_element_type=F32)
    y_b = jnp.dot(read_a(at_refs, i), wf_s[...], preferred_element_type=F32)
    y_a = val * _sigmoid_tanh(gate)
    o_ref[...] = (sa_ref[...].astype(F32) * y_a + sb_ref[...].astype(F32) * y_b).astype(o_ref.dtype)


def _merge(g_parts, attn_parts, w_glu, w_fox_out, sig_gates, *, tm_pref=512, tn_pref=512):
    K = g_parts[0].shape[1]
    T = sum(p.shape[0] for p in g_parts)
    D = w_fox_out.shape[1]
    tm = _pick(math.gcd(*[p.shape[0] for p in g_parts + attn_parts]), tm_pref, 16)
    tn = _pick(D, tn_pref, LANES)
    nj = D // tn
    g_specs, read_g = _stacked_rows(g_parts, tm, K, 1)
    a_specs, read_a = _stacked_rows(attn_parts, tm, K, 1)
    return pl.pallas_call(
        functools.partial(_merge_kernel, n_g=len(g_parts), n_a=len(attn_parts), read_g=read_g, read_a=read_a),
        grid=(nj, T // tm),
        in_specs=g_specs + a_specs + [
            pl.BlockSpec((K, tn), lambda j, i: (0, j)),
            pl.BlockSpec((K, tn), lambda j, i: (0, j + nj)),
            pl.BlockSpec((K, tn), lambda j, i: (0, j)),
            pl.BlockSpec((tm, tn), lambda j, i: (i, j)),
            pl.BlockSpec((tm, tn), lambda j, i: (i, j + nj))],
        out_specs=pl.BlockSpec((tm, tn), lambda j, i: (i, j)),
        out_shape=jax.ShapeDtypeStruct((T, D), BF16),
        scratch_shapes=[pltpu.VMEM((K, tn), BF16) for _ in range(3)],
        compiler_params=_params(("arbitrary", "arbitrary")),
        name="merge_branches",
    )(*g_parts, *attn_parts, w_glu, w_glu, w_fox_out, sig_gates, sig_gates)


def _s5_disc_kernel(are_ref, aim_ref, ldt_ref, bre_ref, bim_ref, lre_ref, lim_ref, bbre_ref, bbim_ref):
    lam_re = jnp.minimum(are_ref[...], LAM_RE_MAX)
    lam_im = aim_ref[...]
    dt = jnp.exp(ldt_ref[...])
    mag = jnp.exp(lam_re * dt)
    lb_re = mag * jnp.cos(lam_im * dt)
    lb_im = mag * jnp.sin(lam_im * dt)
    lre_ref[...] = lb_re
    lim_ref[...] = lb_im
    n_re = lb_re - 1.0
    den = lam_re * lam_re + lam_im * lam_im
    co_re = (n_re * lam_re + lb_im * lam_im) / den
    co_im = (lb_im * lam_re - n_re * lam_im) / den
    G, P = co_re.shape
    PH = bre_ref.shape[1]
    hg = PH // P
    rows = lax.broadcasted_iota(jnp.int32, (P, PH), 0)
    cols = lax.broadcasted_iota(jnp.int32, (P, PH), 1)
    expand = jnp.where(cols // hg == rows, 1.0, 0.0).astype(F32)
    ce_re = jnp.dot(co_re, expand, precision=HIGHEST, preferred_element_type=F32)
    ce_im = jnp.dot(co_im, expand, precision=HIGHEST, preferred_element_type=F32)
    b_re = bre_ref[...]
    b_im = bim_ref[...]
    bbre_ref[...] = ce_re * b_re - ce_im * b_im
    bbim_ref[...] = ce_re * b_im + ce_im * b_re


def _s5_discretise(a_re, a_im, log_dt, b_re, b_im):
    G, P = a_re.shape
    hg = b_re.shape[2]
    outs = pl.pallas_call(
        _s5_disc_kernel,
        out_shape=[jax.ShapeDtypeStruct((G, P), F32), jax.ShapeDtypeStruct((G, P), F32),
                   jax.ShapeDtypeStruct((G, P * hg), F32), jax.ShapeDtypeStruct((G, P * hg), F32)],
        name="s5_discretise",
    )(a_re, a_im, log_dt.reshape(G, 1), b_re.reshape(G, P * hg), b_im.reshape(G, P * hg))
    lam_re, lam_im, bb_re, bb_im = outs
    return lam_re, lam_im, bb_re.reshape(G, P, hg), bb_im.reshape(G, P, hg)


def _s5_pack_weights(lam_re, lam_im, bb_re, bb_im, c_re, c_im):
    G, P, hg = bb_re.shape
    nblk = G // SSM_BLOCK_GROUPS
    half_ch = SSM_BLOCK_CH // 2
    pairs_per_half = SSM_PAIRS // 2

    def pair_view(x):
        return x.reshape((nblk, SSM_PAIRS, 2) + x.shape[1:])

    bre = pair_view(bb_re)
    bim = pair_view(bb_im)
    zero = jnp.zeros_like(bre[:, :, 0])

    def rows_of(which):
        re = [bre[:, :, 0], zero] if which == 0 else [zero, bre[:, :, 1]]
        im = [bim[:, :, 0], zero] if which == 0 else [zero, bim[:, :, 1]]
        blk = jnp.concatenate(re + im, axis=2)
        return jnp.swapaxes(blk, 2, 3)

    compact = jnp.concatenate([rows_of(0), rows_of(1)], axis=2)
    local = jnp.arange(SSM_PAIRS) % pairs_per_half
    place = jax.nn.one_hot(local[:, None] * 2 * hg + jnp.arange(2 * hg)[None, :], half_ch, dtype=F32)
    bw = jnp.einsum('prc,bprs->bpcs', place, compact)
    bw_hi = bw.astype(BF16)
    bw_lo = (bw - bw_hi.astype(F32)).astype(BF16)
    bw_hi2 = jnp.concatenate([bw_hi, bw_hi], axis=2)

    cre = pair_view(c_re)
    cim = pair_view(c_im)
    zc = jnp.zeros_like(cre[:, :, 0])

    def cols_of(which):
        re = [cre[:, :, 0], zc] if which == 0 else [zc, cre[:, :, 1]]
        im = [-cim[:, :, 0], zc] if which == 0 else [zc, -cim[:, :, 1]]
        return jnp.concatenate(re + im, axis=3)

    ccompact = jnp.concatenate([cols_of(0), cols_of(1)], axis=2)
    cplace = jax.nn.one_hot(jnp.arange(SSM_PAIRS)[:, None] * 2 * hg + jnp.arange(2 * hg)[None, :],
                            SSM_BLOCK_CH, dtype=F32)
    cw = jnp.einsum('prc,bprs->bpsc', cplace, ccompact).astype(BF16)

    lre = pair_view(lam_re)
    lim = pair_view(lam_im)
    a_re = jnp.concatenate([lre[:, :, 0], lre[:, :, 1]], axis=-1)
    a_im = jnp.concatenate([lim[:, :, 0], lim[:, :, 1]], axis=-1)
    return bw_hi2, bw_lo, cw, jnp.concatenate([a_re, a_im], axis=-1)


def _pack_state(s_re, s_im):
    B, G, P = s_re.shape
    nblk = G // SSM_BLOCK_GROUPS
    re = s_re.reshape(B, nblk, SSM_PAIRS, 2 * P)
    im = s_im.reshape(B, nblk, SSM_PAIRS, 2 * P)
    return jnp.moveaxis(jnp.concatenate([re, im], axis=-1), 1, 0)


def _unpack_state(st):
    nblk, B, pairs, w = st.shape
    st = jnp.moveaxis(st, 0, 1)
    P = w // 4
    re = st[..., :2 * P].reshape(B, nblk * pairs * 2, P)
    im = st[..., 2 * P:].reshape(B, nblk * pairs * 2, P)
    return re, im


def _s5_kernel(*refs, n_u, nseq, lc, rp, ilp):
    u_refs = refs[:n_u]
    bw2_ref, bwlo_ref, cw_ref, lam_ref, d_ref, h0_ref, y_ref, hout_ref, sre_ref, sim_ref, st_ref = refs[n_u:]
    chunk = pl.program_id(1)
    rows = nseq * lc
    half = LANES

    @pl.when(chunk == 0)
    def _():
        st_ref[...] = h0_ref[0]

    lam = lam_ref[0]
    sw = lam.shape[1] // 2
    a_re = lam[:, :sw]
    a_im = lam[:, sw:]

    u = u_refs[0][...] if n_u == 1 else jnp.concatenate([r[...] for r in u_refs], axis=0)
    for hf in range(2):
        uh = u[:, hf * half:(hf + 1) * half]
        hi = uh.astype(BF16)
        lo = (uh - hi.astype(F32)).astype(BF16)
        lhs = jnp.concatenate([hi, lo], axis=1)
        for q in range(SSM_PAIRS // 2):
            p = hf * (SSM_PAIRS // 2) + q
            bu = (jnp.dot(lhs, bw2_ref[0, p], preferred_element_type=F32)
                  + jnp.dot(hi, bwlo_ref[0, p], preferred_element_type=F32))
            sre_ref[pl.ds(p * rp, rows), :] = bu[:, :sw]
            sim_ref[pl.ds(p * rp, rows), :] = bu[:, sw:]

    def group_body(g, carry):
        seqs = [g * ilp + j for j in range(ilp)]

        def step(t, c):
            out = []
            for j, s in enumerate(seqs):
                x_re, x_im = c[2 * j], c[2 * j + 1]
                at = pl.ds(s * lc + t, SSM_PAIRS, stride=rp)
                n_re = a_re * x_re - a_im * x_im + sre_ref[at, :]
                n_im = a_re * x_im + a_im * x_re + sim_ref[at, :]
                sre_ref[at, :] = n_re
                sim_ref[at, :] = n_im
                out += [n_re, n_im]
            return tuple(out)

        init = []
        for s in seqs:
            x0 = st_ref[s]
            init += [x0[:, :sw], x0[:, sw:]]
        fin = lax.fori_loop(0, lc, step, tuple(init), unroll=4)
        for j, s in enumerate(seqs):
            st_ref[s] = jnp.concatenate([fin[2 * j], fin[2 * j + 1]], axis=1)
        return carry

    lax.fori_loop(0, nseq // ilp, group_body, 0)
    hout_ref[0] = st_ref[...]

    y = d_ref[...] * u
    for p in range(SSM_PAIRS):
        at = pl.ds(p * rp, rows)
        states = jnp.concatenate([sre_ref[at, :], sim_ref[at, :]], axis=1).astype(BF16)
        y = y + jnp.dot(states, cw_ref[0, p], preferred_element_type=F32)
    inner = math.sqrt(2.0 / math.pi) * (y + 0.044715 * (y * y * y))
    y = (0.5 * y * (1.0 + jnp.tanh(inner))).astype(y_ref.dtype)
    rows_u = rows // n_u
    for j in range(n_u):
        y_ref[j] = y[j * rows_u:(j + 1) * rows_u]


def _s5_mixer(z, row0, n_u, seq_per_u, seq_len, weights, d_skip, h0, *, lc):
    bw2, bwlo, cw, lam = weights
    nblk = bw2.shape[0]
    nseq = n_u * seq_per_u
    rows_u = seq_per_u * lc
    rows = n_u * rows_u
    nchunk = seq_len // lc
    assert nchunk == 1 or seq_per_u == 1
    rp = rows + SUBLANES
    r0 = row0 // rows_u
    sw4 = lam.shape[-1]
    ilp = 4 if nseq % 4 == 0 else 1
    kern = functools.partial(_s5_kernel, n_u=n_u, nseq=nseq, lc=lc, rp=rp, ilp=ilp)

    def u_spec(j):
        return pl.BlockSpec((rows_u, SSM_BLOCK_CH), lambda b, c: (r0 + j * nchunk + c, b))

    return pl.pallas_call(
        kern,
        grid=(nblk, nchunk),
        in_specs=[u_spec(j) for j in range(n_u)] + [
            pl.BlockSpec((1,) + bw2.shape[1:], lambda b, c: (b, 0, 0, 0)),
            pl.BlockSpec((1,) + bwlo.shape[1:], lambda b, c: (b, 0, 0, 0)),
            pl.BlockSpec((1,) + cw.shape[1:], lambda b, c: (b, 0, 0, 0)),
            pl.BlockSpec((1, SSM_PAIRS, sw4), lambda b, c: (b, 0, 0)),
            pl.BlockSpec((1, SSM_BLOCK_CH), lambda b, c: (0, b)),
            pl.BlockSpec((1, nseq, SSM_PAIRS, sw4), lambda b, c: (b, 0, 0, 0))],
        out_specs=[pl.BlockSpec((n_u, rows_u, SSM_BLOCK_CH), lambda b, c: (0, c, b)),
                   pl.BlockSpec((1, nseq, SSM_PAIRS, sw4), lambda b, c: (b, 0, 0, 0))],
        out_shape=[jax.ShapeDtypeStruct((n_u, nchunk * rows_u, nblk * SSM_BLOCK_CH), BF16),
                   jax.ShapeDtypeStruct((nblk, nseq, SSM_PAIRS, sw4), F32)],
        scratch_shapes=[pltpu.VMEM((SSM_PAIRS * rp, sw4 // 2), F32),
                        pltpu.VMEM((SSM_PAIRS * rp, sw4 // 2), F32),
                        pltpu.VMEM((nseq, SSM_PAIRS, sw4), F32)],
        compiler_params=_params(("arbitrary", "arbitrary")),
        name="s5_mixer",
    )(*([z] * n_u), bw2, bwlo, cw, lam, d_skip.reshape(1, -1), h0)


def _cumsum_kernel(x_ref, o_ref, *, chunk):
    H, L = x_ref.shape[1], x_ref.shape[2]
    carry = jnp.zeros((H, 1), F32)
    for s in range(0, L, chunk):
        w = min(chunk, L - s)
        r = lax.broadcasted_iota(jnp.int32, (w, w), 0)
        c = lax.broadcasted_iota(jnp.int32, (w, w), 1)
        tri = jnp.where(r <= c, 1.0, 0.0).astype(F32)
        part = jnp.dot(x_ref[0, :, s:s + w], tri, precision=HIGHEST, preferred_element_type=F32) + carry
        o_ref[0, :, s:s + w] = part
        carry = part[:, w - 1:w]


def _cumsum_last(x):
    B, H, L = x.shape
    return pl.pallas_call(
        functools.partial(_cumsum_kernel, chunk=256),
        grid=(B,),
        in_specs=[pl.BlockSpec((1, H, L), lambda b: (b, 0, 0))],
        out_specs=pl.BlockSpec((1, H, L), lambda b: (b, 0, 0)),
        out_shape=jax.ShapeDtypeStruct((B, H, L), F32),
        compiler_params=_params(("arbitrary",)),
        name="logf_cumsum",
    )(x)


def _pick_head(c_ref, h):
    blk = c_ref[...]
    lane = lax.broadcasted_iota(jnp.int32, blk.shape, 1)
    return jnp.sum(jnp.where(lane == h, blk, 0.0), axis=1, keepdims=True)


def _nt_dot(a, b):
    return lax.dot_general(a, b, (((1,), (1,)), ((), ())), preferred_element_type=F32)


def _fox_prompt_kernel(q_ref, k_ref, v_ref, cq_ref, ck_ref, o_ref, *, tq, nq, scale):
    h = pl.program_id(1)
    qi = pl.program_id(2)
    q = (q_ref[...] * (scale * LOG2_E)).astype(BF16)
    cq = _pick_head(cq_ref, h) * LOG2_E

    def block(ki, carry, diagonal):
        m_prev, l_prev, acc = carry
        at = pl.ds(pl.multiple_of(ki * tq, tq), tq)
        s = _nt_dot(q, k_ref[at, :].astype(BF16)) - ck_ref[0, pl.ds(h * nq + ki, 1), :] * LOG2_E
        if diagonal:
            r = lax.broadcasted_iota(jnp.int32, (tq, tq), 0)
            c = lax.broadcasted_iota(jnp.int32, (tq, tq), 1)
            s = jnp.where(c <= r, s, MASK_VALUE)
        m_new = jnp.maximum(m_prev, jnp.max(s, axis=1, keepdims=True) + cq)
        alpha = jnp.exp2(m_prev - m_new)
        p = jnp.exp2(s + (cq - m_new))
        l_new = alpha * l_prev + jnp.sum(p, axis=1, keepdims=True)
        acc = alpha * acc + jnp.dot(p.astype(BF16), v_ref[at, :].astype(BF16), preferred_element_type=F32)
        return m_new, l_new, acc

    init = (jnp.full((tq, 1), MASK_VALUE, F32), jnp.zeros((tq, 1), F32), jnp.zeros(o_ref.shape, F32))
    carry = lax.fori_loop(0, qi, lambda ki, c: block(ki, c, False), init)
    _, l, acc = block(qi, carry, True)
    o_ref[...] = (acc / l).astype(o_ref.dtype)


def _fox_prompt(z, c_col, c_row, B, L, H, col_q, col_k, col_v, scale):
    dh = FOX_HEAD_DIM
    tq = _pick(L, 512)
    nq = L // tq
    return pl.pallas_call(
        functools.partial(_fox_prompt_kernel, tq=tq, nq=nq, scale=scale),
        grid=(B, H, nq),
        in_specs=[pl.BlockSpec((tq, dh), lambda b, h, qi: (b * nq + qi, col_q + h)),
                  pl.BlockSpec((L, dh), lambda b, h, qi: (b, col_k + h)),
                  pl.BlockSpec((L, dh), lambda b, h, qi: (b, col_v + h)),
                  pl.BlockSpec((tq, H), lambda b, h, qi: (b * nq + qi, 0)),
                  pl.BlockSpec((1, H * nq, tq), lambda b, h, qi: (b, 0, 0))],
        out_specs=pl.BlockSpec((tq, dh), lambda b, h, qi: (b * nq + qi, h)),
        out_shape=jax.ShapeDtypeStruct((B * L, H * dh), BF16),
        compiler_params=_params(("arbitrary",) * 3),
        name="fox_prompt",
    )(z, z, z, c_col, c_row.reshape(B, H * nq, tq))


def _fox_sample_kernel(q_ref, kn_ref, vn_ref, kc_ref, vc_ref, cq_ref, ck_ref, o_ref, *, past, n_heads, scale):
    dh = FOX_HEAD_DIM
    n = q_ref.shape[0]
    r = lax.broadcasted_iota(jnp.int32, (n, n), 0)
    c = lax.broadcasted_iota(jnp.int32, (n, n), 1)
    causal = c <= r
    cq_all = cq_ref[...]
    outs = []
    for h in range(n_heads):
        cols = slice(h * dh, (h + 1) * dh)
        q = (q_ref[:, cols] * scale).astype(BF16)
        head_rows = pl.ds(h, past, stride=n_heads)
        c_all = ck_ref[0, h:h + 1, :]
        cq = cq_all[:, h:h + 1]
        s_old = _nt_dot(q, kc_ref[0, head_rows, :].astype(BF16)) + cq - c_all[:, :past]
        s_new = _nt_dot(q, kn_ref[:, cols].astype(BF16)) + cq - c_all[:, past:]
        s_new = jnp.where(causal, s_new, MASK_VALUE)
        m = jnp.maximum(jnp.max(s_old, axis=1, keepdims=True), jnp.max(s_new, axis=1, keepdims=True))
        p_old = jnp.exp(s_old - m)
        p_new = jnp.exp(s_new - m)
        l = jnp.sum(p_old, axis=1, keepdims=True) + jnp.sum(p_new, axis=1, keepdims=True)
        acc = (jnp.dot(p_old.astype(BF16), vc_ref[0, head_rows, :].astype(BF16), preferred_element_type=F32)
               + jnp.dot(p_new.astype(BF16), vn_ref[:, cols].astype(BF16), preferred_element_type=F32))
        outs.append((acc / l).astype(o_ref.dtype))
    o_ref[...] = jnp.concatenate(outs, axis=1)


def _fox_sample(z, row0, cache_k, cache_v, c_col, c_row, B, n, H, col_q, col_k, col_v, scale):
    dh = FOX_HEAD_DIM
    past = cache_k.shape[1] // H
    r0 = row0 // n
    width = H * dh
    assert (col_q * dh) % width == 0 and (col_k * dh) % width == 0 and (col_v * dh) % width == 0
    return pl.pallas_call(
        functools.partial(_fox_sample_kernel, past=past, n_heads=H, scale=scale),
        grid=(B,),
        in_specs=[pl.BlockSpec((n, width), lambda b: (r0 + b, col_q * dh // width)),
                  pl.BlockSpec((n, width), lambda b: (r0 + b, col_k * dh // width)),
                  pl.BlockSpec((n, width), lambda b: (r0 + b, col_v * dh // width)),
                  pl.BlockSpec((1, past * H, dh), lambda b: (b, 0, 0)),
                  pl.BlockSpec((1, past * H, dh), lambda b: (b, 0, 0)),
                  pl.BlockSpec((n, H), lambda b: (b, 0)),
                  pl.BlockSpec((1, H, past + n), lambda b: (b, 0, 0))],
        out_specs=pl.BlockSpec((n, width), lambda b: (b, 0)),
        out_shape=jax.ShapeDtypeStruct((B * n, width), BF16),
        compiler_params=_params(("arbitrary",)),
        name="fox_sample",
    )(z, z, z, cache_k, cache_v, c_col, c_row)


def _gather_rows_kernel(tok_ref, used_ref, h_hbm, o_ref, buf, sem, *, tm):
    i = pl.program_id(0)
    n = used_ref[0]

    def row_copy(blk, slot, r):
        t = tok_ref[blk * tm + r]
        return pltpu.make_async_copy(h_hbm.at[pl.ds(t, 1), :], buf.at[slot, pl.ds(r, 1), :], sem.at[slot])

    def issue(blk, slot):
        def body(r, c):
            row_copy(blk, slot, r).start()
            return c
        lax.fori_loop(0, tm, body, 0, unroll=8)

    def drain(blk, slot):
        del blk
        pltpu.make_async_copy(h_hbm.at[pl.ds(0, tm), :], buf.at[slot], sem.at[slot]).wait()

    @pl.when(jnp.logical_and(i == 0, n > 0))
    def _():
        issue(0, 0)

    @pl.when(i + 1 < n)
    def _():
        issue(i + 1, (i + 1) % 2)

    @pl.when(i < n)
    def _():
        slot = i % 2
        drain(i, slot)
        words = buf[slot]
        lo = lax.bitcast_convert_type(words << 16, F32)
        hi = lax.bitcast_convert_type(words & jnp.uint32(0xFFFF0000), F32)
        o_ref[...] = jnp.concatenate([lo, hi], axis=1).astype(o_ref.dtype)

    @pl.when(i >= n)
    def _():
        o_ref[...] = jnp.zeros(o_ref.shape, o_ref.dtype)


def _gather_rows(h_packed, slot_tok, n_used, tm):
    h = h_packed
    n_slots = slot_tok.shape[0]
    D = 2 * h.shape[1]
    return pl.pallas_call(
        functools.partial(_gather_rows_kernel, tm=tm),
        grid_spec=pltpu.PrefetchScalarGridSpec(
            num_scalar_prefetch=2,
            grid=(n_slots // tm,),
            in_specs=[pl.BlockSpec(memory_space=pl.ANY)],
            out_specs=pl.BlockSpec((tm, D), lambda i, tok, used: (i, 0)),
            scratch_shapes=[pltpu.VMEM((2, tm, D // 2), h.dtype), pltpu.SemaphoreType.DMA((2,))]),
        out_shape=jax.ShapeDtypeStruct((n_slots, D), BF16),
        compiler_params=_params(("arbitrary",)),
        name="moe_dispatch_gather",
    )(slot_tok, n_used, h)


def _grouped_kernel(blk0_ref, nblk_ref, x_hbm, *refs, n_w, tm, tn):
    w_refs = refs[:n_w]
    o_hbm = refs[n_w]
    w_bf = refs[n_w + 1:2 * n_w + 1]
    xbuf, obuf, tbuf, primed_ref, pending_ref, in_sem, out_sem, tail_sem = refs[2 * n_w + 1:]
    col = pl.program_id(0)
    e = pl.program_id(1)
    n_col = pl.num_programs(0)
    n_exp = pl.num_programs(1)
    nblk = nblk_ref[e]
    blk0 = blk0_ref[e]
    npair = nblk // 2

    def block_copy(first, j, slot):
        return pltpu.make_async_copy(x_hbm.at[pl.ds((first + j) * tm, tm), :], xbuf.at[slot], in_sem.at[slot])

    def x_copy(j, slot):
        return block_copy(blk0, j, slot)

    def pair_base(p):
        return jnp.where(p == 0, PRIME_SLOT, (p % 2) * 2)

    def prime(first, count):
        block_copy(first, 0, PRIME_SLOT).start()

        @pl.when(count > 1)
        def _():
            block_copy(first, 1, PRIME_SLOT + 1).start()

    @pl.when(jnp.logical_and(col == 0, e == 0))
    def _():
        primed_ref[0] = 0
        for k in range(3):
            pending_ref[k] = 0

    def settle_pair(slot):
        @pl.when(pending_ref[slot] == 1)
        def _():
            pltpu.make_async_copy(obuf.at[slot], o_hbm.at[pl.ds(0, 2 * tm), pl.ds(0, tn)], out_sem.at[slot]).wait()
            pending_ref[slot] = 0

    def settle_tail():
        @pl.when(pending_ref[2] == 1)
        def _():
            pltpu.make_async_copy(tbuf, o_hbm.at[pl.ds(0, tm), pl.ds(0, tn)], tail_sem).wait()
            pending_ref[2] = 0

    has_next = jnp.logical_or(e + 1 < n_exp, col + 1 < n_col)
    e_next = jnp.where(e + 1 < n_exp, e + 1, 0)
    nblk_next = nblk_ref[e_next]
    prime_next = jnp.logical_and(has_next, nblk_next > 0)

    def request_next():
        @pl.when(prime_next)
        def _():
            prime(blk0_ref[e_next], nblk_next)

    def pair_out(p, slot):
        return pltpu.make_async_copy(obuf.at[slot],
                                     o_hbm.at[pl.ds((blk0 + 2 * p) * tm, 2 * tm), pl.ds(col * tn, tn)],
                                     out_sem.at[slot])

    def tail_out():
        return pltpu.make_async_copy(tbuf, o_hbm.at[pl.ds((blk0 + nblk - 1) * tm, tm), pl.ds(col * tn, tn)],
                                     tail_sem)

    def apply(x):
        if n_w == 2:
            a = jnp.dot(x, w_bf[0][...], preferred_element_type=F32)
            b = jnp.dot(x, w_bf[1][...], preferred_element_type=F32)
            return a * _sigmoid_tanh(a) * b
        return jnp.dot(x, w_bf[0][...], preferred_element_type=F32)

    @pl.when(nblk > 0)
    def _():
        @pl.when(primed_ref[0] == 0)
        def _():
            prime(blk0, nblk)

        for w_ref, dst in zip(w_refs, w_bf):
            dst[...] = w_ref[0].astype(BF16)

        def pair_body(p, carry):
            s0 = pair_base(p)
            s_next = ((p + 1) % 2) * 2

            @pl.when(2 * p + 2 < nblk)
            def _():
                x_copy(2 * p + 2, s_next).start()

            @pl.when(2 * p + 3 < nblk)
            def _():
                x_copy(2 * p + 3, s_next + 1).start()

            x_copy(2 * p, s0).wait()
            x_copy(2 * p + 1, s0 + 1).wait()
            y = apply(xbuf[pl.ds(s0, 2)].reshape(2 * tm, xbuf.shape[2]))
            oslot = p % 2

            @pl.when(p >= 2)
            def _():
                pair_out(p - 2, oslot).wait()

            @pl.when(p < 2)
            def _():
                settle_pair(oslot)

            obuf[oslot] = y.astype(obuf.dtype)
            pair_out(p, oslot).start()

            @pl.when(p == 0)
            def _():
                request_next()

            return carry

        lax.fori_loop(0, npair, pair_body, 0)

        @pl.when(npair >= 2)
        def _():
            pending_ref[npair % 2] = 1

        @pl.when(npair >= 1)
        def _():
            pending_ref[(npair - 1) % 2] = 1

        @pl.when(nblk % 2 == 1)
        def _():
            slot = pair_base(npair)
            x_copy(nblk - 1, slot).wait()
            y = apply(xbuf[slot])
            settle_tail()
            tbuf[...] = y.astype(tbuf.dtype)
            tail_out().start()
            pending_ref[2] = 1

            @pl.when(npair == 0)
            def _():
                request_next()

    @pl.when(nblk == 0)
    def _():
        request_next()

    primed_ref[0] = prime_next.astype(jnp.int32)

    @pl.when(jnp.logical_not(has_next))
    def _():
        settle_pair(0)
        settle_pair(1)

    @pl.when(e == n_exp - 1)
    def _():
        first = blk0 + nblk
        total = o_hbm.shape[0] // tm
        settle_tail()

        def zero_copy(j):
            return pltpu.make_async_copy(tbuf, o_hbm.at[pl.ds(j * tm, tm), pl.ds(col * tn, tn)], tail_sem)

        @pl.when(first < total)
        def _():
            tbuf[...] = jnp.zeros(tbuf.shape, tbuf.dtype)

        def start(j, c):
            zero_copy(j).start()
            return c

        def wait(j, c):
            zero_copy(j).wait()
            return c

        lax.fori_loop(first, total, start, 0)
        lax.fori_loop(first, total, wait, 0)


def _grouped_matmul(x, weights, blk0, nblk, tm, tn_pref, out_dtype, name):
    n_rows, K = x.shape
    E, _, N = weights[0].shape
    n_w = len(weights)
    tn = _pick(N, tn_pref, LANES)
    return pl.pallas_call(
        functools.partial(_grouped_kernel, n_w=n_w, tm=tm, tn=tn),
        grid_spec=pltpu.PrefetchScalarGridSpec(
            num_scalar_prefetch=2,
            grid=(N // tn, E),
            in_specs=[pl.BlockSpec(memory_space=pl.ANY)]
            + [pl.BlockSpec((1, K, tn), lambda c, e, b0, nb: (e, 0, c)) for _ in weights],
            out_specs=pl.BlockSpec(memory_space=pl.ANY),
            scratch_shapes=[pltpu.VMEM((K, tn), BF16) for _ in weights] + [
                pltpu.VMEM((PRIME_SLOT + 2, tm, K), x.dtype),
                pltpu.VMEM((2, 2 * tm, tn), out_dtype),
                pltpu.VMEM((tm, tn), out_dtype),
                pltpu.SMEM((1,), jnp.int32),
                pltpu.SMEM((3,), jnp.int32),
                pltpu.SemaphoreType.DMA((PRIME_SLOT + 2,)),
                pltpu.SemaphoreType.DMA((2,)),
                pltpu.SemaphoreType.DMA(())]),
        out_shape=jax.ShapeDtypeStruct((n_rows, N), out_dtype),
        compiler_params=_params(("arbitrary", "arbitrary")),
        name=name,
    )(blk0, nblk, x, *weights)


def _combine_kernel(dest_ref, x_ref, sh_ref, gate_ref, g_ref, rows_hbm, o_ref, buf, sem, *, tc, final_norm,
                    row0):
    i = pl.program_id(0)
    n = pl.num_programs(0)

    def row_copy(blk, slot, r, k):
        src = dest_ref[(row0 + blk * tc + r) * TOP_K + k]
        return pltpu.make_async_copy(rows_hbm.at[pl.ds(src, 1), :], buf.at[slot, k, pl.ds(r, 1), :],
                                     sem.at[slot])

    def issue(blk, slot):
        def body(r, c):
            for k in range(TOP_K):
                row_copy(blk, slot, r, k).start()
            return c
        lax.fori_loop(0, tc, body, 0)

    def drain(blk, slot):
        del blk
        for k in range(TOP_K):
            pltpu.make_async_copy(rows_hbm.at[pl.ds(0, tc), :], buf.at[slot, k], sem.at[slot]).wait()

    @pl.when(i == 0)
    def _():
        issue(0, 0)

    @pl.when(i + 1 < n)
    def _():
        issue(i + 1, (i + 1) % 2)

    slot = i % 2
    drain(i, slot)
    y = x_ref[...] + sh_ref[...]
    gates = gate_ref[...]
    for k in range(TOP_K):
        y = y + gates[:, k:k + 1] * buf[slot, k]
    o_ref[...] = _rms(y, g_ref[...]) if final_norm else y


def _combine(x, shared, gates, dest, rows, g_final, final_norm, row0, n_rows, tc=32):
    D = x.shape[1]
    tc = _pick(math.gcd(row0, n_rows) if row0 else n_rows, tc)
    b0 = row0 // tc
    return pl.pallas_call(
        functools.partial(_combine_kernel, tc=tc, final_norm=final_norm, row0=row0),
        grid_spec=pltpu.PrefetchScalarGridSpec(
            num_scalar_prefetch=1,
            grid=(n_rows // tc,),
            in_specs=[pl.BlockSpec((tc, D), lambda i, d: (b0 + i, 0)),
                      pl.BlockSpec((tc, D), lambda i, d: (b0 + i, 0)),
                      pl.BlockSpec((tc, TOP_K), lambda i, d: (b0 + i, 0)),
                      pl.BlockSpec((1, D), lambda i, d: (0, 0)),
                      pl.BlockSpec(memory_space=pl.ANY)],
            out_specs=pl.BlockSpec((tc, D), lambda i, d: (i, 0)),
            scratch_shapes=[pltpu.VMEM((2, TOP_K, tc, D), F32), pltpu.SemaphoreType.DMA((2,))]),
        out_shape=jax.ShapeDtypeStruct((n_rows, D), F32),
        compiler_params=_params(("arbitrary",)),
        name="moe_combine_norm",
    )(dest, x, shared, gates, g_final.reshape(1, D), rows)


def _slot_assign_kernel(start_ref, idx_ref, rank_ref, dest_ref, *, n_experts):
    idx = idx_ref[...]
    base = jnp.zeros(idx.shape, jnp.int32)
    for e in range(n_experts):
        base = jnp.where(idx == e, start_ref[e], base)
    dest_ref[...] = base + rank_ref[...]


def _routing_tables(idx, rank, counts, n_experts, tm):
    T = idx.shape[0]
    n_assign = T * TOP_K
    n_blocks = (n_assign + n_experts * (tm - 1) + tm - 1) // tm
    counts = counts.reshape(n_experts).astype(jnp.int32)
    padded = (counts + tm - 1) // tm * tm
    pad_end = jnp.cumsum(padded)
    pad_start = (pad_end - padded).astype(jnp.int32)
    shape2d = (n_assign // LANES, LANES) if n_assign % LANES == 0 else (T, TOP_K)
    dest = pl.pallas_call(
        functools.partial(_slot_assign_kernel, n_experts=n_experts),
        grid_spec=pltpu.PrefetchScalarGridSpec(
            num_scalar_prefetch=1, grid=(1,),
            in_specs=[pl.BlockSpec(shape2d, lambda i, s: (0, 0)), pl.BlockSpec(shape2d, lambda i, s: (0, 0))],
            out_specs=pl.BlockSpec(shape2d, lambda i, s: (0, 0))),
        out_shape=jax.ShapeDtypeStruct(shape2d, jnp.int32),
        name="moe_slot_assign",
    )(pad_start, idx.reshape(shape2d), rank.reshape(shape2d)).reshape(n_assign)
    slot_tok = jnp.zeros((n_blocks * tm,), jnp.int32).at[dest].set(
        jnp.arange(n_assign, dtype=jnp.int32) // TOP_K)
    return dest, slot_tok, (pad_start // tm).astype(jnp.int32), (padded // tm).astype(jnp.int32)


def kernel(x_prompt, x_sample, cache_k, cache_v, cache_logf, state_ssm_re, state_ssm_im, norm_mix, w_in, b_forget, ssm_a_re, ssm_a_im, ssm_log_dt, ssm_b_re, ssm_b_im, ssm_c_re, ssm_c_im, ssm_d, w_glu, w_fox_out, w_out, norm_ffn, w_router, router_bias, w_exp_gate, w_exp_up, w_exp_down, w_sh_gate, w_sh_up, w_sh_down, norm_final):
    Bp, Lp, D = x_prompt.shape
    Bs, Ls, _ = x_sample.shape
    depth = w_in.shape[0]
    past = cache_k.shape[2]
    H = cache_k.shape[3]
    dh = cache_k.shape[4]
    assert dh == FOX_HEAD_DIM
    d_fox = H * dh
    d_ssm = ssm_d.shape[1]
    n_experts = w_router.shape[2]
    Tp = Bp * Lp
    Ts = Bs * Ls
    T = Tp + Ts
    scale = dh ** -0.5
    n_uqkv = d_ssm + 3 * d_fox
    col_q = d_ssm // dh
    col_k = col_q + H
    col_v = col_k + H
    moe_tm = 256

    x_parts = (x_prompt.reshape(Tp, D), x_sample.reshape(Ts, D))
    outs = {name: [] for name in ("kp", "vp", "fp", "rp", "ip", "ks", "vs", "fs", "rs", "is")}
    for l in range(depth):
        wt_in = jnp.swapaxes(w_in[l], 0, 1)
        wt_f = wt_in[n_uqkv:n_uqkv + H]

        h, logf = _norm_forget(x_parts, norm_mix[l], wt_f, b_forget[l])
        z = _matmul_wt(h, wt_in, 0, n_uqkv, F32, name="in_proj")
        sig_gates = _matmul_wt(h, wt_in, n_uqkv + H, 2 * D, BF16, sigmoid=True, name="gate_proj")

        lam_re, lam_im, bb_re, bb_im = _s5_discretise(ssm_a_re[l], ssm_a_im[l], ssm_log_dt[l],
                                                      ssm_b_re[l], ssm_b_im[l])
        s5_w = _s5_pack_weights(lam_re, lam_im, bb_re, bb_im, ssm_c_re[l], ssm_c_im[l])
        zero_state = jnp.zeros((Bp,) + state_ssm_re.shape[2:], F32)
        g_p, st_p = _s5_mixer(z, 0, Bp, 1, Lp, s5_w, ssm_d[l], _pack_state(zero_state, zero_state),
                              lc=_pick(Lp, 256))
        g_s, st_s = _s5_mixer(z, Tp, 1, Bs, Ls, s5_w, ssm_d[l],
                              _pack_state(state_ssm_re[l].astype(F32), state_ssm_im[l].astype(F32)), lc=Ls)
        g_parts = (g_p.reshape(Tp, d_ssm), g_s.reshape(Ts, d_ssm))

        logf_p = logf[:Tp].reshape(Bp, Lp, H)
        logf_s = logf[Tp:].reshape(Bs, Ls, H)
        crow_p = _cumsum_last(jnp.swapaxes(logf_p, 1, 2))
        ccol_p = jnp.swapaxes(crow_p, 1, 2).reshape(Tp, H)
        attn_p = _fox_prompt(z, ccol_p, crow_p, Bp, Lp, H, col_q, col_k, col_v, scale)
        lf_all = jnp.concatenate([cache_logf[l].astype(F32), logf_s], axis=1)
        crow_s = _cumsum_last(jnp.swapaxes(lf_all, 1, 2))
        ccol_s = jnp.swapaxes(crow_s[:, :, past:], 1, 2).reshape(Ts, H)
        attn_s = _fox_sample(z, Tp, cache_k[l].reshape(Bs, past * H, dh), cache_v[l].reshape(Bs, past * H, dh),
                             ccol_s, crow_s, Bs, Ls, H, col_q, col_k, col_v, scale)

        merged = _merge(g_parts, (attn_p, attn_s), w_glu[l], w_fox_out[l], sig_gates)
        x = _matmul_residual(merged, w_out[l].astype(BF16), x_parts)

        norm_out = norm_ffn[l]
        h2, h2p, idx, gates, rank, counts = _norm_router(x, norm_out, w_router[l], router_bias[l])
        dest, slot_tok, e_blk0, e_nblk = _routing_tables(idx, rank, counts, n_experts, moe_tm)
        xs = _gather_rows(h2p, slot_tok, (e_blk0[-1:] + e_nblk[-1:]), moe_tm)
        act = _grouped_matmul(xs, (w_exp_gate[l], w_exp_up[l]), e_blk0, e_nblk, moe_tm, 512, BF16, "expert_up")
        routed = _grouped_matmul(act, (w_exp_down[l],), e_blk0, e_nblk, moe_tm, 2048, F32, "expert_down")
        sh_tm = _pick(T, moe_tm, 16)
        sh_blk0 = jnp.zeros((1,), jnp.int32)
        sh_nblk = jnp.full((1,), T // sh_tm, jnp.int32)
        sh_act = _grouped_matmul(h2, (w_sh_gate[l][None], w_sh_up[l][None]), sh_blk0, sh_nblk, sh_tm, 512, BF16,
                                 "shared_up")
        shared = _grouped_matmul(sh_act, (w_sh_down[l][None],), sh_blk0, sh_nblk, sh_tm, 2048, F32,
                                 "shared_down")
        if l == depth - 1:
            y_p = _combine(x, shared, gates, dest, routed, norm_final, True, 0, Tp)
            y_s = _combine(x, shared, gates, dest, routed, norm_final, True, Tp, Ts)
        else:
            x_parts = (_combine(x, shared, gates, dest, routed, norm_final, False, 0, T),)

        k_all = z[:, d_ssm + d_fox:d_ssm + 2 * d_fox]
        v_all = z[:, d_ssm + 2 * d_fox:d_ssm + 3 * d_fox]
        re_p, im_p = _unpack_state(st_p)
        re_s, im_s = _unpack_state(st_s)
        outs["kp"].append(k_all[:Tp].reshape(Bp, Lp, H, dh))
        outs["vp"].append(v_all[:Tp].reshape(Bp, Lp, H, dh))
        outs["fp"].append(logf_p)
        outs["rp"].append(re_p)
        outs["ip"].append(im_p)
        outs["ks"].append(k_all[Tp:].reshape(Bs, Ls, H, dh))
        outs["vs"].append(v_all[Tp:].reshape(Bs, Ls, H, dh))
        outs["fs"].append(logf_s)
        outs["rs"].append(re_s)
        outs["is"].append(im_s)

    return (y_p.reshape(Bp, Lp, D), y_s.reshape(Bs, Ls, D),
            jnp.stack(outs["kp"]), jnp.stack(outs["vp"]), jnp.stack(outs["fp"]),
            jnp.stack(outs["rp"]), jnp.stack(outs["ip"]),
            jnp.stack(outs["ks"]), jnp.stack(outs["vs"]), jnp.stack(outs["fs"]),
            jnp.stack(outs["rs"]), jnp.stack(outs["is"]))
```
